```python
import math
import jax, jax.numpy as jnp
from jax import lax
import numpy as np

D_MODEL = 1024
BATCH = 16
SEQ = 4096
DEPTH = 1
DEC_BATCH = 128
DEC_SEQ = 1
PAST_LEN = 8192
PAGE_SIZE = 128

ATT_GROUPS = ((128, 1), (512, 4), (2048, 16))
N_GROUPS_A = 3
HEADS_PER_GROUP = 8
HEAD_DIM = 64
ATT_W = HEADS_PER_GROUP * HEAD_DIM
Q_COLS = N_GROUPS_A * ATT_W
ROPE_THETA = 10000.0
D_INNER = 2 * D_MODEL
SSD_HEAD_DIM = 64
SSD_HEADS = D_INNER // SSD_HEAD_DIM
SSD_GROUPS = 8
SSD_HPG = SSD_HEADS // SSD_GROUPS
D_STATE = 128
CONV_W = 4
CONV_CH = D_INNER + 2 * SSD_GROUPS * D_STATE
SSD_CHUNK = 64
N_KEYS = 128
N_EXPERTS = N_KEYS * N_KEYS
PEER_HEADS = 8
PEER_DKEY = 256
PEER_TOPK = 16
PEER_BLOCK = 256
PLE_DIM = 256
EPS = 1e-6

kernel_name = 'hybrid_dilated_ssd_peer_step'


def rmsnorm(x, g):
    xf = x.astype(jnp.float32)
    y = xf * lax.rsqrt(jnp.mean(xf * xf, axis=-1, keepdims=True) + EPS)
    return (y * g.astype(jnp.float32)).astype(x.dtype)


def rope(x, pos):
    half = HEAD_DIM // 2
    inv = jnp.exp(jnp.arange(half, dtype=jnp.float32) * (-2.0 * math.log(ROPE_THETA) / HEAD_DIM))
    ang = pos.astype(jnp.float32)[:, None] * inv[None, :]
    shape = (1, pos.shape[0]) + (1,) * (x.ndim - 3) + (half,)
    cos, sin = jnp.cos(ang).reshape(shape), jnp.sin(ang).reshape(shape)
    xf = x.astype(jnp.float32)
    x1, x2 = xf[..., :half], xf[..., half:]
    return jnp.concatenate([x1 * cos - x2 * sin, x2 * cos + x1 * sin], axis=-1).astype(x.dtype)


def dilated_attn_prompt(q, k, v, window, dil):
    b, s, h, c = q.shape
    nk = window // dil
    span = nk * dil
    sp = -(-s // span) * span
    nb = sp // span
    padw = ((0, 0), (0, sp - s), (0, 0), (0, 0))
    blk = lambda t: jnp.pad(t, padw).reshape(b, nb, nk, dil, h, c)
    qb, kb, vb = blk(q), blk(k), blk(v)
    prev = lambda t: jnp.concatenate([jnp.zeros_like(t[:, :1]), t[:, :-1]], axis=1)
    kk = jnp.concatenate([prev(kb), kb], axis=2)
    vv = jnp.concatenate([prev(vb), vb], axis=2)
    sc = jnp.einsum('bnirhc,bnjrhc->bnrhij', qb, kk, preferred_element_type=jnp.float32) * (c ** -0.5)
    i_idx = jnp.arange(nk)[:, None]
    j_idx = jnp.arange(2 * nk)[None, :]
    delta = nk + i_idx - j_idx
    band = (delta >= 0) & (delta <= nk)
    has_prev = (jnp.arange(nb)[:, None] > 0) | (jnp.arange(2 * nk)[None, :] >= nk)
    mask = band[None] & has_prev[:, None, :]
    sc = jnp.where(mask[None, :, None, None], sc, -jnp.inf)
    m = jnp.max(sc, axis=-1, keepdims=True)
    e = jnp.exp(sc - m)
    den = jnp.sum(e, axis=-1)
    o = jnp.einsum('bnrhij,bnjrhc->bnirhc', e, vv.astype(jnp.float32))
    o = o / jnp.transpose(den, (0, 1, 4, 2, 3))[..., None]
    lse = jnp.transpose(m[..., 0] + jnp.log(den), (0, 1, 4, 2, 3))
    return o.reshape(b, sp, h, c)[:, :s], lse.reshape(b, sp, h)[:, :s]


def dilated_attn_step(q, k_all, v_all, n_past, window, dil):
    t = q.shape[1]
    nk = window // dil
    idx = n_past + jnp.arange(t)[:, None] - dil * jnp.arange(nk + 1)[None, :]
    valid = idx >= 0
    idxc = jnp.clip(idx, 0)
    kg = k_all[:, idxc]
    vg = v_all[:, idxc]
    sc = jnp.einsum('bthc,btmhc->bthm', q, kg, preferred_element_type=jnp.float32) * (q.shape[-1] ** -0.5)
    sc = jnp.where(valid[None, :, None, :], sc, -jnp.inf)
    m = jnp.max(sc, axis=-1, keepdims=True)
    e = jnp.exp(sc - m)
    den = jnp.sum(e, axis=-1)
    o = jnp.einsum('bthm,btmhc->bthc', e, vg.astype(jnp.float32)) / den[..., None]
    return o, m[..., 0] + jnp.log(den)


def ssd_scan(xs, dt, A, Bm, Cm, h0, chunk):
    b, l = xs.shape[:2]
    nc = l // chunk
    r = lambda t: t.reshape((b, nc, chunk) + t.shape[2:])
    xdt = r(xs * dt[..., None])
    acs = jnp.cumsum(r(dt * A), axis=2)
    Bc, Cc = r(Bm), r(Cm)
    tril = jnp.tril(jnp.ones((chunk, chunk), dtype=bool))
    diff = acs[:, :, :, None] - acs[:, :, None, :]
    lmat = jnp.exp(jnp.where(tril[None, None, :, :, None, None], diff, -jnp.inf))
    cb = jnp.einsum('bcign,bcjgn->bcijg', Cc, Bc)
    y_diag = jnp.einsum('bcijge,bcjgep->bcigep', cb[..., None] * lmat, xdt)
    decay = jnp.exp(acs[:, :, -1:] - acs)
    states = jnp.einsum('bcjgn,bcjgep->bcgepn', Bc, xdt * decay[..., None])
    chunk_decay = jnp.exp(acs[:, :, -1])

    def step(hc, inp):
        st, dec = inp
        return hc * dec[..., None, None] + st, hc

    h_last, h_prev = lax.scan(step, h0, (jnp.moveaxis(states, 1, 0), jnp.moveaxis(chunk_decay, 1, 0)))
    h_prev = jnp.moveaxis(h_prev, 0, 1)
    y_off = jnp.einsum('bcign,bcgepn->bcigep', Cc, h_prev) * jnp.exp(acs)[..., None]
    y = (y_diag + y_off).reshape((b, l) + xs.shape[2:])
    return y, h_last


def mixer(h, pos, kv_bufs, conv_state, ssm_state, w_in, conv_w, conv_b, dt_bias, a_log, d_skip,
          ssd_norm, w_att_out, w_ssd_out, w_out):
    b, l, _ = h.shape
    proj = jnp.einsum('bld,de->ble', h, w_in)
    sizes = (Q_COLS, Q_COLS, Q_COLS, D_INNER, CONV_CH, SSD_HEADS, D_MODEL)
    cuts = [sum(sizes[:i + 1]) for i in range(len(sizes) - 1)]
    q, k, v, z, xbc, dt_raw, gates = jnp.split(proj, cuts, axis=-1)
    grp = lambda t: t.reshape(b, l, N_GROUPS_A, HEADS_PER_GROUP, HEAD_DIM)
    q, k, v = rope(grp(q), pos), rope(grp(k), pos), grp(v)
    outs, lses, new_kv = [], [], []
    for g, (win, dil) in enumerate(ATT_GROUPS):
        qg, kg, vg = q[:, :, g], k[:, :, g], v[:, :, g]
        kv_new = jnp.stack([kg, vg], axis=2)
        if kv_bufs is None:
            o, lse = dilated_attn_prompt(qg, kg, vg, win, dil)
            new_kv.append(kv_new[:, l - min(win, l):])
        else:
            buf = kv_bufs[g]
            kv_all = jnp.concatenate([buf.astype(kv_new.dtype), kv_new], axis=1)
            o, lse = dilated_attn_step(qg, kv_all[:, :, 0], kv_all[:, :, 1], buf.shape[1], win, dil)
            new_kv.append(kv_new)
        outs.append(o)
        lses.append(lse)
    wts = jax.nn.softmax(jnp.stack(lses), axis=0)
    att = jnp.sum(wts[..., None] * jnp.stack(outs), axis=0).reshape(b, l, ATT_W).astype(h.dtype)
    out_a = jnp.einsum('blc,cd->bld', att, w_att_out)
    if conv_state is None:
        conv_state = jnp.zeros((b, CONV_W - 1, CONV_CH), xbc.dtype)
    if ssm_state is None:
        ssm_state = jnp.zeros((b, SSD_HEADS, SSD_HEAD_DIM, D_STATE), jnp.float32)
    xpad = jnp.concatenate([conv_state.astype(xbc.dtype), xbc], axis=1)
    conv = conv_b
    for j in range(CONV_W):
        conv = conv + xpad[:, j:j + l] * conv_w[j]
    xc = jax.nn.silu(conv)
    new_conv = xpad[:, xpad.shape[1] - (CONV_W - 1):]
    xs, Bm, Cm = jnp.split(xc, [D_INNER, D_INNER + SSD_GROUPS * D_STATE], axis=-1)
    xs5 = xs.reshape(b, l, SSD_GROUPS, SSD_HPG, SSD_HEAD_DIM).astype(jnp.float32)
    Bm = Bm.reshape(b, l, SSD_GROUPS, D_STATE).astype(jnp.float32)
    Cm = Cm.reshape(b, l, SSD_GROUPS, D_STATE).astype(jnp.float32)
    dt = jax.nn.softplus(dt_raw.astype(jnp.float32) + dt_bias.astype(jnp.float32))
    dt = dt.reshape(b, l, SSD_GROUPS, SSD_HPG)
    A = -jnp.exp(a_log.astype(jnp.float32)).reshape(SSD_GROUPS, SSD_HPG)
    h0 = ssm_state.astype(jnp.float32).reshape(b, SSD_GROUPS, SSD_HPG, SSD_HEAD_DIM, D_STATE)
    y, h_last = ssd_scan(xs5, dt, A, Bm, Cm, h0, math.gcd(l, SSD_CHUNK))
    y = y + d_skip.astype(jnp.float32).reshape(SSD_GROUPS, SSD_HPG)[:, :, None] * xs5
    y = y.reshape(b, l, D_INNER) * jax.nn.silu(z.astype(jnp.float32))
    y = rmsnorm(y, ssd_norm).astype(h.dtype)
    out_b = jnp.einsum('ble,ed->bld', y, w_ssd_out)
    ga, gb = jnp.split(gates, 2, axis=-1)
    merged = jax.nn.sigmoid(ga) * out_a + jax.nn.sigmoid(gb) * out_b
    new_ssm = h_last.reshape(b, SSD_HEADS, SSD_HEAD_DIM, D_STATE).astype(h.dtype)
    return jnp.einsum('bld,de->ble', merged, w_out), new_kv, new_conv, new_ssm


def peer(h, w_peer_q, peer_keys1, peer_keys2, peer_u, peer_v):
    b, l, d = h.shape
    x = h.reshape(b * l, d)
    nt = x.shape[0]
    q = jnp.einsum('td,de->te', x, w_peer_q).reshape(nt, PEER_HEADS, 2, PEER_DKEY // 2)
    s1 = jnp.einsum('thc,hkc->thk', q[:, :, 0], peer_keys1, preferred_element_type=jnp.float32)
    s2 = jnp.einsum('thc,hkc->thk', q[:, :, 1], peer_keys2, preferred_element_type=jnp.float32)
    v1, i1 = lax.top_k(s1, PEER_TOPK)
    v2, i2 = lax.top_k(s2, PEER_TOPK)
    cand = (v1[..., :, None] + v2[..., None, :]).reshape(nt, PEER_HEADS, PEER_TOPK * PEER_TOPK)
    cidx = (i1[..., :, None] * N_KEYS + i2[..., None, :]).reshape(nt, PEER_HEADS, PEER_TOPK * PEER_TOPK)
    sc, sel = lax.top_k(cand, PEER_TOPK)
    eidx = jnp.take_along_axis(cidx, sel, axis=-1)
    gate = jax.nn.softmax(sc, axis=-1)
    blk = math.gcd(nt, PEER_BLOCK)
    nbk = nt // blk

    def block_fn(args):
        xb, eb, gb = args
        u = peer_u[eb]
        act = jax.nn.gelu(jnp.einsum('thkd,td->thk', u, xb, preferred_element_type=jnp.float32))
        return jnp.einsum('thk,thkd->td', (gb * act).astype(xb.dtype), peer_v[eb])

    out = lax.map(block_fn, (x.reshape(nbk, blk, d),
                             eidx.reshape(nbk, blk, PEER_HEADS, PEER_TOPK),
                             gate.reshape(nbk, blk, PEER_HEADS, PEER_TOPK)))
    return out.reshape(b, l, d)


def trunk(x, p, pos, kv_bufs, conv_state, ssm_state, layer_w, norm_final):
    (norm_mix, w_in, conv_w, conv_b, dt_bias, a_log, d_skip, ssd_norm, w_att_out, w_ssd_out, w_out,
     norm_ffn, w_peer_q, peer_keys1, peer_keys2, peer_u, peer_v, norm_ple, w_ple_gate, w_ple_proj) = layer_w
    kv_out = [[] for _ in ATT_GROUPS]
    conv_out, ssm_out = [], []
    for i in range(DEPTH):
        bufs = None if kv_bufs is None else tuple(c[i] for c in kv_bufs)
        cs = None if conv_state is None else conv_state[i]
        ss = None if ssm_state is None else ssm_state[i]
        mo, kv, nc, ns = mixer(rmsnorm(x, norm_mix[i]), pos, bufs, cs, ss, w_in[i], conv_w[i], conv_b[i],
                               dt_bias[i], a_log[i], d_skip[i], ssd_norm[i], w_att_out[i], w_ssd_out[i], w_out[i])
        x = x + mo
        x = x + peer(rmsnorm(x, norm_ffn[i]), w_peer_q[i], peer_keys1[i], peer_keys2[i], peer_u[i], peer_v[i])
        pg = jax.nn.sigmoid(jnp.einsum('bld,de->ble', rmsnorm(x, norm_ple[i]), w_ple_gate[i]))
        x = x + pg * jnp.einsum('blc,cd->bld', p[i].astype(x.dtype), w_ple_proj[i])
        for g in range(N_GROUPS_A):
            kv_out[g].append(kv[g])
        conv_out.append(nc)
        ssm_out.append(ns)
    kvs = [jnp.stack(t) for t in kv_out]
    return rmsnorm(x, norm_final), kvs, jnp.stack(conv_out), jnp.stack(ssm_out)


def setup_inputs(seed: int = 0) -> dict:
    key = jax.random.key(seed)
    ks = iter(jax.random.split(key, 40))
    nrm = lambda shape, scale: jax.random.normal(next(ks), shape, jnp.float32) * scale
    inp = {}
    inp['x_prompt'] = nrm((BATCH, SEQ, D_MODEL), 1.0)
    inp['x_sample'] = nrm((DEC_BATCH, DEC_SEQ, D_MODEL), 1.0)
    for win, _ in ATT_GROUPS:
        inp['cache_kv_w%d' % win] = nrm((DEPTH, DEC_BATCH, min(win, PAST_LEN), 2, HEADS_PER_GROUP, HEAD_DIM), 1.0)
    inp['state_conv'] = nrm((DEPTH, DEC_BATCH, CONV_W - 1, CONV_CH), 1.0)
    inp['state_ssm'] = nrm((DEPTH, DEC_BATCH, SSD_HEADS, SSD_HEAD_DIM, D_STATE), 0.1)
    inp['p_prompt'] = nrm((DEPTH, BATCH, SEQ, PLE_DIM), 1.0)
    inp['p_sample'] = nrm((DEPTH, DEC_BATCH, DEC_SEQ, PLE_DIM), 1.0)
    in_cols = 3 * Q_COLS + D_INNER + CONV_CH + SSD_HEADS + 2 * D_MODEL
    inp['norm_mix'] = 1.0 + nrm((DEPTH, D_MODEL), 0.02)
    inp['w_in'] = nrm((DEPTH, D_MODEL, in_cols), D_MODEL ** -0.5)
    inp['conv_w'] = nrm((DEPTH, CONV_W, CONV_CH), CONV_W ** -0.5)
    inp['conv_b'] = nrm((DEPTH, CONV_CH), 0.01)
    dt0 = jnp.exp(jax.random.uniform(next(ks), (DEPTH, SSD_HEADS), jnp.float32, math.log(1e-3), math.log(1e-1)))
    inp['dt_bias'] = dt0 + jnp.log(-jnp.expm1(-dt0))
    inp['a_log'] = jnp.log(jax.random.uniform(next(ks), (DEPTH, SSD_HEADS), jnp.float32, 1.0, 16.0))
    inp['d_skip'] = 1.0 + nrm((DEPTH, SSD_HEADS), 0.1)
    inp['ssd_norm'] = 1.0 + nrm((DEPTH, D_INNER), 0.02)
    inp['w_att_out'] = nrm((DEPTH, ATT_W, D_MODEL), ATT_W ** -0.5)
    inp['w_ssd_out'] = nrm((DEPTH, D_INNER, D_MODEL), D_INNER ** -0.5)
    inp['w_out'] = nrm((DEPTH, D_MODEL, D_MODEL), D_MODEL ** -0.5)
    inp['norm_ffn'] = 1.0 + nrm((DEPTH, D_MODEL), 0.02)
    inp['w_peer_q'] = nrm((DEPTH, D_MODEL, PEER_HEADS * PEER_DKEY), D_MODEL ** -0.5)
    inp['peer_keys1'] = nrm((DEPTH, PEER_HEADS, N_KEYS, PEER_DKEY // 2), (PEER_DKEY // 2) ** -0.5)
    inp['peer_keys2'] = nrm((DEPTH, PEER_HEADS, N_KEYS, PEER_DKEY // 2), (PEER_DKEY // 2) ** -0.5)
    inp['peer_u'] = nrm((DEPTH, N_EXPERTS, D_MODEL), D_MODEL ** -0.5)
    inp['peer_v'] = nrm((DEPTH, N_EXPERTS, D_MODEL), PEER_HEADS ** -0.5)
    inp['norm_ple'] = 1.0 + nrm((DEPTH, D_MODEL), 0.02)
    inp['w_ple_gate'] = nrm((DEPTH, D_MODEL, D_MODEL), D_MODEL ** -0.5)
    inp['w_ple_proj'] = nrm((DEPTH, PLE_DIM, D_MODEL), PLE_DIM ** -0.5)
    inp['norm_final'] = 1.0 + nrm((D_MODEL,), 0.02)
    return inp


def reference(x_prompt, x_sample, cache_kv_w128, cache_kv_w512, cache_kv_w2048, state_conv, state_ssm,
              p_prompt, p_sample, norm_mix, w_in, conv_w, conv_b, dt_bias, a_log, d_skip, ssd_norm,
              w_att_out, w_ssd_out, w_out, norm_ffn, w_peer_q, peer_keys1, peer_keys2, peer_u, peer_v,
              norm_ple, w_ple_gate, w_ple_proj, norm_final):
    layer_w = (norm_mix, w_in, conv_w, conv_b, dt_bias, a_log, d_skip, ssd_norm, w_att_out, w_ssd_out, w_out,
               norm_ffn, w_peer_q, peer_keys1, peer_keys2, peer_u, peer_v, norm_ple, w_ple_gate, w_ple_proj)
    pos_p = jnp.arange(x_prompt.shape[1], dtype=jnp.int32)
    pos_s = PAST_LEN + jnp.arange(x_sample.shape[1], dtype=jnp.int32)
    y_prompt, kv_p, conv_p, ssm_p = trunk(x_prompt, p_prompt, pos_p, None, None, None, layer_w, norm_final)
    y_sample, kv_s, conv_s, ssm_s = trunk(x_sample, p_sample, pos_s, (cache_kv_w128, cache_kv_w512, cache_kv_w2048),
                                          state_conv, state_ssm, layer_w, norm_final)
    return (y_prompt, y_sample, kv_p[0], kv_p[1], kv_p[2], conv_p, ssm_p, kv_s[0], kv_s[1], kv_s[2], conv_s, ssm_s)
```

```python
import functools
import math

import jax
import jax.numpy as jnp
from jax import lax
from jax.experimental import pallas as pl
from jax.experimental.pallas import tpu as pltpu

F32 = jnp.float32
BF16 = jnp.bfloat16
HIGHEST = lax.Precision.HIGHEST

LANES = 128
SUBLANES = 8
VMEM_BYTES_V7X = 64 * 1024 * 1024
VMEM_LIMIT = VMEM_BYTES_V7X * 3 // 4

D_MODEL = 1024
ATT_GROUPS = ((128, 1), (512, 4), (2048, 16))
N_GROUPS = len(ATT_GROUPS)
N_HEADS = 8
HEAD_DIM = 64
ATT_W = N_HEADS * HEAD_DIM
ATT_SLABS = ATT_W // LANES
Q_COLS = N_GROUPS * ATT_W
QKV_COLS = 3 * Q_COLS
ROPE_THETA = 10000.0
D_INNER = 2048
SSD_HEADS = 32
SSD_GROUPS = 8
SSD_GW = D_INNER // SSD_GROUPS
D_STATE = 128
CONV_W = 4
CONV_CH = D_INNER + 2 * SSD_GROUPS * D_STATE
CHUNK = 64
N_KEYS = 128
N_EXPERTS = N_KEYS * N_KEYS
PEER_HEADS = 8
PEER_TOPK = 16
PLE_DIM = 256
EPS = 1e-6
PAST_LEN = 8192
NEG = -1e30

REST_Z = CONV_CH
REST_GATES = REST_Z + D_INNER
REST_COLS = REST_GATES + 2 * D_MODEL
PROJ_TN = 512
ATT_ROWS = 2048


def _params(sem):
    return pltpu.CompilerParams(dimension_semantics=sem, vmem_limit_bytes=VMEM_LIMIT)


def _rmsnorm_rows(x, g):
    return x * lax.rsqrt(jnp.mean(x * x, axis=-1, keepdims=True) + EPS) * g


def _nt(a, b):
    return lax.dot_general(a, b, (((1,), (1,)), ((), ())), preferred_element_type=F32)


def _proj_qkv_body(x_ref, g_ref, w_ref, cos_ref, sin_ref, o_ref, xn_ref, *, rope_tiles):
    j = pl.program_id(1)

    @pl.when(j == 0)
    def _():
        xn_ref[...] = _rmsnorm_rows(x_ref[...], g_ref[...]).astype(BF16)

    acc = jnp.dot(xn_ref[...], w_ref[...], preferred_element_type=F32)
    n_slab = acc.shape[1] // LANES

    @pl.when(j < rope_tiles)
    def _():
        cos = cos_ref[...]
        sin = sin_ref[...]
        lane = lax.broadcasted_iota(jnp.int32, cos.shape, 1)
        first_half = (lane & (HEAD_DIM - 1)) < HEAD_DIM // 2
        for c in range(n_slab):
            a = acc[:, c * LANES:(c + 1) * LANES]
            partner = jnp.where(first_half, pltpu.roll(a, LANES - HEAD_DIM // 2, 1),
                                pltpu.roll(a, HEAD_DIM // 2, 1))
            o_ref[c] = a * cos + partner * sin

    @pl.when(j >= rope_tiles)
    def _():
        for c in range(n_slab):
            o_ref[c] = acc[:, c * LANES:(c + 1) * LANES]


def _proj_plain_body(x_ref, g_ref, w_ref, o_ref, xn_ref):
    @pl.when(pl.program_id(1) == 0)
    def _():
        xn_ref[...] = _rmsnorm_rows(x_ref[...], g_ref[...]).astype(BF16)

    o_ref[...] = jnp.dot(xn_ref[...], w_ref[...], preferred_element_type=F32)


def _proj_dt_body(x_ref, g_ref, w_ref, bias_ref, o_ref):
    xn = _rmsnorm_rows(x_ref[...], g_ref[...]).astype(BF16)
    v = jnp.dot(xn, w_ref[...], preferred_element_type=F32) + bias_ref[...]
    o_ref[...] = jnp.maximum(v, 0.0) + jnp.log1p(jnp.exp(-jnp.abs(v)))


def _rope_tables(pos):
    half = HEAD_DIM // 2
    inv = jnp.exp(jnp.arange(half, dtype=F32) * (-2.0 * math.log(ROPE_THETA) / HEAD_DIM))
    ang = pos.astype(F32)[:, None] * inv[None, :]
    cos, sin = jnp.cos(ang), jnp.sin(ang)
    cos_t = jnp.concatenate([cos, cos, cos, cos], axis=1)
    sin_t = jnp.concatenate([-sin, sin, -sin, sin], axis=1)
    return cos_t, sin_t


def _projections(x2d, pos, norm_mix, w_qkv, w_rest, w_dt, dt_bias_row, tm):
    t, k = x2d.shape
    tm = min(tm, t)
    cos_t, sin_t = _rope_tables(pos)
    n_pos_blocks = pos.shape[0] // tm
    x_spec = pl.BlockSpec((tm, k), lambda i, j: (i, 0))
    g_spec = pl.BlockSpec((1, k), lambda i, j: (0, 0))
    w_spec = pl.BlockSpec((k, PROJ_TN), lambda i, j: (0, j))
    tab_spec = pl.BlockSpec((tm, LANES), lambda i, j: (i % n_pos_blocks, 0))
    slabs = PROJ_TN // LANES
    qkv = pl.pallas_call(
        functools.partial(_proj_qkv_body, rope_tiles=2 * Q_COLS // PROJ_TN),
        out_shape=jax.ShapeDtypeStruct((QKV_COLS // LANES, t, LANES), F32),
        grid=(t // tm, QKV_COLS // PROJ_TN),
        in_specs=[x_spec, g_spec, w_spec, tab_spec, tab_spec],
        out_specs=pl.BlockSpec((slabs, tm, LANES), lambda i, j: (j, i, 0)),
        scratch_shapes=[pltpu.VMEM((tm, k), BF16)],
        compiler_params=_params(("parallel", "arbitrary")),
        name="proj_qkv",
    )(x2d, norm_mix, w_qkv, cos_t, sin_t)
    rest = pl.pallas_call(
        _proj_plain_body,
        out_shape=jax.ShapeDtypeStruct((t, REST_COLS), F32),
        grid=(t // tm, REST_COLS // PROJ_TN),
        in_specs=[x_spec, g_spec, w_spec],
        out_specs=pl.BlockSpec((tm, PROJ_TN), lambda i, j: (i, j)),
        scratch_shapes=[pltpu.VMEM((tm, k), BF16)],
        compiler_params=_params(("parallel", "arbitrary")),
        name="proj_rest",
    )(x2d, norm_mix, w_rest)
    dt = pl.pallas_call(
        _proj_dt_body,
        out_shape=jax.ShapeDtypeStruct((t, LANES), F32),
        grid=(t // tm,),
        in_specs=[pl.BlockSpec((tm, k), lambda i: (i, 0)), pl.BlockSpec((1, k), lambda i: (0, 0)),
                  pl.BlockSpec((k, LANES), lambda i: (0, 0)), pl.BlockSpec((1, LANES), lambda i: (0, 0))],
        out_specs=pl.BlockSpec((tm, LANES), lambda i: (i, 0)),
        compiler_params=_params(("parallel",)),
        name="proj_dt",
    )(x2d, norm_mix, w_dt, dt_bias_row)
    return qkv, rest, dt


def _attn_prompt_body(*refs, dil, has_prev_group, emit_lse):
    q_ref, kc_ref, kp_ref, vc_ref, vp_ref = refs[:5]
    pos = 5
    if has_prev_group:
        op_ref, lp_ref = refs[pos:pos + 2]
        pos += 2
    o_ref = refs[pos]
    pos += 1
    if emit_lse:
        l_ref = refs[pos]
        pos += 1
    kbuf, vbuf = refs[pos:pos + 2]

    i = pl.program_id(1)
    rows = q_ref.shape[0]
    span = LANES * dil
    kbuf[0:rows, :] = kp_ref[...]
    kbuf[rows:2 * rows, :] = kc_ref[...]
    vbuf[0:rows, :] = vp_ref[...]
    vbuf[rows:2 * rows, :] = vc_ref[...]

    row = lax.broadcasted_iota(jnp.int32, (LANES, LANES), 0)
    col = lax.broadcasted_iota(jnp.int32, (LANES, LANES), 1)
    causal = col <= row
    band_prev = col >= row
    low_half = col < HEAD_DIM

    def rd(ref, start):
        if dil == 1:
            return ref[pl.ds(start, LANES), :]
        return ref[pl.ds(start, LANES, stride=dil), :]

    def sub_block(u, carry):
        sp = u // dil
        start = sp * span + (u % dil)
        q2 = rd(q_ref, start)
        kc2 = rd(kbuf, rows + start).astype(BF16)
        kp2 = rd(kbuf, rows - span + start).astype(BF16)
        vc2 = rd(vbuf, rows + start).astype(BF16)
        vp2 = rd(vbuf, rows - span + start).astype(BF16)
        mask_p = jnp.logical_and(band_prev, jnp.logical_or(i > 0, sp > 0))
        outs, lses = [], []
        for hh in range(2):
            head_lanes = low_half if hh == 0 else jnp.logical_not(low_half)
            qh = jnp.where(head_lanes, q2, 0.0).astype(BF16)
            sc = jnp.where(causal, _nt(qh, kc2), NEG)
            spv = jnp.where(mask_p, _nt(qh, kp2), NEG)
            m = jnp.maximum(jnp.max(sc, axis=1, keepdims=True), jnp.max(spv, axis=1, keepdims=True))
            ec = jnp.exp(sc - m)
            ep = jnp.exp(spv - m)
            den = jnp.sum(ec, axis=1, keepdims=True) + jnp.sum(ep, axis=1, keepdims=True)
            outs.append((jnp.dot(ec.astype(BF16), vc2, preferred_element_type=F32)
                         + jnp.dot(ep.astype(BF16), vp2, preferred_element_type=F32)) / den)
            lses.append(m + jnp.log(den))
        o2 = jnp.where(low_half, outs[0], outs[1])
        l2 = jnp.where(low_half, lses[0], lses[1])
        if has_prev_group:
            o_prev = rd(op_ref, start)
            l_prev = rd(lp_ref, start)
            mm = jnp.maximum(l_prev, l2)
            wp = jnp.exp(l_prev - mm)
            wc = jnp.exp(l2 - mm)
            tot = wp + wc
            o2 = (wp * o_prev + wc * o2) / tot
            l2 = mm + jnp.log(tot)
        if dil == 1:
            o_ref[pl.ds(start, LANES), :] = o2
            if emit_lse:
                l_ref[pl.ds(start, LANES), :] = l2
        else:
            o_ref[pl.ds(start, LANES, stride=dil), :] = o2
            if emit_lse:
                l_ref[pl.ds(start, LANES, stride=dil), :] = l2
        return carry

    lax.fori_loop(0, rows // LANES, sub_block, 0)


def _attn_prompt_group(qkv, g, dil, b, s, prev):
    rows = min(ATT_ROWS, s)
    assert rows % (LANES * dil) == 0 and s % rows == 0
    qkv4 = qkv.reshape(QKV_COLS // LANES, b, s, LANES)
    kslab = Q_COLS // LANES
    vslab = 2 * Q_COLS // LANES
    blk = (None, None, rows, LANES)
    in_specs = [
        pl.BlockSpec(blk, lambda bb, i, c: (g * ATT_SLABS + c, bb, i, 0)),
        pl.BlockSpec(blk, lambda bb, i, c: (kslab + g * ATT_SLABS + c, bb, i, 0)),
        pl.BlockSpec(blk, lambda bb, i, c: (kslab + g * ATT_SLABS + c, bb, jnp.maximum(i - 1, 0), 0)),
        pl.BlockSpec(blk, lambda bb, i, c: (vslab + g * ATT_SLABS + c, bb, i, 0)),
        pl.BlockSpec(blk, lambda bb, i, c: (vslab + g * ATT_SLABS + c, bb, jnp.maximum(i - 1, 0), 0)),
    ]
    args = [qkv4] * 5
    o_spec = pl.BlockSpec(blk, lambda bb, i, c: (c, bb, i, 0))
    if prev is not None:
        in_specs += [o_spec, o_spec]
        args += [prev[0].reshape(ATT_SLABS, b, s, LANES), prev[1].reshape(ATT_SLABS, b, s, LANES)]
    emit_lse = g < N_GROUPS - 1
    shape = jax.ShapeDtypeStruct((ATT_SLABS, b, s, LANES), F32)
    body = functools.partial(_attn_prompt_body, dil=dil, has_prev_group=prev is not None, emit_lse=emit_lse)
    out = pl.pallas_call(
        body,
        out_shape=(shape, shape) if emit_lse else shape,
        grid=(b, s // rows, ATT_SLABS),
        in_specs=in_specs,
        out_specs=(o_spec, o_spec) if emit_lse else o_spec,
        scratch_shapes=[pltpu.VMEM((2 * rows, LANES), F32), pltpu.VMEM((2 * rows, LANES), F32)],
        compiler_params=_params(("parallel", "parallel", "parallel")),
        name="attn_prompt_g%d" % g,
    )(*args)
    if emit_lse:
        return out[0].reshape(ATT_SLABS, b * s, LANES), out[1].reshape(ATT_SLABS, b * s, LANES)
    return out.reshape(ATT_SLABS, b * s, LANES)


def _attn_prompt(qkv, b, s):
    prev = None
    for g, (win, dil) in enumerate(ATT_GROUPS):
        assert win // dil == LANES
        prev = _attn_prompt_group(qkv, g, dil, b, s, prev)
    return prev


def _ssd_prompt_body(xbc_ref, dt_ref, cw_ref, cb_ref, ax_ref, dx_ref, e_ref, ltri_ref,
                     y_ref, st_ref, xext, h_ref, *, n_chunks):
    i = pl.program_id(1)
    lc = n_chunks * CHUNK
    halo = SUBLANES

    @pl.when(i == 0)
    def _():
        xext[0:halo, :] = jnp.zeros((halo, CONV_CH), F32)
        h_ref[...] = jnp.zeros_like(h_ref)

    xext[halo:halo + lc, :] = xbc_ref[...]

    lane_x = lax.broadcasted_iota(jnp.int32, (CHUNK, D_INNER), 1) & (CHUNK - 1)
    row_x = lax.broadcasted_iota(jnp.int32, (CHUNK, D_INNER), 0)
    diag = lane_x == row_x
    tril = row_x >= lane_x
    lane_p = lax.broadcasted_iota(jnp.int32, (CHUNK, LANES), 1)
    low_half = lane_p < CHUNK
    ax = ax_ref[...]
    dx = dx_ref[...]
    emat = e_ref[...]
    ltri = ltri_ref[...]

    for c in range(n_chunks):
        o = c * CHUNK
        conv = cb_ref[...] + xext[halo - 3 + o:halo - 3 + o + CHUNK, :] * cw_ref[0:1, :]
        conv = conv + xext[halo - 2 + o:halo - 2 + o + CHUNK, :] * cw_ref[1:2, :]
        conv = conv + xext[halo - 1 + o:halo - 1 + o + CHUNK, :] * cw_ref[2:3, :]
        conv = conv + xext[halo + o:halo + o + CHUNK, :] * cw_ref[3:4, :]
        xc = conv * jax.nn.sigmoid(conv)
        xs = xc[:, :D_INNER]
        bm = xc[:, D_INNER:D_INNER + SSD_GROUPS * D_STATE]
        cm = xc[:, D_INNER + SSD_GROUPS * D_STATE:]

        dt32 = dt_ref[o:o + CHUNK, 0:SSD_HEADS]
        dtx = jnp.dot(dt32, emat, precision=HIGHEST, preferred_element_type=F32)
        acsx = jnp.dot(ltri, dtx * ax, precision=HIGHEST, preferred_element_type=F32)
        last = acsx[CHUNK - 1:CHUNK, :]
        xdt = xs * dtx
        xw = (xdt * jnp.exp(last - acsx)).astype(BF16)
        eacs = jnp.exp(acsx)
        cdec = jnp.exp(last)
        rrow = jnp.sum(jnp.where(diag, acsx, 0.0), axis=0, keepdims=True)
        lm = jnp.exp(jnp.where(tril, acsx - rrow, NEG))
        xdtb = xdt.astype(BF16)
        cmb = cm.astype(BF16)

        for g in range(SSD_GROUPS):
            gs = slice(g * SSD_GW, (g + 1) * SSD_GW)
            bg = bm[:, g * D_STATE:(g + 1) * D_STATE]
            bgb = bg.astype(BF16)
            cg = cmb[:, g * D_STATE:(g + 1) * D_STATE]
            cbx = _nt(cg, jnp.concatenate([bgb] * (SSD_GW // CHUNK), axis=0))
            wg = (cbx * lm[:, gs]).astype(BF16)
            ydiag = []
            for pr in range(SSD_GW // LANES):
                xp = xdtb[:, g * SSD_GW + pr * LANES:g * SSD_GW + (pr + 1) * LANES]
                zero = jnp.zeros_like(xp)
                bd = jnp.concatenate([jnp.where(low_half, xp, zero), jnp.where(low_half, zero, xp)], axis=0)
                ydiag.append(jnp.dot(wg[:, pr * LANES:(pr + 1) * LANES], bd, preferred_element_type=F32))
            hp = h_ref[g]
            yoff = jnp.dot(cg, hp.astype(BF16), preferred_element_type=F32) * eacs[:, gs]
            st = jnp.dot(bg.T.astype(BF16), xw[:, gs], preferred_element_type=F32)
            h_ref[g] = hp * cdec[:, gs] + st
            y_ref[o:o + CHUNK, gs] = jnp.concatenate(ydiag, axis=1) + yoff + dx[:, gs] * xs[:, gs]

    xext[0:halo, :] = xext[lc:lc + halo, :]

    @pl.when(i == pl.num_programs(1) - 1)
    def _():
        for g in range(SSD_GROUPS):
            st_ref[g * SSD_GW:(g + 1) * SSD_GW, :] = h_ref[g].T


def _ssd_consts(a_log, d_skip):
    a_x = jnp.repeat(-jnp.exp(a_log.astype(F32)), CHUNK)[None, :]
    d_x = jnp.repeat(d_skip.astype(F32), CHUNK)[None, :]
    emat = (jnp.arange(D_INNER)[None, :] // CHUNK == jnp.arange(SSD_HEADS)[:, None]).astype(F32)
    ltri = (jnp.arange(CHUNK)[:, None] >= jnp.arange(CHUNK)[None, :]).astype(F32)
    return a_x, d_x, emat, ltri


def _ssd_prompt(rest, dt, conv_w, conv_b, a_x, d_x, emat, ltri, b, s, n_chunks=1):
    lc = n_chunks * CHUNK
    rest_v = rest.reshape(b, s, REST_COLS)
    dt_v = dt.reshape(b, s, LANES)
    const = lambda shape: pl.BlockSpec(shape, lambda bb, i: (0,) * len(shape))
    body = functools.partial(_ssd_prompt_body, n_chunks=n_chunks)
    y, st = pl.pallas_call(
        body,
        out_shape=(jax.ShapeDtypeStruct((b, s, D_INNER), F32),
                   jax.ShapeDtypeStruct((b, D_INNER, D_STATE), F32)),
        grid=(b, s // lc),
        in_specs=[pl.BlockSpec((None, lc, CONV_CH), lambda bb, i: (bb, i, 0)),
                  pl.BlockSpec((None, lc, LANES), lambda bb, i: (bb, i, 0)),
                  const((CONV_W, CONV_CH)), const((1, CONV_CH)), const((1, D_INNER)),
                  const((1, D_INNER)), const((SSD_HEADS, D_INNER)), const((CHUNK, CHUNK))],
        out_specs=(pl.BlockSpec((None, lc, D_INNER), lambda bb, i: (bb, i, 0)),
                   pl.BlockSpec((None, D_INNER, D_STATE), lambda bb, i: (bb, 0, 0))),
        scratch_shapes=[pltpu.VMEM((lc + SUBLANES, CONV_CH), F32),
                        pltpu.VMEM((SSD_GROUPS, D_STATE, SSD_GW), F32)],
        compiler_params=_params(("parallel", "arbitrary")),
        name="ssd_prompt",
    )(rest_v, dt_v, conv_w, conv_b, a_x, d_x, emat, ltri)
    return y.reshape(b * s, D_INNER), st


def _attn_step_body(q_ref, kn_ref, vn_ref, c0_ref, c1_ref, c2_ref, o_ref, *, bb):
    caches = (c0_ref, c1_ref, c2_ref)
    n_flat = LANES * N_HEADS
    row = lax.broadcasted_iota(jnp.int32, (N_HEADS, n_flat), 0)
    lane = lax.broadcasted_iota(jnp.int32, (N_HEADS, n_flat), 1)
    own_head = (lane & (N_HEADS - 1)) == row
    for bi in range(bb):
        s_list, sn_list = [], []
        for g in range(N_GROUPS):
            qb = q_ref[bi, g].astype(BF16)
            kmat = caches[g][bi, :, 0].reshape(n_flat, HEAD_DIM).astype(BF16)
            s_list.append(jnp.where(own_head, _nt(qb, kmat), NEG))
            sn_list.append(jnp.sum(qb.astype(F32) * kn_ref[bi, g].astype(BF16).astype(F32),
                                   axis=1, keepdims=True))
        m = sn_list[0]
        for g in range(N_GROUPS):
            m = jnp.maximum(m, jnp.maximum(jnp.max(s_list[g], axis=1, keepdims=True), sn_list[g]))
        den = jnp.zeros((N_HEADS, 1), F32)
        acc = jnp.zeros((N_HEADS, HEAD_DIM), F32)
        for g in range(N_GROUPS):
            e = jnp.exp(s_list[g] - m)
            en = jnp.exp(sn_list[g] - m)
            den = den + jnp.sum(e, axis=1, keepdims=True) + en
            vmat = caches[g][bi, :, 1].reshape(n_flat, HEAD_DIM).astype(BF16)
            acc = acc + jnp.dot(e.astype(BF16), vmat, preferred_element_type=F32)
            acc = acc + en.astype(BF16).astype(F32) * vn_ref[bi, g].astype(BF16).astype(F32)
        o_ref[bi] = acc / den


def _attn_step(q, kn, vn, caches, bb=4):
    nb = q.shape[0]
    views = []
    specs = []
    for (win, dil), c in zip(ATT_GROUPS, caches):
        w = c.shape[1]
        assert w == win, "cache must hold exactly one window of past rows"
        views.append(c.reshape(nb, w // dil, dil, 2, N_HEADS, HEAD_DIM))
        specs.append(pl.BlockSpec((bb, LANES, None, 2, N_HEADS, HEAD_DIM), lambda i: (i, 0, 0, 0, 0, 0)))
    qspec = pl.BlockSpec((bb, N_GROUPS, N_HEADS, HEAD_DIM), lambda i: (i, 0, 0, 0))
    return pl.pallas_call(
        functools.partial(_attn_step_body, bb=bb),
        out_shape=jax.ShapeDtypeStruct((nb, N_HEADS, HEAD_DIM), F32),
        grid=(nb // bb,),
        in_specs=[qspec, qspec, qspec, *specs],
        out_specs=pl.BlockSpec((bb, N_HEADS, HEAD_DIM), lambda i: (i, 0, 0)),
        compiler_params=_params(("parallel",)),
        name="attn_step",
    )(q, kn, vn, *views)


def _conv_step_body(rest_ref, sc_ref, dt_ref, cw_ref, cb_ref, ax_ref, e_ref,
                    xs_ref, bm_ref, cm_ref, xdt_ref, dec_ref):
    conv = cb_ref[...] + sc_ref[:, 0:CONV_CH] * cw_ref[0:1, :]
    conv = conv + sc_ref[:, CONV_CH:2 * CONV_CH] * cw_ref[1:2, :]
    conv = conv + sc_ref[:, 2 * CONV_CH:3 * CONV_CH] * cw_ref[2:3, :]
    conv = conv + rest_ref[...] * cw_ref[3:4, :]
    xc = conv * jax.nn.sigmoid(conv)
    xs = xc[:, :D_INNER]
    dtx = jnp.dot(dt_ref[:, 0:SSD_HEADS], e_ref[...], precision=HIGHEST, preferred_element_type=F32)
    xs_ref[...] = xs
    bm_ref[...] = xc[:, D_INNER:D_INNER + SSD_GROUPS * D_STATE]
    cm_ref[...] = xc[:, D_INNER + SSD_GROUPS * D_STATE:]
    xdt_ref[...] = xs * dtx
    dec_ref[...] = jnp.exp(dtx * ax_ref[...])


def _state_step_body(h_ref, xdt_ref, dec_ref, b_ref, c_ref, xs_ref, dx_ref, ho_ref, y_ref, *, bb):
    grow = lax.broadcasted_iota(jnp.int32, (SSD_GROUPS, D_INNER), 0)
    glane = lax.broadcasted_iota(jnp.int32, (SSD_GROUPS, D_INNER), 1) // SSD_GW
    gmask = grow == glane
    for bi in range(bb):
        h = h_ref[bi]
        bmat = b_ref[bi]
        bx = jnp.concatenate([jnp.broadcast_to(bmat[g:g + 1, :], (SSD_GW, D_STATE))
                              for g in range(SSD_GROUPS)], axis=0)
        hn = h * dec_ref[:, bi:bi + 1] + xdt_ref[:, bi:bi + 1] * bx
        ho_ref[bi] = hn
        y8 = _nt(c_ref[bi].astype(BF16), hn.astype(BF16))
        y = jnp.sum(jnp.where(gmask, y8, 0.0), axis=0, keepdims=True)
        y_ref[bi:bi + 1, :] = y + dx_ref[...] * xs_ref[bi:bi + 1, :]


def _ssd_step(rest, dt, state_conv, state_ssm, conv_w, conv_b, a_x, d_x, emat, bb=4):
    nb = rest.shape[0]
    nblk = nb // bb
    full = lambda shape: pl.BlockSpec(shape, lambda i: (0,) * len(shape))
    xs, bm, cm, xdt, dec = pl.pallas_call(
        _conv_step_body,
        out_shape=(jax.ShapeDtypeStruct((nb, D_INNER), F32),
                   jax.ShapeDtypeStruct((nb, SSD_GROUPS * D_STATE), F32),
                   jax.ShapeDtypeStruct((nb, SSD_GROUPS * D_STATE), F32),
                   jax.ShapeDtypeStruct((nb, D_INNER), F32),
                   jax.ShapeDtypeStruct((nb, D_INNER), F32)),
        grid=(1,),
        in_specs=[pl.BlockSpec((nb, CONV_CH), lambda i: (0, 0)),
                  full((nb, (CONV_W - 1) * CONV_CH)), full((nb, LANES)),
                  full((CONV_W, CONV_CH)), full((1, CONV_CH)), full((1, D_INNER)),
                  full((SSD_HEADS, D_INNER))],
        out_specs=(full((nb, D_INNER)), full((nb, SSD_GROUPS * D_STATE)),
                   full((nb, SSD_GROUPS * D_STATE)), full((nb, D_INNER)), full((nb, D_INNER))),
        compiler_params=_params(("arbitrary",)),
        name="conv_step",
    )(rest, state_conv.reshape(nb, (CONV_W - 1) * CONV_CH), dt, conv_w, conv_b, a_x, emat)

    to_cols = lambda a: a.reshape(nblk, bb, D_INNER).transpose(0, 2, 1)
    blk3 = lambda d1, d2: pl.BlockSpec((None, d1, d2), lambda i: (i, 0, 0))
    hspec = pl.BlockSpec((bb, D_INNER, D_STATE), lambda i: (i, 0, 0))
    gspec = pl.BlockSpec((bb, SSD_GROUPS, D_STATE), lambda i: (i, 0, 0))
    h_new, y = pl.pallas_call(
        functools.partial(_state_step_body, bb=bb),
        out_shape=(jax.ShapeDtypeStruct((nb, D_INNER, D_STATE), F32),
                   jax.ShapeDtypeStruct((nblk, bb, D_INNER), F32)),
        grid=(nblk,),
        in_specs=[hspec, blk3(D_INNER, bb), blk3(D_INNER, bb), gspec, gspec, blk3(bb, D_INNER),
                  pl.BlockSpec((1, D_INNER), lambda i: (0, 0))],
        out_specs=(hspec, blk3(bb, D_INNER)),
        compiler_params=_params(("parallel",)),
        name="state_step",
    )(state_ssm.reshape(nb, D_INNER, D_STATE), to_cols(xdt), to_cols(dec),
      bm.reshape(nb, SSD_GROUPS, D_STATE), cm.reshape(nb, SSD_GROUPS, D_STATE),
      xs.reshape(nblk, bb, D_INNER), d_x)
    return y.reshape(nb, D_INNER), h_new


def _merge_body(x_ref, att_ref, y_ref, z_ref, ga_ref, gb_ref, gn_ref, wa_ref, wb_ref, wo_ref, o_ref):
    att = jnp.concatenate([att_ref[c] for c in range(ATT_SLABS)], axis=1)
    out_a = jnp.dot(att.astype(BF16), wa_ref[...], preferred_element_type=F32)
    z = z_ref[...]
    y = y_ref[...] * (z * jax.nn.sigmoid(z))
    yn = _rmsnorm_rows(y, gn_ref[...]).astype(BF16)
    out_b = jnp.dot(yn, wb_ref[...], preferred_element_type=F32)
    merged = jax.nn.sigmoid(ga_ref[...]) * out_a + jax.nn.sigmoid(gb_ref[...]) * out_b
    o_ref[...] = x_ref[...] + jnp.dot(merged.astype(BF16), wo_ref[...], preferred_element_type=F32)


def _merge(x2d, att, y, rest, ssd_norm, w_att_out, w_ssd_out, w_out, tm=256):
    t = x2d.shape[0]
    tm = min(tm, t)
    row = lambda w, cb: pl.BlockSpec((tm, w), lambda i: (i, cb))
    full = lambda shape: pl.BlockSpec(shape, lambda i: (0, 0))
    return pl.pallas_call(
        _merge_body,
        out_shape=jax.ShapeDtypeStruct((t, D_MODEL), F32),
        grid=(t // tm,),
        in_specs=[row(D_MODEL, 0),
                  pl.BlockSpec((ATT_SLABS, tm, LANES), lambda i: (0, i, 0)),
                  row(D_INNER, 0),
                  row(D_INNER, REST_Z // D_INNER),
                  row(D_MODEL, REST_GATES // D_MODEL), row(D_MODEL, REST_GATES // D_MODEL + 1),
                  full((1, D_INNER)), full((ATT_W, D_MODEL)), full((D_INNER, D_MODEL)),
                  full((D_MODEL, D_MODEL))],
        out_specs=row(D_MODEL, 0),
        compiler_params=_params(("parallel",)),
        name="merge",
    )(x2d, att, y, rest, rest, rest, ssd_norm, w_att_out, w_ssd_out, w_out)


I1_BLOCK = SUBLANES


def _gelu_tanh(x):
    return 0.5 * x * (1.0 + jnp.tanh(math.sqrt(2.0 / math.pi) * (x + 0.044715 * (x * x * x))))


def _peer_body(x_ref, g_ref, wq_ref, k1_ref, k2_ref, u_ref, vt_ref, o_ref,
               xnt, acct, s_all, v_all, rk_all, e2, n1, w1, act, hbuf, *, tt):
    j = pl.program_id(1)
    nch = tt // LANES
    nkb = N_KEYS // I1_BLOCK

    @pl.when(j == 0)
    def _preamble():
        xn = _rmsnorm_rows(x_ref[...], g_ref[...])
        xnt[...] = xn.T.astype(BF16)
        acct[...] = jnp.zeros_like(acct)
        qt = jnp.dot(wq_ref[...], xnt[...], preferred_element_type=F32)
        for h in range(PEER_HEADS):
            for sd, kref in enumerate((k1_ref, k2_ref)):
                r0 = (2 * h + sd) * N_KEYS
                s_all[2 * h + sd] = jnp.dot(kref[h], qt[r0:r0 + N_KEYS, :].astype(BF16),
                                            preferred_element_type=F32)

        rowid = lax.broadcasted_iota(jnp.int32, (N_KEYS, LANES), 0).astype(F32)
        row16 = lax.broadcasted_iota(jnp.int32, (PEER_TOPK, LANES), 0)
        row16f = row16.astype(F32)

        def extract(idx, carry):
            hs = idx // nch
            off = pl.multiple_of((idx % nch) * LANES, LANES)
            s = s_all[hs, :, pl.ds(off, LANES)]
            rank = jnp.full((N_KEYS, LANES), float(PEER_TOPK), F32)
            vals = jnp.zeros((PEER_TOPK, LANES), F32)
            for k in range(PEER_TOPK):
                m = jnp.max(s, axis=0, keepdims=True)
                first = jnp.min(jnp.where(s == m, rowid, float(N_KEYS)), axis=0, keepdims=True)
                sel = rowid == first
                rank = jnp.where(sel, float(k), rank)
                s = jnp.where(sel, -jnp.inf, s)
                vals = jnp.where(row16 == k, m, vals)
            rk_all[hs, :, pl.ds(off, LANES)] = rank
            v_all[hs, :, pl.ds(off, LANES)] = vals
            return carry

        lax.fori_loop(0, 2 * PEER_HEADS * nch, extract, 0)

        def finish(idx, carry):
            h = idx // nch
            off = pl.multiple_of((idx % nch) * LANES, LANES)
            v1 = v_all[2 * h, :, pl.ds(off, LANES)]
            v2 = v_all[2 * h + 1, :, pl.ds(off, LANES)]
            shifted = [jnp.broadcast_to(v2[b:b + 1, :], (PEER_TOPK, LANES)) for b in range(PEER_TOPK)]
            cnt = jnp.zeros((PEER_TOPK, LANES), F32)
            zsum = jnp.zeros((1, LANES), F32)
            top = v1[0:1, :] + v2[0:1, :]
            for step in range(PEER_TOPK):
                front = v1 + shifted[0]
                m = jnp.max(front, axis=0, keepdims=True)
                first = jnp.min(jnp.where(front == m, row16f, float(PEER_TOPK)), axis=0, keepdims=True)
                sel = row16f == first
                cnt = cnt + jnp.where(sel, 1.0, 0.0)
                zsum = zsum + jnp.exp(m - top)
                live = PEER_TOPK - 1 - step
                for b in range(live):
                    shifted[b] = jnp.where(sel, shifted[b + 1], shifted[b])
            rz = 1.0 / zsum
            r1 = rk_all[2 * h, :, pl.ds(off, LANES)]
            s1 = s_all[2 * h, :, pl.ds(off, LANES)]
            s2 = s_all[2 * h + 1, :, pl.ds(off, LANES)]
            n_of = jnp.zeros((N_KEYS, LANES), F32)
            for a in range(PEER_TOPK):
                n_of = jnp.where(r1 == float(a), cnt[a:a + 1, :], n_of)
            wgt = jnp.exp(s1 - v1[0:1, :]) * rz
            e2[h, :, pl.ds(off, LANES)] = jnp.exp(s2 - v2[0:1, :])
            for kb in range(nkb):
                rs = slice(kb * I1_BLOCK, (kb + 1) * I1_BLOCK)
                n1[kb, h, :, pl.ds(off, LANES)] = n_of[rs, :]
                w1[kb, h, :, pl.ds(off, LANES)] = wgt[rs, :]
            return carry

        lax.fori_loop(0, PEER_HEADS * nch, finish, 0)

    act[...] = jnp.dot(u_ref[...], xnt[...], preferred_element_type=F32)

    def dense(c, carry):
        off = pl.multiple_of(c * LANES, LANES)
        r2s = [rk_all[2 * h + 1, :, pl.ds(off, LANES)] for h in range(PEER_HEADS)]
        for ii in range(I1_BLOCK):
            gate = jnp.zeros((N_KEYS, LANES), F32)
            for h in range(PEER_HEADS):
                nrow = n1[j, h, ii:ii + 1, pl.ds(off, LANES)]
                wrow = w1[j, h, ii:ii + 1, pl.ds(off, LANES)]
                gate = gate + jnp.where(r2s[h] < nrow, e2[h, :, pl.ds(off, LANES)] * wrow, 0.0)
            a = act[ii * N_KEYS:(ii + 1) * N_KEYS, pl.ds(off, LANES)]
            hbuf[ii * N_KEYS:(ii + 1) * N_KEYS, pl.ds(off, LANES)] = (gate * _gelu_tanh(a)).astype(BF16)
        return carry

    lax.fori_loop(0, nch, dense, 0)
    acct[...] += jnp.dot(vt_ref[...], hbuf[...], preferred_element_type=F32)

    @pl.when(j == pl.num_programs(1) - 1)
    def _():
        o_ref[...] = x_ref[...] + acct[...].T


def _peer(x2d, norm_ffn, wq_t, keys1, keys2, u_bf, vt_bf, tt=256):
    t = x2d.shape[0]
    tt = min(tt, t)
    nkb = N_KEYS // I1_BLOCK
    eb = I1_BLOCK * N_KEYS
    full = lambda shape: pl.BlockSpec(shape, lambda i, j: (0,) * len(shape))
    return pl.pallas_call(
        functools.partial(_peer_body, tt=tt),
        out_shape=jax.ShapeDtypeStruct((t, D_MODEL), F32),
        grid=(t // tt, nkb),
        in_specs=[pl.BlockSpec((tt, D_MODEL), lambda i, j: (i, 0)),
                  full((1, D_MODEL)), full((2 * PEER_HEADS * N_KEYS, D_MODEL)),
                  full((PEER_HEADS, N_KEYS, N_KEYS)), full((PEER_HEADS, N_KEYS, N_KEYS)),
                  pl.BlockSpec((eb, D_MODEL), lambda i, j: (j, 0)),
                  pl.BlockSpec((D_MODEL, eb), lambda i, j: (0, j))],
        out_specs=pl.BlockSpec((tt, D_MODEL), lambda i, j: (i, 0)),
        scratch_shapes=[pltpu.VMEM((D_MODEL, tt), BF16),
                        pltpu.VMEM((D_MODEL, tt), F32),
                        pltpu.VMEM((2 * PEER_HEADS, N_KEYS, tt), F32),
                        pltpu.VMEM((2 * PEER_HEADS, PEER_TOPK, tt), F32),
                        pltpu.VMEM((2 * PEER_HEADS, N_KEYS, tt), F32),
                        pltpu.VMEM((PEER_HEADS, N_KEYS, tt), F32),
                        pltpu.VMEM((nkb, PEER_HEADS, I1_BLOCK, tt), F32),
                        pltpu.VMEM((nkb, PEER_HEADS, I1_BLOCK, tt), F32),
                        pltpu.VMEM((eb, tt), F32),
                        pltpu.VMEM((eb, tt), BF16)],
        compiler_params=_params(("parallel", "arbitrary")),
        name="peer",
    )(x2d, norm_ffn, wq_t, keys1, keys2, u_bf, vt_bf)


def _ple_body(x_ref, p_ref, gp_ref, wg_ref, wp_ref, gf_ref, o_ref):
    x = x_ref[...]
    xn = _rmsnorm_rows(x, gp_ref[...]).astype(BF16)
    pg = jax.nn.sigmoid(jnp.dot(xn, wg_ref[...], preferred_element_type=F32))
    x3 = x + pg * jnp.dot(p_ref[...].astype(BF16), wp_ref[...], preferred_element_type=F32)
    o_ref[...] = _rmsnorm_rows(x3, gf_ref[...])


def _ple_final(x2d, p2d, norm_ple, w_gate, w_proj, norm_final, tm=512):
    t = x2d.shape[0]
    tm = min(tm, t)
    full = lambda shape: pl.BlockSpec(shape, lambda i: (0, 0))
    return pl.pallas_call(
        _ple_body,
        out_shape=jax.ShapeDtypeStruct((t, D_MODEL), F32),
        grid=(t // tm,),
        in_specs=[pl.BlockSpec((tm, D_MODEL), lambda i: (i, 0)),
                  pl.BlockSpec((tm, PLE_DIM), lambda i: (i, 0)),
                  full((1, D_MODEL)), full((D_MODEL, D_MODEL)), full((PLE_DIM, D_MODEL)),
                  full((1, D_MODEL))],
        out_specs=pl.BlockSpec((tm, D_MODEL), lambda i: (i, 0)),
        compiler_params=_params(("parallel",)),
        name="ple_final",
    )(x2d, p2d, norm_ple, w_gate, w_proj, norm_final)


def _prep_weights(norm_mix, w_in, conv_w, conv_b, dt_bias, a_log, d_skip, ssd_norm, w_att_out,
                  w_ssd_out, w_out, norm_ffn, w_peer_q, peer_keys1, peer_keys2, peer_u, peer_v,
                  norm_ple, w_ple_gate, w_ple_proj, norm_final):
    w = w_in[0]
    c = [0, Q_COLS, 2 * Q_COLS, 3 * Q_COLS, 3 * Q_COLS + D_INNER, 3 * Q_COLS + D_INNER + CONV_CH,
         3 * Q_COLS + D_INNER + CONV_CH + SSD_HEADS]
    wq, wk, wv, wz, wxbc, wdt, wgt = (w[:, c[0]:c[1]], w[:, c[1]:c[2]], w[:, c[2]:c[3]], w[:, c[3]:c[4]],
                                      w[:, c[4]:c[5]], w[:, c[5]:c[6]], w[:, c[6]:])
    row = lambda v: v.reshape(1, -1).astype(F32)
    p = {}
    p["norm_mix"] = row(norm_mix[0])
    p["w_qkv"] = jnp.concatenate([wq * (HEAD_DIM ** -0.5), wk, wv], axis=1).astype(BF16)
    p["w_rest"] = jnp.concatenate([wxbc, wz, wgt], axis=1).astype(BF16)
    p["w_dt"] = jnp.pad(wdt, ((0, 0), (0, LANES - SSD_HEADS))).astype(BF16)
    p["dt_bias"] = jnp.pad(row(dt_bias[0]), ((0, 0), (0, LANES - SSD_HEADS)))
    p["conv_w"] = conv_w[0].astype(F32)
    p["conv_b"] = row(conv_b[0])
    p["a_x"], p["d_x"], p["emat"], p["ltri"] = _ssd_consts(a_log[0], d_skip[0])
    p["ssd_norm"] = row(ssd_norm[0])
    p["w_att_out"] = w_att_out[0].astype(BF16)
    p["w_ssd_out"] = w_ssd_out[0].astype(BF16)
    p["w_out"] = w_out[0].astype(BF16)
    p["norm_ffn"] = row(norm_ffn[0])
    p["wq_t"] = w_peer_q[0].T.astype(BF16)
    p["keys1"] = peer_keys1[0].astype(BF16)
    p["keys2"] = peer_keys2[0].astype(BF16)
    p["u"] = peer_u[0].astype(BF16)
    p["vt"] = peer_v[0].T.astype(BF16)
    p["norm_ple"] = row(norm_ple[0])
    p["w_ple_gate"] = w_ple_gate[0].astype(BF16)
    p["w_ple_proj"] = w_ple_proj[0].astype(BF16)
    p["norm_final"] = row(norm_final)
    return p


def _tail(x2d, att, y, rest, p2d, p, peer_tt):
    x1 = _merge(x2d, att, y, rest, p["ssd_norm"], p["w_att_out"], p["w_ssd_out"], p["w_out"])
    x2 = _peer(x1, p["norm_ffn"], p["wq_t"], p["keys1"], p["keys2"], p["u"], p["vt"], tt=peer_tt)
    return _ple_final(x2, p2d, p["norm_ple"], p["w_ple_gate"], p["w_ple_proj"], p["norm_final"])


def _heads_from_slabs(qkv, which, g, b, s, rows):
    s0 = which * (Q_COLS // LANES) + g * ATT_SLABS
    x = qkv[s0:s0 + ATT_SLABS].reshape(ATT_SLABS, b, s, LANES)[:, :, s - rows:]
    return x.transpose(1, 2, 0, 3).reshape(b, rows, N_HEADS, HEAD_DIM)


def _kv_rows(qkv, g, b, s, rows):
    return jnp.stack([_heads_from_slabs(qkv, 1, g, b, s, rows),
                      _heads_from_slabs(qkv, 2, g, b, s, rows)], axis=2)[None]


def _prompt(x, p_in, p):
    b, s, _ = x.shape
    x2d = x.reshape(b * s, D_MODEL)
    qkv, rest, dt = _projections(x2d, jnp.arange(s, dtype=jnp.int32), p["norm_mix"], p["w_qkv"],
                                 p["w_rest"], p["w_dt"], p["dt_bias"], tm=1024)
    att = _attn_prompt(qkv, b, s)
    y, ssm = _ssd_prompt(rest, dt, p["conv_w"], p["conv_b"], p["a_x"], p["d_x"], p["emat"], p["ltri"], b, s)
    out = _tail(x2d, att, y, rest, p_in[0].reshape(b * s, PLE_DIM), p, peer_tt=256)
    kvs = [_kv_rows(qkv, g, b, s, min(win, s)) for g, (win, _) in enumerate(ATT_GROUPS)]
    conv = rest.reshape(b, s, REST_COLS)[:, s - (CONV_W - 1):, :CONV_CH][None]
    ssm = ssm.reshape(1, b, SSD_HEADS, CHUNK, D_STATE)
    return out.reshape(b, s, D_MODEL), kvs, conv, ssm


def _sample(x, p_in, caches, state_conv, state_ssm, p):
    b, s, _ = x.shape
    assert s == 1
    x2d = x.reshape(b, D_MODEL)
    pos = jnp.full((b,), PAST_LEN, dtype=jnp.int32)
    qkv, rest, dt = _projections(x2d, pos, p["norm_mix"], p["w_qkv"], p["w_rest"], p["w_dt"],
                                 p["dt_bias"], tm=b)
    heads = lambda which: jnp.stack([_heads_from_slabs(qkv, which, g, b, 1, 1)[:, 0]
                                     for g in range(N_GROUPS)], axis=1)
    q4, k4, v4 = heads(0), heads(1), heads(2)
    att = _attn_step(q4, k4, v4, [c[0] for c in caches])
    att = att.reshape(b, ATT_SLABS, LANES).transpose(1, 0, 2)
    xbc = rest[:, :CONV_CH]
    y, ssm = _ssd_step(xbc, dt, state_conv[0], state_ssm[0],
                       p["conv_w"], p["conv_b"], p["a_x"], p["d_x"], p["emat"])
    out = _tail(x2d, att, y, rest, p_in[0].reshape(b, PLE_DIM), p, peer_tt=b)
    kvs = [jnp.stack([k4[:, g], v4[:, g]], axis=1)[None, :, None] for g in range(N_GROUPS)]
    conv = jnp.concatenate([state_conv[0][:, 1:], xbc[:, None, :]], axis=1)[None]
    ssm = ssm.reshape(1, b, SSD_HEADS, CHUNK, D_STATE)
    return out.reshape(b, 1, D_MODEL), kvs, conv, ssm


def kernel(x_prompt, x_sample, cache_kv_w128, cache_kv_w512, cache_kv_w2048, state_conv, state_ssm, p_prompt, p_sample, norm_mix, w_in, conv_w, conv_b, dt_bias, a_log, d_skip, ssd_norm, w_att_out, w_ssd_out, w_out, norm_ffn, w_peer_q, peer_keys1, peer_keys2, peer_u, peer_v, norm_ple, w_ple_gate, w_ple_proj, norm_final):
    p = _prep_weights(norm_mix, w_in, conv_w, conv_b, dt_bias, a_log, d_skip, ssd_norm, w_att_out,
                      w_ssd_out, w_out, norm_ffn, w_peer_q, peer_keys1, peer_keys2, peer_u, peer_v,
                      norm_ple, w_ple_gate, w_ple_proj, norm_final)
    y_p, kv_p, conv_p, ssm_p = _prompt(x_prompt, p_prompt, p)
    y_s, kv_s, conv_s, ssm_s = _sample(x_sample, p_sample, (cache_kv_w128, cache_kv_w512, cache_kv_w2048),
                                       state_conv, state_ssm, p)
    return (y_p, y_s, kv_p[0], kv_p[1], kv_p[2], conv_p, ssm_p, kv_s[0], kv_s[1], kv_s[2], conv_s, ssm_s)
```

```python
import functools
import math

import jax
import jax.numpy as jnp
from jax import lax
from jax.experimental import pallas as pl
from jax.experimental.pallas import tpu as pltpu

F32 = jnp.float32
BF16 = jnp.bfloat16

LANES = 128
SUBLANES = 8
VMEM_BYTES_V7X = 64 * 1024 * 1024
VMEM_LIMIT = VMEM_BYTES_V7X * 3 // 4

D_MODEL = 1024
ATT_GROUPS = ((128, 1), (512, 4), (2048, 16))
N_GROUPS = len(ATT_GROUPS)
N_HEADS = 8
HEAD_DIM = 64
ATT_W = N_HEADS * HEAD_DIM
ATT_SLABS = ATT_W // LANES
Q_COLS = N_GROUPS * ATT_W
QKV_COLS = 3 * Q_COLS
ROPE_THETA = 10000.0
D_INNER = 2048
SSD_HEADS = 32
SSD_GROUPS = 8
SSD_GW = D_INNER // SSD_GROUPS
D_STATE = 128
CONV_W = 4
CONV_CH = D_INNER + 2 * SSD_GROUPS * D_STATE
CHUNK = 64
N_KEYS = 128
N_EXPERTS = N_KEYS * N_KEYS
PEER_HEADS = 8
PEER_TOPK = 16
PLE_DIM = 256
EPS = 1e-6
PAST_LEN = 8192
NEG = -1e30

REST_Z = CONV_CH
REST_GATES = REST_Z + D_INNER
REST_COLS = REST_GATES + 2 * D_MODEL
QKV_TN = 768
REST_TN = 1024
PROJ_TM = 1024
ATT_ROWS = 2048
SSD_STEP_CHUNKS = 2


def _params(sem):
    return pltpu.CompilerParams(dimension_semantics=sem, vmem_limit_bytes=VMEM_LIMIT)


def _rmsnorm_rows(x, g):
    return x * lax.rsqrt(jnp.mean(x * x, axis=-1, keepdims=True) + EPS) * g


def _nt(a, b):
    return lax.dot_general(a, b, (((1,), (1,)), ((), ())), preferred_element_type=F32)


def _proj_qkv_body(x_ref, g_ref, w_ref, cos_ref, sin_ref, o_ref, xn_ref, *, rope_tiles):
    j = pl.program_id(1)

    @pl.when(j == 0)
    def _():
        xn_ref[...] = _rmsnorm_rows(x_ref[...], g_ref[...]).astype(BF16)

    acc = jnp.dot(xn_ref[...], w_ref[...], preferred_element_type=F32)
    n_slab = acc.shape[1] // LANES

    @pl.when(j < rope_tiles)
    def _():
        cos = cos_ref[...]
        sin = sin_ref[...]
        lane = lax.broadcasted_iota(jnp.int32, cos.shape, 1)
        first_half = (lane & (HEAD_DIM - 1)) < HEAD_DIM // 2
        for c in range(n_slab):
            a = acc[:, c * LANES:(c + 1) * LANES]
            partner = jnp.where(first_half, pltpu.roll(a, LANES - HEAD_DIM // 2, 1),
                                pltpu.roll(a, HEAD_DIM // 2, 1))
            o_ref[c] = a * cos + partner * sin

    @pl.when(j >= rope_tiles)
    def _():
        for c in range(n_slab):
            o_ref[c] = acc[:, c * LANES:(c + 1) * LANES]


def _proj_plain_body(x_ref, g_ref, w_ref, o_ref, xn_ref):
    @pl.when(pl.program_id(1) == 0)
    def _():
        xn_ref[...] = _rmsnorm_rows(x_ref[...], g_ref[...]).astype(BF16)

    o_ref[...] = jnp.dot(xn_ref[...], w_ref[...], preferred_element_type=F32)


def _proj_dt_body(x_ref, g_ref, w_ref, bias_ref, o_ref):
    xn = _rmsnorm_rows(x_ref[...], g_ref[...]).astype(BF16)
    v = jnp.dot(xn, w_ref[...], preferred_element_type=F32) + bias_ref[...]
    o_ref[...] = jnp.maximum(v, 0.0) + jnp.log1p(jnp.exp(-jnp.abs(v)))


def _rope_tables(pos):
    half = HEAD_DIM // 2
    inv = jnp.exp(jnp.arange(half, dtype=F32) * (-2.0 * math.log(ROPE_THETA) / HEAD_DIM))
    ang = pos.astype(F32)[:, None] * inv[None, :]
    cos, sin = jnp.cos(ang), jnp.sin(ang)
    cos_t = jnp.concatenate([cos, cos, cos, cos], axis=1)
    sin_t = jnp.concatenate([-sin, sin, -sin, sin], axis=1)
    return cos_t, sin_t


def _projections(x2d, pos, norm_mix, w_qkv, w_rest, w_dt, dt_bias_row, tm):
    t, k = x2d.shape
    tm = min(tm, t)
    cos_t, sin_t = _rope_tables(pos)
    n_pos_blocks = pos.shape[0] // tm
    x_spec = pl.BlockSpec((tm, k), lambda i, j: (i, 0))
    g_spec = pl.BlockSpec((1, k), lambda i, j: (0, 0))
    tab_spec = pl.BlockSpec((tm, LANES), lambda i, j: (i % n_pos_blocks, 0))
    slabs = QKV_TN // LANES
    qkv = pl.pallas_call(
        functools.partial(_proj_qkv_body, rope_tiles=2 * Q_COLS // QKV_TN),
        out_shape=jax.ShapeDtypeStruct((QKV_COLS // LANES, t, LANES), F32),
        grid=(t // tm, QKV_COLS // QKV_TN),
        in_specs=[x_spec, g_spec, pl.BlockSpec((k, QKV_TN), lambda i, j: (0, j)), tab_spec, tab_spec],
        out_specs=pl.BlockSpec((slabs, tm, LANES), lambda i, j: (j, i, 0)),
        scratch_shapes=[pltpu.VMEM((tm, k), BF16)],
        compiler_params=_params(("parallel", "arbitrary")),
        name="proj_qkv",
    )(x2d, norm_mix, w_qkv, cos_t, sin_t)
    rest = pl.pallas_call(
        _proj_plain_body,
        out_shape=jax.ShapeDtypeStruct((t, REST_COLS), F32),
        grid=(t // tm, REST_COLS // REST_TN),
        in_specs=[x_spec, g_spec, pl.BlockSpec((k, REST_TN), lambda i, j: (0, j))],
        out_specs=pl.BlockSpec((tm, REST_TN), lambda i, j: (i, j)),
        scratch_shapes=[pltpu.VMEM((tm, k), BF16)],
        compiler_params=_params(("parallel", "arbitrary")),
        name="proj_rest",
    )(x2d, norm_mix, w_rest)
    dt = pl.pallas_call(
        _proj_dt_body,
        out_shape=jax.ShapeDtypeStruct((t, LANES), F32),
        grid=(t // tm,),
        in_specs=[pl.BlockSpec((tm, k), lambda i: (i, 0)), pl.BlockSpec((1, k), lambda i: (0, 0)),
                  pl.BlockSpec((k, LANES), lambda i: (0, 0)), pl.BlockSpec((1, LANES), lambda i: (0, 0))],
        out_specs=pl.BlockSpec((tm, LANES), lambda i: (i, 0)),
        compiler_params=_params(("parallel",)),
        name="proj_dt",
    )(x2d, norm_mix, w_dt, dt_bias_row)
    return qkv, rest, dt


def _attn_prompt_body(*refs, dil, has_prev_group, emit_lse):
    q_ref, kc_ref, kp_ref, vc_ref, vp_ref = refs[:5]
    pos = 5
    if has_prev_group:
        op_ref, lp_ref = refs[pos:pos + 2]
        pos += 2
    o_ref = refs[pos]
    pos += 1
    if emit_lse:
        l_ref = refs[pos]
        pos += 1
    kbuf, vbuf = refs[pos:pos + 2]

    i = pl.program_id(1)
    rows = q_ref.shape[0]
    span = LANES * dil
    kbuf[0:rows, :] = kp_ref[...]
    kbuf[rows:2 * rows, :] = kc_ref[...]
    vbuf[0:rows, :] = vp_ref[...]
    vbuf[rows:2 * rows, :] = vc_ref[...]

    row = lax.broadcasted_iota(jnp.int32, (LANES, LANES), 0)
    col = lax.broadcasted_iota(jnp.int32, (LANES, LANES), 1)
    causal = col <= row
    band_prev = col >= row
    low_half = col < HEAD_DIM

    def rd(ref, start):
        if dil == 1:
            return ref[pl.ds(start, LANES), :]
        return ref[pl.ds(start, LANES, stride=dil), :]

    def sub_block(u, carry):
        sp = u // dil
        start = sp * span + (u % dil)
        q2 = rd(q_ref, start)
        kc2 = rd(kbuf, rows + start).astype(BF16)
        kp2 = rd(kbuf, rows - span + start).astype(BF16)
        vc2 = rd(vbuf, rows + start).astype(BF16)
        vp2 = rd(vbuf, rows - span + start).astype(BF16)
        mask_p = jnp.logical_and(band_prev, jnp.logical_or(i > 0, sp > 0))
        outs, lses = [], []
        for hh in range(2):
            head_lanes = low_half if hh == 0 else jnp.logical_not(low_half)
            qh = jnp.where(head_lanes, q2, 0.0).astype(BF16)
            sc = jnp.where(causal, _nt(qh, kc2), NEG)
            spv = jnp.where(mask_p, _nt(qh, kp2), NEG)
            m = jnp.maximum(jnp.max(sc, axis=1, keepdims=True), jnp.max(spv, axis=1, keepdims=True))
            ec = jnp.exp(sc - m)
            ep = jnp.exp(spv - m)
            den = jnp.sum(ec, axis=1, keepdims=True) + jnp.sum(ep, axis=1, keepdims=True)
            outs.append((jnp.dot(ec.astype(BF16), vc2, preferred_element_type=F32)
                         + jnp.dot(ep.astype(BF16), vp2, preferred_element_type=F32)) / den)
            lses.append(m + jnp.log(den))
        o2 = jnp.where(low_half, outs[0], outs[1])
        l2 = jnp.where(low_half, lses[0], lses[1])
        if has_prev_group:
            o_prev = rd(op_ref, start)
            l_prev = rd(lp_ref, start)
            mm = jnp.maximum(l_prev, l2)
            wp = jnp.exp(l_prev - mm)
            wc = jnp.exp(l2 - mm)
            tot = wp + wc
            o2 = (wp * o_prev + wc * o2) / tot
            l2 = mm + jnp.log(tot)
        if dil == 1:
            o_ref[pl.ds(start, LANES), :] = o2
            if emit_lse:
                l_ref[pl.ds(start, LANES), :] = l2
        else:
            o_ref[pl.ds(start, LANES, stride=dil), :] = o2
            if emit_lse:
                l_ref[pl.ds(start, LANES, stride=dil), :] = l2
        return carry

    lax.fori_loop(0, rows // LANES, sub_block, 0, unroll=2)


def _attn_prompt_group(qkv, g, dil, b, s, prev):
    rows = min(ATT_ROWS, s)
    assert rows % (LANES * dil) == 0 and s % rows == 0
    qkv4 = qkv.reshape(QKV_COLS // LANES, b, s, LANES)
    kslab = Q_COLS // LANES
    vslab = 2 * Q_COLS // LANES
    blk = (None, None, rows, LANES)
    in_specs = [
        pl.BlockSpec(blk, lambda bb, i, c: (g * ATT_SLABS + c, bb, i, 0)),
        pl.BlockSpec(blk, lambda bb, i, c: (kslab + g * ATT_SLABS + c, bb, i, 0)),
        pl.BlockSpec(blk, lambda bb, i, c: (kslab + g * ATT_SLABS + c, bb, jnp.maximum(i - 1, 0), 0)),
        pl.BlockSpec(blk, lambda bb, i, c: (vslab + g * ATT_SLABS + c, bb, i, 0)),
        pl.BlockSpec(blk, lambda bb, i, c: (vslab + g * ATT_SLABS + c, bb, jnp.maximum(i - 1, 0), 0)),
    ]
    args = [qkv4] * 5
    o_spec = pl.BlockSpec(blk, lambda bb, i, c: (c, bb, i, 0))
    if prev is not None:
        in_specs += [o_spec, o_spec]
        args += [prev[0].reshape(ATT_SLABS, b, s, LANES), prev[1].reshape(ATT_SLABS, b, s, LANES)]
    emit_lse = g < N_GROUPS - 1
    shape = jax.ShapeDtypeStruct((ATT_SLABS, b, s, LANES), F32)
    body = functools.partial(_attn_prompt_body, dil=dil, has_prev_group=prev is not None, emit_lse=emit_lse)
    out = pl.pallas_call(
        body,
        out_shape=(shape, shape) if emit_lse else shape,
        grid=(b, s // rows, ATT_SLABS),
        in_specs=in_specs,
        out_specs=(o_spec, o_spec) if emit_lse else o_spec,
        scratch_shapes=[pltpu.VMEM((2 * rows, LANES), F32), pltpu.VMEM((2 * rows, LANES), F32)],
        compiler_params=_params(("parallel", "parallel", "parallel")),
        name="attn_prompt_g%d" % g,
    )(*args)
    if emit_lse:
        return out[0].reshape(ATT_SLABS, b * s, LANES), out[1].reshape(ATT_SLABS, b * s, LANES)
    return out.reshape(ATT_SLABS, b * s, LANES)


def _attn_prompt(qkv, b, s):
    prev = None
    for g, (win, dil) in enumerate(ATT_GROUPS):
        assert win // dil == LANES
        prev = _attn_prompt_group(qkv, g, dil, b, s, prev)
    return prev


def _split3(a):
    hi = a.astype(BF16)
    r1 = a - hi.astype(F32)
    mid = r1.astype(BF16)
    lo = (r1 - mid.astype(F32)).astype(BF16)
    return hi, mid, lo


def _expand_heads(a, emat):
    return sum(jnp.dot(part, emat, preferred_element_type=F32) for part in _split3(a))


def _ssd_prompt_body(xbc_ref, dt_ref, cw_ref, cb_ref, a_ref, dx_ref, e_ref, ltri_ref,
                     y_ref, st_ref, xext, h_ref, *, n_chunks):
    i = pl.program_id(1)
    lc = n_chunks * CHUNK
    halo = SUBLANES

    @pl.when(i == 0)
    def _():
        xext[0:halo, :] = jnp.zeros((halo, CONV_CH), F32)
        h_ref[...] = jnp.zeros_like(h_ref)

    xext[halo:halo + lc, :] = xbc_ref[...]

    lane_x = lax.broadcasted_iota(jnp.int32, (CHUNK, D_INNER), 1) & (CHUNK - 1)
    row_x = lax.broadcasted_iota(jnp.int32, (CHUNK, D_INNER), 0)
    diag = lane_x == row_x
    tril = row_x >= lane_x
    lane_p = lax.broadcasted_iota(jnp.int32, (CHUNK, LANES), 1)
    low_half = lane_p < CHUNK
    a_row = a_ref[...]
    dx = dx_ref[...]
    emat = e_ref[...]
    ltri = ltri_ref[...]

    for c in range(n_chunks):
        o = c * CHUNK
        conv = cb_ref[...] + xext[halo - 3 + o:halo - 3 + o + CHUNK, :] * cw_ref[0:1, :]
        conv = conv + xext[halo - 2 + o:halo - 2 + o + CHUNK, :] * cw_ref[1:2, :]
        conv = conv + xext[halo - 1 + o:halo - 1 + o + CHUNK, :] * cw_ref[2:3, :]
        conv = conv + xext[halo + o:halo + o + CHUNK, :] * cw_ref[3:4, :]
        xc = conv * jax.nn.sigmoid(conv)
        xs = xc[:, :D_INNER]
        bm = xc[:, D_INNER:D_INNER + SSD_GROUPS * D_STATE]
        cm = xc[:, D_INNER + SSD_GROUPS * D_STATE:]

        dt = dt_ref[o:o + CHUNK, :]
        dtx = _expand_heads(dt, emat)
        acs = sum(jnp.dot(ltri, part, preferred_element_type=F32) for part in _split3(dt * a_row))
        acsx = _expand_heads(acs, emat)
        last = acsx[CHUNK - 1:CHUNK, :]
        xdt = xs * dtx
        xw = (xdt * jnp.exp(last - acsx)).astype(BF16)
        eacs = jnp.exp(acsx)
        cdec = jnp.exp(last)
        rrow = jnp.sum(jnp.where(diag, acsx, 0.0), axis=0, keepdims=True)
        lm = jnp.exp(jnp.where(tril, acsx - rrow, NEG))
        xdtb = xdt.astype(BF16)
        cmb = cm.astype(BF16)

        for g in range(SSD_GROUPS):
            gs = slice(g * SSD_GW, (g + 1) * SSD_GW)
            bg = bm[:, g * D_STATE:(g + 1) * D_STATE]
            bgb = bg.astype(BF16)
            cg = cmb[:, g * D_STATE:(g + 1) * D_STATE]
            cbx = _nt(cg, jnp.concatenate([bgb] * (SSD_GW // CHUNK), axis=0))
            wg = (cbx * lm[:, gs]).astype(BF16)
            ydiag = []
            for pr in range(SSD_GW // LANES):
                xp = xdtb[:, g * SSD_GW + pr * LANES:g * SSD_GW + (pr + 1) * LANES]
                zero = jnp.zeros_like(xp)
                bd = jnp.concatenate([jnp.where(low_half, xp, zero), jnp.where(low_half, zero, xp)], axis=0)
                ydiag.append(jnp.dot(wg[:, pr * LANES:(pr + 1) * LANES], bd, preferred_element_type=F32))
            hp = h_ref[g]
            yoff = jnp.dot(cg, hp.astype(BF16), preferred_element_type=F32) * eacs[:, gs]
            st = jnp.dot(bg.T.astype(BF16), xw[:, gs], preferred_element_type=F32)
            h_ref[g] = hp * cdec[:, gs] + st
            y_ref[o:o + CHUNK, gs] = jnp.concatenate(ydiag, axis=1) + yoff + dx[:, gs] * xs[:, gs]

    xext[0:halo, :] = xext[lc:lc + halo, :]

    @pl.when(i == pl.num_programs(1) - 1)
    def _():
        for g in range(SSD_GROUPS):
            st_ref[g * SSD_GW:(g + 1) * SSD_GW, :] = h_ref[g].T


def _ssd_consts(a_log, d_skip):
    a = -jnp.exp(a_log.astype(F32))
    a_row = jnp.pad(a, (0, LANES - SSD_HEADS))[None, :]
    a_x = jnp.repeat(a, CHUNK)[None, :]
    d_x = jnp.repeat(d_skip.astype(F32), CHUNK)[None, :]
    emat = (jnp.arange(D_INNER)[None, :] // CHUNK == jnp.arange(LANES)[:, None]).astype(BF16)
    ltri = (jnp.arange(CHUNK)[:, None] >= jnp.arange(CHUNK)[None, :]).astype(BF16)
    return a_row, a_x, d_x, emat, ltri


def _ssd_prompt(rest, dt, conv_w, conv_b, a_row, d_x, emat, ltri, b, s, n_chunks=SSD_STEP_CHUNKS):
    lc = n_chunks * CHUNK
    rest_v = rest.reshape(b, s, REST_COLS)
    dt_v = dt.reshape(b, s, LANES)
    const = lambda shape: pl.BlockSpec(shape, lambda bb, i: (0,) * len(shape))
    body = functools.partial(_ssd_prompt_body, n_chunks=n_chunks)
    y, st = pl.pallas_call(
        body,
        out_shape=(jax.ShapeDtypeStruct((b, s, D_INNER), F32),
                   jax.ShapeDtypeStruct((b, D_INNER, D_STATE), F32)),
        grid=(b, s // lc),
        in_specs=[pl.BlockSpec((None, lc, CONV_CH), lambda bb, i: (bb, i, 0)),
                  pl.BlockSpec((None, lc, LANES), lambda bb, i: (bb, i, 0)),
                  const((CONV_W, CONV_CH)), const((1, CONV_CH)), const((1, LANES)),
                  const((1, D_INNER)), const((LANES, D_INNER)), const((CHUNK, CHUNK))],
        out_specs=(pl.BlockSpec((None, lc, D_INNER), lambda bb, i: (bb, i, 0)),
                   pl.BlockSpec((None, D_INNER, D_STATE), lambda bb, i: (bb, 0, 0))),
        scratch_shapes=[pltpu.VMEM((lc + SUBLANES, CONV_CH), F32),
                        pltpu.VMEM((SSD_GROUPS, D_STATE, SSD_GW), F32)],
        compiler_params=_params(("parallel", "arbitrary")),
        name="ssd_prompt",
    )(rest_v, dt_v, conv_w, conv_b, a_row, d_x, emat, ltri)
    return y.reshape(b * s, D_INNER), st


def _attn_step_body(q_ref, kn_ref, vn_ref, c0_ref, c1_ref, c2_ref, o_ref):
    caches = (c0_ref, c1_ref, c2_ref)
    hrow = lax.broadcasted_iota(jnp.int32, (N_HEADS, ATT_W), 0)
    hlane = lax.broadcasted_iota(jnp.int32, (N_HEADS, ATT_W), 1) // HEAD_DIM
    hmask = hrow == hlane
    s_list, sn_list = [], []
    for g, (win, dil) in enumerate(ATT_GROUPS):
        qbd = jnp.where(hmask, jnp.broadcast_to(q_ref[g:g + 1, :], (N_HEADS, ATT_W)), 0.0).astype(BF16)
        s = jnp.dot(qbd, caches[g][0].astype(BF16), preferred_element_type=F32)
        if dil > 1:
            wpos = lax.broadcasted_iota(jnp.int32, s.shape, 1)
            s = jnp.where((wpos & (dil - 1)) == 0, s, NEG)
        s_list.append(s)
        sn_list.append(jnp.sum(qbd.astype(F32) * kn_ref[g:g + 1, :].astype(BF16).astype(F32),
                               axis=1, keepdims=True))
    m = sn_list[0]
    for g in range(N_GROUPS):
        m = jnp.maximum(m, jnp.maximum(jnp.max(s_list[g], axis=1, keepdims=True), sn_list[g]))
    den = jnp.zeros((N_HEADS, 1), F32)
    acc = jnp.zeros((N_HEADS, ATT_W), F32)
    for g in range(N_GROUPS):
        e = jnp.exp(s_list[g] - m)
        en = jnp.exp(sn_list[g] - m)
        den = den + jnp.sum(e, axis=1, keepdims=True) + en
        acc = acc + _nt(e.astype(BF16), caches[g][1].astype(BF16))
        acc = acc + en.astype(BF16).astype(F32) * vn_ref[g:g + 1, :].astype(BF16).astype(F32)
    o_ref[...] = jnp.sum(jnp.where(hmask, acc / den, 0.0), axis=0, keepdims=True)


def _attn_step(q, kn, vn, caches):
    nb = q.shape[0]
    views, specs = [], []
    for (win, dil), c in zip(ATT_GROUPS, caches):
        w = c.shape[1]
        assert w == win, "cache must hold exactly one window of past rows"
        views.append(c.transpose(0, 2, 3, 4, 1).reshape(nb, 2, ATT_W, w))
        specs.append(pl.BlockSpec((None, 2, ATT_W, w), lambda i: (i, 0, 0, 0)))
    qspec = pl.BlockSpec((None, N_GROUPS, ATT_W), lambda i: (i, 0, 0))
    out = pl.pallas_call(
        _attn_step_body,
        out_shape=jax.ShapeDtypeStruct((nb, 1, ATT_W), F32),
        grid=(nb,),
        in_specs=[qspec, qspec, qspec, *specs],
        out_specs=pl.BlockSpec((None, 1, ATT_W), lambda i: (i, 0, 0)),
        compiler_params=_params(("parallel",)),
        name="attn_step",
    )(q, kn, vn, *views)
    return out.reshape(nb, ATT_W)


def _conv_step_body(rest_ref, sc_ref, dt_ref, cw_ref, cb_ref, ax_ref, e_ref,
                    xs_ref, bm_ref, cm_ref, xdt_ref, dec_ref):
    conv = cb_ref[...] + sc_ref[:, 0:CONV_CH] * cw_ref[0:1, :]
    conv = conv + sc_ref[:, CONV_CH:2 * CONV_CH] * cw_ref[1:2, :]
    conv = conv + sc_ref[:, 2 * CONV_CH:3 * CONV_CH] * cw_ref[2:3, :]
    conv = conv + rest_ref[...] * cw_ref[3:4, :]
    xc = conv * jax.nn.sigmoid(conv)
    xs = xc[:, :D_INNER]
    dtx = _expand_heads(dt_ref[...], e_ref[...])
    xs_ref[...] = xs
    bm_ref[...] = xc[:, D_INNER:D_INNER + SSD_GROUPS * D_STATE]
    cm_ref[...] = xc[:, D_INNER + SSD_GROUPS * D_STATE:]
    xdt_ref[...] = xs * dtx
    dec_ref[...] = jnp.exp(dtx * ax_ref[...])


def _state_step_body(h_ref, xdt_ref, dec_ref, b_ref, c_ref, xs_ref, dx_ref, ho_ref, y_ref, *, bb):
    grow = lax.broadcasted_iota(jnp.int32, (SSD_GROUPS, D_INNER), 0)
    glane = lax.broadcasted_iota(jnp.int32, (SSD_GROUPS, D_INNER), 1) // SSD_GW
    gmask = grow == glane
    for bi in range(bb):
        h = h_ref[bi]
        bmat = b_ref[bi]
        bx = jnp.concatenate([jnp.broadcast_to(bmat[g:g + 1, :], (SSD_GW, D_STATE))
                              for g in range(SSD_GROUPS)], axis=0)
        hn = h * dec_ref[:, bi:bi + 1] + xdt_ref[:, bi:bi + 1] * bx
        ho_ref[bi] = hn
        y8 = _nt(c_ref[bi].astype(BF16), hn.astype(BF16))
        y = jnp.sum(jnp.where(gmask, y8, 0.0), axis=0, keepdims=True)
        y_ref[bi:bi + 1, :] = y + dx_ref[...] * xs_ref[bi:bi + 1, :]


def _ssd_step(rest, dt, state_conv, state_ssm, conv_w, conv_b, a_x, d_x, emat, bb=4):
    nb = rest.shape[0]
    nblk = nb // bb
    full = lambda shape: pl.BlockSpec(shape, lambda i: (0,) * len(shape))
    xs, bm, cm, xdt, dec = pl.pallas_call(
        _conv_step_body,
        out_shape=(jax.ShapeDtypeStruct((nb, D_INNER), F32),
                   jax.ShapeDtypeStruct((nb, SSD_GROUPS * D_STATE), F32),
                   jax.ShapeDtypeStruct((nb, SSD_GROUPS * D_STATE), F32),
                   jax.ShapeDtypeStruct((nb, D_INNER), F32),
                   jax.ShapeDtypeStruct((nb, D_INNER), F32)),
        grid=(1,),
        in_specs=[pl.BlockSpec((nb, CONV_CH), lambda i: (0, 0)),
                  full((nb, (CONV_W - 1) * CONV_CH)), full((nb, LANES)),
                  full((CONV_W, CONV_CH)), full((1, CONV_CH)), full((1, D_INNER)),
                  full((LANES, D_INNER))],
        out_specs=(full((nb, D_INNER)), full((nb, SSD_GROUPS * D_STATE)),
                   full((nb, SSD_GROUPS * D_STATE)), full((nb, D_INNER)), full((nb, D_INNER))),
        compiler_params=_params(("arbitrary",)),
        name="conv_step",
    )(rest, state_conv.reshape(nb, (CONV_W - 1) * CONV_CH), dt, conv_w, conv_b, a_x, emat)

    to_cols = lambda a: a.reshape(nblk, bb, D_INNER).transpose(0, 2, 1)
    blk3 = lambda d1, d2: pl.BlockSpec((None, d1, d2), lambda i: (i, 0, 0))
    hspec = pl.BlockSpec((bb, D_INNER, D_STATE), lambda i: (i, 0, 0))
    gspec = pl.BlockSpec((bb, SSD_GROUPS, D_STATE), lambda i: (i, 0, 0))
    h_new, y = pl.pallas_call(
        functools.partial(_state_step_body, bb=bb),
        out_shape=(jax.ShapeDtypeStruct((nb, D_INNER, D_STATE), F32),
                   jax.ShapeDtypeStruct((nblk, bb, D_INNER), F32)),
        grid=(nblk,),
        in_specs=[hspec, blk3(D_INNER, bb), blk3(D_INNER, bb), gspec, gspec, blk3(bb, D_INNER),
                  pl.BlockSpec((1, D_INNER), lambda i: (0, 0))],
        out_specs=(hspec, blk3(bb, D_INNER)),
        compiler_params=_params(("parallel",)),
        name="state_step",
    )(state_ssm.reshape(nb, D_INNER, D_STATE), to_cols(xdt), to_cols(dec),
      bm.reshape(nb, SSD_GROUPS, D_STATE), cm.reshape(nb, SSD_GROUPS, D_STATE),
      xs.reshape(nblk, bb, D_INNER), d_x)
    return y.reshape(nb, D_INNER), h_new


def _merge_body(x_ref, att_ref, y_ref, z_ref, ga_ref, gb_ref, gn_ref, wa_ref, wb_ref, wo_ref, o_ref):
    att = jnp.concatenate([att_ref[c] for c in range(ATT_SLABS)], axis=1)
    out_a = jnp.dot(att.astype(BF16), wa_ref[...], preferred_element_type=F32)
    z = z_ref[...]
    y = y_ref[...] * (z * jax.nn.sigmoid(z))
    yn = _rmsnorm_rows(y, gn_ref[...]).astype(BF16)
    out_b = jnp.dot(yn, wb_ref[...], preferred_element_type=F32)
    merged = jax.nn.sigmoid(ga_ref[...]) * out_a + jax.nn.sigmoid(gb_ref[...]) * out_b
    o_ref[...] = x_ref[...] + jnp.dot(merged.astype(BF16), wo_ref[...], preferred_element_type=F32)


def _merge(x2d, att, y, rest, ssd_norm, w_att_out, w_ssd_out, w_out, tm=256):
    t = x2d.shape[0]
    tm = min(tm, t)
    row = lambda w, cb: pl.BlockSpec((tm, w), lambda i: (i, cb))
    full = lambda shape: pl.BlockSpec(shape, lambda i: (0, 0))
    return pl.pallas_call(
        _merge_body,
        out_shape=jax.ShapeDtypeStruct((t, D_MODEL), F32),
        grid=(t // tm,),
        in_specs=[row(D_MODEL, 0),
                  pl.BlockSpec((ATT_SLABS, tm, LANES), lambda i: (0, i, 0)),
                  row(D_INNER, 0),
                  row(D_INNER, REST_Z // D_INNER),
                  row(D_MODEL, REST_GATES // D_MODEL), row(D_MODEL, REST_GATES // D_MODEL + 1),
                  full((1, D_INNER)), full((ATT_W, D_MODEL)), full((D_INNER, D_MODEL)),
                  full((D_MODEL, D_MODEL))],
        out_specs=row(D_MODEL, 0),
        compiler_params=_params(("parallel",)),
        name="merge",
    )(x2d, att, y, rest, rest, rest, ssd_norm, w_att_out, w_ssd_out, w_out)


I1_BLOCK = SUBLANES
BF16_ROWS = 2 * SUBLANES
PEER_GATE_TT = 256
PEER_DENSE_TT = 512


def _gelu_tanh(x):
    return 0.5 * x * (1.0 + jnp.tanh(math.sqrt(2.0 / math.pi) * (x + 0.044715 * (x * x * x))))


def _peer_gate_body(x_ref, g_ref, wq_ref, k1_ref, k2_ref,
                    xnt_ref, r2_ref, e2_ref, n1_ref, w1_ref, s_all, v_all, rk1, *, tg):
    nch = tg // LANES
    nkb = N_KEYS // I1_BLOCK
    xn = _rmsnorm_rows(x_ref[...], g_ref[...])
    xnt = xn.T.astype(BF16)
    xnt_ref[...] = xnt
    qt = jnp.dot(wq_ref[...], xnt, preferred_element_type=F32)
    for h in range(PEER_HEADS):
        for sd, kref in enumerate((k1_ref, k2_ref)):
            r0 = (2 * h + sd) * N_KEYS
            s_all[2 * h + sd] = jnp.dot(kref[h], qt[r0:r0 + N_KEYS, :].astype(BF16),
                                        preferred_element_type=F32)

    rowid = lax.broadcasted_iota(jnp.int32, (N_KEYS, LANES), 0).astype(F32)
    row16 = lax.broadcasted_iota(jnp.int32, (PEER_TOPK, LANES), 0)
    row16f = row16.astype(F32)

    def extract(idx, carry):
        h = idx // nch
        off = pl.multiple_of((idx % nch) * LANES, LANES)
        s = [s_all[2 * h + sd, :, pl.ds(off, LANES)] for sd in range(2)]
        rank = [jnp.full((N_KEYS, LANES), float(PEER_TOPK), F32) for _ in range(2)]
        vals = [jnp.zeros((PEER_TOPK, LANES), F32) for _ in range(2)]
        for k in range(PEER_TOPK):
            for sd in range(2):
                m = jnp.max(s[sd], axis=0, keepdims=True)
                first = jnp.min(jnp.where(s[sd] == m, rowid, float(N_KEYS)), axis=0, keepdims=True)
                sel = rowid == first
                rank[sd] = jnp.where(sel, float(k), rank[sd])
                s[sd] = jnp.where(sel, -jnp.inf, s[sd])
                vals[sd] = jnp.where(row16 == k, m, vals[sd])
        rk1[h, :, pl.ds(off, LANES)] = rank[0]
        r2_ref[h, :, :, pl.ds(off, LANES)] = rank[1].reshape(N_KEYS // BF16_ROWS, BF16_ROWS, LANES).astype(BF16)
        v_all[2 * h, :, pl.ds(off, LANES)] = vals[0]
        v_all[2 * h + 1, :, pl.ds(off, LANES)] = vals[1]
        return carry

    lax.fori_loop(0, PEER_HEADS * nch, extract, 0)

    def finish(h, carry):
        for c in range(nch):
            lanes = slice(c * LANES, (c + 1) * LANES)
            v1 = v_all[2 * h, :, lanes]
            v2 = v_all[2 * h + 1, :, lanes]
            shifted = [jnp.broadcast_to(v2[b:b + 1, :], (PEER_TOPK, LANES)) for b in range(PEER_TOPK)]
            cnt = jnp.zeros((PEER_TOPK, LANES), F32)
            zsum = jnp.zeros((1, LANES), F32)
            top = v1[0:1, :] + v2[0:1, :]
            for step in range(PEER_TOPK):
                front = v1 + shifted[0]
                m = jnp.max(front, axis=0, keepdims=True)
                first = jnp.min(jnp.where(front == m, row16f, float(PEER_TOPK)), axis=0, keepdims=True)
                sel = row16f == first
                cnt = cnt + jnp.where(sel, 1.0, 0.0)
                zsum = zsum + jnp.exp(m - top)
                live = PEER_TOPK - 1 - step
                for b in range(live):
                    shifted[b] = jnp.where(sel, shifted[b + 1], shifted[b])
            rz = 1.0 / zsum
            r1 = rk1[h, :, lanes]
            s1 = s_all[2 * h, :, lanes]
            s2 = s_all[2 * h + 1, :, lanes]
            n_of = jnp.zeros((N_KEYS, LANES), F32)
            for a in range(PEER_TOPK):
                n_of = jnp.where(r1 == float(a), cnt[a:a + 1, :], n_of)
            wgt = jnp.exp(s1 - v1[0:1, :]) * rz
            e2v = jnp.exp(s2 - v2[0:1, :])
            e2_ref[h, :, :, lanes] = e2v.reshape(N_KEYS // BF16_ROWS, BF16_ROWS, LANES).astype(BF16)
            for kb in range(nkb):
                rs = slice(kb * I1_BLOCK, (kb + 1) * I1_BLOCK)
                n1_ref[kb, h, :, lanes] = n_of[rs, :]
                w1_ref[kb, h, :, lanes] = wgt[rs, :]
        return carry

    lax.fori_loop(0, PEER_HEADS, finish, 0)


def _peer_dense_body(x_ref, xnt_ref, r2_ref, e2_ref, n1_ref, w1_ref, u_ref, vt_ref, o_ref,
                     acct, act, hbuf, *, tt):
    j = pl.program_id(1)
    nch = tt // LANES
    half = I1_BLOCK // 2 * N_KEYS
    ktiles = N_KEYS // BF16_ROWS

    @pl.when(j == 0)
    def _():
        acct[...] = jnp.zeros_like(acct)

    xnt = xnt_ref[...]
    for hf in range(2):
        act[hf * half:(hf + 1) * half, :] = jnp.dot(u_ref[hf * half:(hf + 1) * half, :], xnt,
                                                    preferred_element_type=F32)
    for hf in range(2):
        for ii in range(hf * I1_BLOCK // 2, (hf + 1) * I1_BLOCK // 2):
            for c in range(nch):
                lanes = slice(c * LANES, (c + 1) * LANES)
                gate = None
                for h in range(PEER_HEADS):
                    n16 = jnp.broadcast_to(n1_ref[h, ii:ii + 1, lanes], (BF16_ROWS, LANES)).astype(BF16)
                    w16 = jnp.broadcast_to(w1_ref[h, ii:ii + 1, lanes], (BF16_ROWS, LANES)).astype(BF16)
                    prod = e2_ref[h, :, :, lanes] * w16[None]
                    term = jnp.where(r2_ref[h, :, :, lanes] < n16[None], prod, jnp.zeros_like(prod))
                    gate = term if gate is None else gate + term
                a = act[ii * N_KEYS:(ii + 1) * N_KEYS, lanes]
                hv = gate.astype(F32).reshape(N_KEYS, LANES) * _gelu_tanh(a)
                hbuf[ii * N_KEYS:(ii + 1) * N_KEYS, lanes] = hv.astype(BF16)
        acct[...] += jnp.dot(vt_ref[:, hf * half:(hf + 1) * half], hbuf[hf * half:(hf + 1) * half, :],
                             preferred_element_type=F32)

    @pl.when(j == pl.num_programs(1) - 1)
    def _():
        o_ref[...] = x_ref[...] + acct[...].T


def _peer(x2d, norm_ffn, wq_t, keys1, keys2, u_bf, vt_bf):
    t = x2d.shape[0]
    tg = min(PEER_GATE_TT, t)
    tt = min(PEER_DENSE_TT, t)
    nkb = N_KEYS // I1_BLOCK
    eb = I1_BLOCK * N_KEYS
    ktiles = N_KEYS // BF16_ROWS
    full1 = lambda shape: pl.BlockSpec(shape, lambda i: (0,) * len(shape))
    tab_shape = jax.ShapeDtypeStruct((PEER_HEADS, ktiles, BF16_ROWS, t), BF16)
    row_shape = jax.ShapeDtypeStruct((nkb, PEER_HEADS, I1_BLOCK, t), F32)
    tab_spec1 = pl.BlockSpec((PEER_HEADS, ktiles, BF16_ROWS, tg), lambda i: (0, 0, 0, i))
    row_spec1 = pl.BlockSpec((nkb, PEER_HEADS, I1_BLOCK, tg), lambda i: (0, 0, 0, i))
    xnt, r2, e2, n1, w1 = pl.pallas_call(
        functools.partial(_peer_gate_body, tg=tg),
        out_shape=(jax.ShapeDtypeStruct((D_MODEL, t), BF16), tab_shape, tab_shape, row_shape, row_shape),
        grid=(t // tg,),
        in_specs=[pl.BlockSpec((tg, D_MODEL), lambda i: (i, 0)),
                  full1((1, D_MODEL)), full1((2 * PEER_HEADS * N_KEYS, D_MODEL)),
                  full1((PEER_HEADS, N_KEYS, N_KEYS)), full1((PEER_HEADS, N_KEYS, N_KEYS))],
        out_specs=(pl.BlockSpec((D_MODEL, tg), lambda i: (0, i)), tab_spec1, tab_spec1, row_spec1, row_spec1),
        scratch_shapes=[pltpu.VMEM((2 * PEER_HEADS, N_KEYS, tg), F32),
                        pltpu.VMEM((2 * PEER_HEADS, PEER_TOPK, tg), F32),
                        pltpu.VMEM((PEER_HEADS, N_KEYS, tg), F32)],
        compiler_params=_params(("parallel",)),
        name="peer_gate",
    )(x2d, norm_ffn, wq_t, keys1, keys2)

    tab_spec = pl.BlockSpec((PEER_HEADS, ktiles, BF16_ROWS, tt), lambda i, j: (0, 0, 0, i))
    row_spec = pl.BlockSpec((None, PEER_HEADS, I1_BLOCK, tt), lambda i, j: (j, 0, 0, i))
    return pl.pallas_call(
        functools.partial(_peer_dense_body, tt=tt),
        out_shape=jax.ShapeDtypeStruct((t, D_MODEL), F32),
        grid=(t // tt, nkb),
        in_specs=[pl.BlockSpec((tt, D_MODEL), lambda i, j: (i, 0)),
                  pl.BlockSpec((D_MODEL, tt), lambda i, j: (0, i)),
                  tab_spec, tab_spec, row_spec, row_spec,
                  pl.BlockSpec((eb, D_MODEL), lambda i, j: (j, 0)),
                  pl.BlockSpec((D_MODEL, eb), lambda i, j: (0, j))],
        out_specs=pl.BlockSpec((tt, D_MODEL), lambda i, j: (i, 0)),
        scratch_shapes=[pltpu.VMEM((D_MODEL, tt), F32),
                        pltpu.VMEM((eb, tt), F32),
                        pltpu.VMEM((eb, tt), BF16)],
        compiler_params=_params(("parallel", "arbitrary")),
        name="peer_dense",
    )(x2d, xnt, r2, e2, n1, w1, u_bf, vt_bf)


def _ple_body(x_ref, p_ref, gp_ref, wg_ref, wp_ref, gf_ref, o_ref):
    x = x_ref[...]
    xn = _rmsnorm_rows(x, gp_ref[...]).astype(BF16)
    pg = jax.nn.sigmoid(jnp.dot(xn, wg_ref[...], preferred_element_type=F32))
    x3 = x + pg * jnp.dot(p_ref[...].astype(BF16), wp_ref[...], preferred_element_type=F32)
    o_ref[...] = _rmsnorm_rows(x3, gf_ref[...])


def _ple_final(x2d, p2d, norm_ple, w_gate, w_proj, norm_final, tm=512):
    t = x2d.shape[0]
    tm = min(tm, t)
    full = lambda shape: pl.BlockSpec(shape, lambda i: (0, 0))
    return pl.pallas_call(
        _ple_body,
        out_shape=jax.ShapeDtypeStruct((t, D_MODEL), F32),
        grid=(t // tm,),
        in_specs=[pl.BlockSpec((tm, D_MODEL), lambda i: (i, 0)),
                  pl.BlockSpec((tm, PLE_DIM), lambda i: (i, 0)),
                  full((1, D_MODEL)), full((D_MODEL, D_MODEL)), full((PLE_DIM, D_MODEL)),
                  full((1, D_MODEL))],
        out_specs=pl.BlockSpec((tm, D_MODEL), lambda i: (i, 0)),
        compiler_params=_params(("parallel",)),
        name="ple_final",
    )(x2d, p2d, norm_ple, w_gate, w_proj, norm_final)


def _prep_weights(norm_mix, w_in, conv_w, conv_b, dt_bias, a_log, d_skip, ssd_norm, w_att_out,
                  w_ssd_out, w_out, norm_ffn, w_peer_q, peer_keys1, peer_keys2, peer_u, peer_v,
                  norm_ple, w_ple_gate, w_ple_proj, norm_final):
    w = w_in[0]
    c = [0, Q_COLS, 2 * Q_COLS, 3 * Q_COLS, 3 * Q_COLS + D_INNER, 3 * Q_COLS + D_INNER + CONV_CH,
         3 * Q_COLS + D_INNER + CONV_CH + SSD_HEADS]
    wq, wk, wv, wz, wxbc, wdt, wgt = (w[:, c[0]:c[1]], w[:, c[1]:c[2]], w[:, c[2]:c[3]], w[:, c[3]:c[4]],
                                      w[:, c[4]:c[5]], w[:, c[5]:c[6]], w[:, c[6]:])
    row = lambda v: v.reshape(1, -1).astype(F32)
    p = {}
    p["norm_mix"] = row(norm_mix[0])
    p["w_qkv"] = jnp.concatenate([wq * (HEAD_DIM ** -0.5), wk, wv], axis=1).astype(BF16)
    p["w_rest"] = jnp.concatenate([wxbc, wz, wgt], axis=1).astype(BF16)
    p["w_dt"] = jnp.pad(wdt, ((0, 0), (0, LANES - SSD_HEADS))).astype(BF16)
    p["dt_bias"] = jnp.pad(row(dt_bias[0]), ((0, 0), (0, LANES - SSD_HEADS)))
    p["conv_w"] = conv_w[0].astype(F32)
    p["conv_b"] = row(conv_b[0])
    p["a_row"], p["a_x"], p["d_x"], p["emat"], p["ltri"] = _ssd_consts(a_log[0], d_skip[0])
    p["ssd_norm"] = row(ssd_norm[0])
    p["w_att_out"] = w_att_out[0].astype(BF16)
    p["w_ssd_out"] = w_ssd_out[0].astype(BF16)
    p["w_out"] = w_out[0].astype(BF16)
    p["norm_ffn"] = row(norm_ffn[0])
    p["wq_t"] = w_peer_q[0].T.astype(BF16)
    p["keys1"] = peer_keys1[0].astype(BF16)
    p["keys2"] = peer_keys2[0].astype(BF16)
    p["u"] = peer_u[0].astype(BF16)
    p["vt"] = peer_v[0].T.astype(BF16)
    p["norm_ple"] = row(norm_ple[0])
    p["w_ple_gate"] = w_ple_gate[0].astype(BF16)
    p["w_ple_proj"] = w_ple_proj[0].astype(BF16)
    p["norm_final"] = row(norm_final)
    return p


def _tail(x2d, att, y, rest, p2d, p):
    x1 = _merge(x2d, att, y, rest, p["ssd_norm"], p["w_att_out"], p["w_ssd_out"], p["w_out"])
    x2 = _peer(x1, p["norm_ffn"], p["wq_t"], p["keys1"], p["keys2"], p["u"], p["vt"])
    return _ple_final(x2, p2d, p["norm_ple"], p["w_ple_gate"], p["w_ple_proj"], p["norm_final"])


def _heads_from_slabs(qkv, which, g, b, s, rows):
    s0 = which * (Q_COLS // LANES) + g * ATT_SLABS
    x = qkv.reshape(QKV_COLS // LANES, b, s, LANES)[s0:s0 + ATT_SLABS, :, s - rows:]
    return x.transpose(1, 2, 0, 3).reshape(b, rows, N_HEADS, HEAD_DIM)


def _kv_rows(qkv, g, b, s, rows):
    return jnp.stack([_heads_from_slabs(qkv, 1, g, b, s, rows),
                      _heads_from_slabs(qkv, 2, g, b, s, rows)], axis=2)[None]


def _prompt(x, p_in, p):
    b, s, _ = x.shape
    x2d = x.reshape(b * s, D_MODEL)
    qkv, rest, dt = _projections(x2d, jnp.arange(s, dtype=jnp.int32), p["norm_mix"], p["w_qkv"],
                                 p["w_rest"], p["w_dt"], p["dt_bias"], tm=PROJ_TM)
    att = _attn_prompt(qkv, b, s)
    y, ssm = _ssd_prompt(rest, dt, p["conv_w"], p["conv_b"], p["a_row"], p["d_x"], p["emat"], p["ltri"], b, s)
    out = _tail(x2d, att, y, rest, p_in[0].reshape(b * s, PLE_DIM), p)
    kvs = [_kv_rows(qkv, g, b, s, min(win, s)) for g, (win, _) in enumerate(ATT_GROUPS)]
    conv = rest.reshape(b, s, REST_COLS)[:, s - (CONV_W - 1):, :CONV_CH][None]
    ssm = ssm.reshape(1, b, SSD_HEADS, CHUNK, D_STATE)
    return out.reshape(b, s, D_MODEL), kvs, conv, ssm


def _sample(x, p_in, caches, state_conv, state_ssm, p):
    b, s, _ = x.shape
    assert s == 1
    x2d = x.reshape(b, D_MODEL)
    pos = jnp.full((b,), PAST_LEN, dtype=jnp.int32)
    qkv, rest, dt = _projections(x2d, pos, p["norm_mix"], p["w_qkv"], p["w_rest"], p["w_dt"],
                                 p["dt_bias"], tm=b)
    rows = lambda which: jnp.stack([_heads_from_slabs(qkv, which, g, b, 1, 1).reshape(b, ATT_W)
                                    for g in range(N_GROUPS)], axis=1)
    q3, k3, v3 = rows(0), rows(1), rows(2)
    att = _attn_step(q3, k3, v3, [c[0] for c in caches])
    att = att.reshape(b, ATT_SLABS, LANES).transpose(1, 0, 2)
    xbc = rest[:, :CONV_CH]
    y, ssm = _ssd_step(xbc, dt, state_conv[0], state_ssm[0],
                       p["conv_w"], p["conv_b"], p["a_x"], p["d_x"], p["emat"])
    out = _tail(x2d, att, y, rest, p_in[0].reshape(b, PLE_DIM), p)
    hd = lambda a, g: a[:, g].reshape(b, N_HEADS, HEAD_DIM)
    kvs = [jnp.stack([hd(k3, g), hd(v3, g)], axis=1)[None, :, None] for g in range(N_GROUPS)]
    conv = jnp.concatenate([state_conv[0][:, 1:], xbc[:, None, :]], axis=1)[None]
    ssm = ssm.reshape(1, b, SSD_HEADS, CHUNK, D_STATE)
    return out.reshape(b, 1, D_MODEL), kvs, conv, ssm


def kernel(x_prompt, x_sample, cache_kv_w128, cache_kv_w512, cache_kv_w2048, state_conv, state_ssm, p_prompt, p_sample, norm_mix, w_in, conv_w, conv_b, dt_bias, a_log, d_skip, ssd_norm, w_att_out, w_ssd_out, w_out, norm_ffn, w_peer_q, peer_keys1, peer_keys2, peer_u, peer_v, norm_ple, w_ple_gate, w_ple_proj, norm_final):
    p = _prep_weights(norm_mix, w_in, conv_w, conv_b, dt_bias, a_log, d_skip, ssd_norm, w_att_out,
                      w_ssd_out, w_out, norm_ffn, w_peer_q, peer_keys1, peer_keys2, peer_u, peer_v,
                      norm_ple, w_ple_gate, w_ple_proj, norm_final)
    y_p, kv_p, conv_p, ssm_p = _prompt(x_prompt, p_prompt, p)
    y_s, kv_s, conv_s, ssm_s = _sample(x_sample, p_sample, (cache_kv_w128, cache_kv_w512, cache_kv_w2048),
                                       state_conv, state_ssm, p)
    return (y_p, y_s, kv_p[0], kv_p[1], kv_p[2], conv_p, ssm_p, kv_s[0], kv_s[1], kv_s[2], conv_s, ssm_s)
```

```python
import functools
import math

import jax
import jax.numpy as jnp
from jax import lax
from jax.experimental import pallas as pl
from jax.experimental.pallas import tpu as pltpu

F32 = jnp.float32
BF16 = jnp.bfloat16

LANES = 128
SUBLANES = 8
VMEM_BYTES_V7X = 64 * 1024 * 1024
VMEM_LIMIT = VMEM_BYTES_V7X * 3 // 4

D_MODEL = 1024
ATT_GROUPS = ((128, 1), (512, 4), (2048, 16))
N_GROUPS = len(ATT_GROUPS)
N_HEADS = 8
HEAD_DIM = 64
ATT_W = N_HEADS * HEAD_DIM
ATT_SLABS = ATT_W // LANES
Q_COLS = N_GROUPS * ATT_W
QKV_COLS = 3 * Q_COLS
ROPE_THETA = 10000.0
D_INNER = 2048
SSD_HEADS = 32
SSD_GROUPS = 8
SSD_GW = D_INNER // SSD_GROUPS
D_STATE = 128
CONV_W = 4
CONV_CH = D_INNER + 2 * SSD_GROUPS * D_STATE
CHUNK = 64
N_KEYS = 128
N_EXPERTS = N_KEYS * N_KEYS
PEER_HEADS = 8
PEER_TOPK = 16
PLE_DIM = 256
EPS = 1e-6
PAST_LEN = 8192
NEG = -1e30

REST_Z = CONV_CH
REST_GATES = REST_Z + D_INNER
REST_COLS = REST_GATES + 2 * D_MODEL
QKV_TN = 768
REST_TN = 1024
PROJ_TM = 1024
ATT_ROWS = 2048
ATT_ILP = 8
SSD_STEP_CHUNKS = 4


def _params(sem):
    return pltpu.CompilerParams(dimension_semantics=sem, vmem_limit_bytes=VMEM_LIMIT)


def _rmsnorm_rows(x, g):
    return x * lax.rsqrt(jnp.mean(x * x, axis=-1, keepdims=True) + EPS) * g


def _nt(a, b):
    return lax.dot_general(a, b, (((1,), (1,)), ((), ())), preferred_element_type=F32)


def _proj_qkv_body(x_ref, g_ref, w_ref, cos_ref, sin_ref, o_ref, xn_ref, *, rope_tiles):
    j = pl.program_id(1)

    @pl.when(j == 0)
    def _():
        xn_ref[...] = _rmsnorm_rows(x_ref[...], g_ref[...]).astype(BF16)

    acc = jnp.dot(xn_ref[...], w_ref[...], preferred_element_type=F32)
    n_slab = acc.shape[1] // LANES

    @pl.when(j < rope_tiles)
    def _():
        cos = cos_ref[...]
        sin = sin_ref[...]
        lane = lax.broadcasted_iota(jnp.int32, cos.shape, 1)
        first_half = (lane & (HEAD_DIM - 1)) < HEAD_DIM // 2
        for c in range(n_slab):
            a = acc[:, c * LANES:(c + 1) * LANES]
            partner = jnp.where(first_half, pltpu.roll(a, LANES - HEAD_DIM // 2, 1),
                                pltpu.roll(a, HEAD_DIM // 2, 1))
            o_ref[c] = a * cos + partner * sin

    @pl.when(j >= rope_tiles)
    def _():
        for c in range(n_slab):
            o_ref[c] = acc[:, c * LANES:(c + 1) * LANES]


def _proj_plain_body(x_ref, g_ref, w_ref, o_ref, xn_ref):
    @pl.when(pl.program_id(1) == 0)
    def _():
        xn_ref[...] = _rmsnorm_rows(x_ref[...], g_ref[...]).astype(BF16)

    o_ref[...] = jnp.dot(xn_ref[...], w_ref[...], preferred_element_type=F32)


def _proj_dt_body(x_ref, g_ref, w_ref, bias_ref, o_ref):
    xn = _rmsnorm_rows(x_ref[...], g_ref[...]).astype(BF16)
    v = jnp.dot(xn, w_ref[...], preferred_element_type=F32) + bias_ref[...]
    o_ref[...] = jnp.maximum(v, 0.0) + jnp.log1p(jnp.exp(-jnp.abs(v)))


def _rope_tables(pos):
    half = HEAD_DIM // 2
    inv = jnp.exp(jnp.arange(half, dtype=F32) * (-2.0 * math.log(ROPE_THETA) / HEAD_DIM))
    ang = pos.astype(F32)[:, None] * inv[None, :]
    cos, sin = jnp.cos(ang), jnp.sin(ang)
    cos_t = jnp.concatenate([cos, cos, cos, cos], axis=1)
    sin_t = jnp.concatenate([-sin, sin, -sin, sin], axis=1)
    return cos_t, sin_t


def _projections(x2d, pos, norm_mix, w_qkv, w_rest, w_dt, dt_bias_row, tm):
    t, k = x2d.shape
    tm = min(tm, t)
    cos_t, sin_t = _rope_tables(pos)
    n_pos_blocks = pos.shape[0] // tm
    x_spec = pl.BlockSpec((tm, k), lambda i, j: (i, 0))
    g_spec = pl.BlockSpec((1, k), lambda i, j: (0, 0))
    tab_spec = pl.BlockSpec((tm, LANES), lambda i, j: (i % n_pos_blocks, 0))
    slabs = QKV_TN // LANES
    qkv = pl.pallas_call(
        functools.partial(_proj_qkv_body, rope_tiles=2 * Q_COLS // QKV_TN),
        out_shape=jax.ShapeDtypeStruct((QKV_COLS // LANES, t, LANES), F32),
        grid=(t // tm, QKV_COLS // QKV_TN),
        in_specs=[x_spec, g_spec, pl.BlockSpec((k, QKV_TN), lambda i, j: (0, j)), tab_spec, tab_spec],
        out_specs=pl.BlockSpec((slabs, tm, LANES), lambda i, j: (j, i, 0)),
        scratch_shapes=[pltpu.VMEM((tm, k), BF16)],
        compiler_params=_params(("parallel", "arbitrary")),
        name="proj_qkv",
    )(x2d, norm_mix, w_qkv, cos_t, sin_t)
    rest = pl.pallas_call(
        _proj_plain_body,
        out_shape=jax.ShapeDtypeStruct((t, REST_COLS), F32),
        grid=(t // tm, REST_COLS // REST_TN),
        in_specs=[x_spec, g_spec, pl.BlockSpec((k, REST_TN), lambda i, j: (0, j))],
        out_specs=pl.BlockSpec((tm, REST_TN), lambda i, j: (i, j)),
        scratch_shapes=[pltpu.VMEM((tm, k), BF16)],
        compiler_params=_params(("parallel", "arbitrary")),
        name="proj_rest",
    )(x2d, norm_mix, w_rest)
    dt = pl.pallas_call(
        _proj_dt_body,
        out_shape=jax.ShapeDtypeStruct((t, LANES), F32),
        grid=(t // tm,),
        in_specs=[pl.BlockSpec((tm, k), lambda i: (i, 0)), pl.BlockSpec((1, k), lambda i: (0, 0)),
                  pl.BlockSpec((k, LANES), lambda i: (0, 0)), pl.BlockSpec((1, LANES), lambda i: (0, 0))],
        out_specs=pl.BlockSpec((tm, LANES), lambda i: (i, 0)),
        compiler_params=_params(("parallel",)),
        name="proj_dt",
    )(x2d, norm_mix, w_dt, dt_bias_row)
    return qkv, rest, dt


def _attn_prompt_body(*refs, dil, has_prev_group, emit_lse):
    q_ref, kc_ref, kp_ref, vc_ref, vp_ref = refs[:5]
    pos = 5
    if has_prev_group:
        op_ref, lp_ref = refs[pos:pos + 2]
        pos += 2
    o_ref = refs[pos]
    pos += 1
    if emit_lse:
        l_ref = refs[pos]
        pos += 1
    kbuf, vbuf = refs[pos:pos + 2]

    i = pl.program_id(1)
    rows = q_ref.shape[0]
    span = LANES * dil
    kbuf[0:span, :] = kp_ref[rows - span:rows, :]
    kbuf[span:span + rows, :] = kc_ref[...]
    vbuf[0:span, :] = vp_ref[rows - span:rows, :]
    vbuf[span:span + rows, :] = vc_ref[...]

    krow = lax.broadcasted_iota(jnp.int32, (2 * LANES, 2 * LANES), 0)
    qidx = lax.broadcasted_iota(jnp.int32, (2 * LANES, 2 * LANES), 1) & (LANES - 1)
    own_ok = jnp.logical_and(krow >= LANES, krow - LANES <= qidx)
    prev_ok = jnp.logical_and(krow < LANES, krow >= qidx)
    low_half = lax.broadcasted_iota(jnp.int32, (LANES, LANES), 1) < HEAD_DIM

    def rd(ref, start):
        if dil == 1:
            return ref[pl.ds(start, LANES), :]
        return ref[pl.ds(start, LANES, stride=dil), :]

    def sub_blocks(p, carry):
        us = [p * ATT_ILP + k for k in range(ATT_ILP)]
        sps = [u // dil for u in us]
        starts = [sp * span + (u % dil) for u, sp in zip(us, sps)]
        scs, vsts = [], []
        for sp, start in zip(sps, starts):
            q2 = rd(q_ref, start)
            qs = jnp.concatenate([jnp.where(low_half, q2, 0.0), jnp.where(low_half, 0.0, q2)], axis=0).astype(BF16)
            ks = jnp.concatenate([rd(kbuf, start), rd(kbuf, span + start)], axis=0).astype(BF16)
            mask = jnp.logical_or(own_ok, jnp.logical_and(prev_ok, jnp.logical_or(i > 0, sp > 0)))
            scs.append(jnp.where(mask, _nt(ks, qs), NEG))
        for start in starts:
            vsts.append(jnp.concatenate([rd(vbuf, start), rd(vbuf, span + start)], axis=0).T.astype(BF16))
        es, dens, lses = [], [], []
        for sc in scs:
            m = jnp.max(sc, axis=0, keepdims=True)
            e = jnp.exp(sc - m)
            den = jnp.sum(e, axis=0, keepdims=True)
            es.append(e.astype(BF16))
            dens.append(den)
            lses.append(m + jnp.log(den))
        ots = [jnp.dot(vst, e, preferred_element_type=F32) for vst, e in zip(vsts, es)]
        for start, ot, den, lse in zip(starts, ots, dens, lses):
            ot = jnp.concatenate([ot[:HEAD_DIM, :LANES] / den[:, :LANES],
                                  ot[HEAD_DIM:, LANES:] / den[:, LANES:]], axis=0)
            lt = jnp.concatenate([jnp.broadcast_to(lse[:, :LANES], (HEAD_DIM, LANES)),
                                  jnp.broadcast_to(lse[:, LANES:], (HEAD_DIM, LANES))], axis=0)
            o2 = ot.T
            l2 = lt.T
            if has_prev_group:
                o_prev = rd(op_ref, start)
                l_prev = rd(lp_ref, start)
                mm = jnp.maximum(l_prev, l2)
                wp = jnp.exp(l_prev - mm)
                wc = jnp.exp(l2 - mm)
                tot = wp + wc
                o2 = (wp * o_prev + wc * o2) / tot
                l2 = mm + jnp.log(tot)
            if dil == 1:
                o_ref[pl.ds(start, LANES), :] = o2
                if emit_lse:
                    l_ref[pl.ds(start, LANES), :] = l2
            else:
                o_ref[pl.ds(start, LANES, stride=dil), :] = o2
                if emit_lse:
                    l_ref[pl.ds(start, LANES, stride=dil), :] = l2
        return carry

    lax.fori_loop(0, rows // (LANES * ATT_ILP), sub_blocks, 0)


def _attn_prompt_group(qkv, g, dil, b, s, prev):
    rows = min(ATT_ROWS, s)
    assert rows % (LANES * dil) == 0 and rows % (LANES * ATT_ILP) == 0 and s % rows == 0
    qkv4 = qkv.reshape(QKV_COLS // LANES, b, s, LANES)
    kslab = Q_COLS // LANES
    vslab = 2 * Q_COLS // LANES
    blk = (None, None, rows, LANES)
    in_specs = [
        pl.BlockSpec(blk, lambda bb, i, c: (g * ATT_SLABS + c, bb, i, 0)),
        pl.BlockSpec(blk, lambda bb, i, c: (kslab + g * ATT_SLABS + c, bb, i, 0)),
        pl.BlockSpec(blk, lambda bb, i, c: (kslab + g * ATT_SLABS + c, bb, jnp.maximum(i - 1, 0), 0)),
        pl.BlockSpec(blk, lambda bb, i, c: (vslab + g * ATT_SLABS + c, bb, i, 0)),
        pl.BlockSpec(blk, lambda bb, i, c: (vslab + g * ATT_SLABS + c, bb, jnp.maximum(i - 1, 0), 0)),
    ]
    args = [qkv4] * 5
    o_spec = pl.BlockSpec(blk, lambda bb, i, c: (c, bb, i, 0))
    if prev is not None:
        in_specs += [o_spec, o_spec]
        args += [prev[0].reshape(ATT_SLABS, b, s, LANES), prev[1].reshape(ATT_SLABS, b, s, LANES)]
    emit_lse = g < N_GROUPS - 1
    shape = jax.ShapeDtypeStruct((ATT_SLABS, b, s, LANES), F32)
    body = functools.partial(_attn_prompt_body, dil=dil, has_prev_group=prev is not None, emit_lse=emit_lse)
    out = pl.pallas_call(
        body,
        out_shape=(shape, shape) if emit_lse else shape,
        grid=(b, s // rows, ATT_SLABS),
        in_specs=in_specs,
        out_specs=(o_spec, o_spec) if emit_lse else o_spec,
        scratch_shapes=[pltpu.VMEM((LANES * dil + rows, LANES), F32),
                        pltpu.VMEM((LANES * dil + rows, LANES), F32)],
        compiler_params=_params(("parallel", "parallel", "parallel")),
        name="attn_prompt_g%d" % g,
    )(*args)
    if emit_lse:
        return out[0].reshape(ATT_SLABS, b * s, LANES), out[1].reshape(ATT_SLABS, b * s, LANES)
    return out.reshape(ATT_SLABS, b * s, LANES)


def _attn_prompt(qkv, b, s):
    prev = None
    for g, (win, dil) in enumerate(ATT_GROUPS):
        assert win // dil == LANES
        prev = _attn_prompt_group(qkv, g, dil, b, s, prev)
    return prev


def _split3(a):
    hi = a.astype(BF16)
    r1 = a - hi.astype(F32)
    mid = r1.astype(BF16)
    lo = (r1 - mid.astype(F32)).astype(BF16)
    return hi, mid, lo


def _expand_heads(a, emat):
    return sum(jnp.dot(part, emat, preferred_element_type=F32) for part in _split3(a))


def _ssd_prompt_body(xbc_ref, dt_ref, cw_ref, cb_ref, a_ref, dx_ref, e_ref, ltri_ref,
                     y_ref, st_ref, xext, h_ref, *, n_chunks):
    i = pl.program_id(1)
    lc = n_chunks * CHUNK
    halo = SUBLANES

    @pl.when(i == 0)
    def _():
        xext[0:halo, :] = jnp.zeros((halo, CONV_CH), F32)
        h_ref[...] = jnp.zeros_like(h_ref)

    xext[halo:halo + lc, :] = xbc_ref[...]

    lane_x = lax.broadcasted_iota(jnp.int32, (CHUNK, D_INNER), 1) & (CHUNK - 1)
    row_x = lax.broadcasted_iota(jnp.int32, (CHUNK, D_INNER), 0)
    diag = lane_x == row_x
    tril = row_x >= lane_x
    lane_p = lax.broadcasted_iota(jnp.int32, (CHUNK, LANES), 1)
    low_half = lane_p < CHUNK
    a_row = a_ref[...]
    dx = dx_ref[...]
    emat = e_ref[...]
    ltri = ltri_ref[...]

    for c in range(n_chunks):
        o = c * CHUNK
        conv = cb_ref[...] + xext[halo - 3 + o:halo - 3 + o + CHUNK, :] * cw_ref[0:1, :]
        conv = conv + xext[halo - 2 + o:halo - 2 + o + CHUNK, :] * cw_ref[1:2, :]
        conv = conv + xext[halo - 1 + o:halo - 1 + o + CHUNK, :] * cw_ref[2:3, :]
        conv = conv + xext[halo + o:halo + o + CHUNK, :] * cw_ref[3:4, :]
        xc = conv * jax.nn.sigmoid(conv)
        xs = xc[:, :D_INNER]
        bm = xc[:, D_INNER:D_INNER + SSD_GROUPS * D_STATE]
        cm = xc[:, D_INNER + SSD_GROUPS * D_STATE:]

        dt = dt_ref[o:o + CHUNK, :]
        dtx = _expand_heads(dt, emat)
        acs = sum(jnp.dot(ltri, part, preferred_element_type=F32) for part in _split3(dt * a_row))
        acsx = _expand_heads(acs, emat)
        last = acsx[CHUNK - 1:CHUNK, :]
        xdt = xs * dtx
        xw = (xdt * jnp.exp(last - acsx)).astype(BF16)
        eacs = jnp.exp(acsx)
        cdec = jnp.exp(last)
        rrow = jnp.sum(jnp.where(diag, acsx, 0.0), axis=0, keepdims=True)
        lm = jnp.exp(jnp.where(tril, acsx - rrow, NEG))
        xdtb = xdt.astype(BF16)
        cmb = cm.astype(BF16)

        for g in range(SSD_GROUPS):
            gs = slice(g * SSD_GW, (g + 1) * SSD_GW)
            bg = bm[:, g * D_STATE:(g + 1) * D_STATE]
            bgb = bg.astype(BF16)
            cg = cmb[:, g * D_STATE:(g + 1) * D_STATE]
            cbx = _nt(cg, jnp.concatenate([bgb] * (SSD_GW // CHUNK), axis=0))
            wg = (cbx * lm[:, gs]).astype(BF16)
            ydiag = []
            for pr in range(SSD_GW // LANES):
                xp = xdtb[:, g * SSD_GW + pr * LANES:g * SSD_GW + (pr + 1) * LANES]
                zero = jnp.zeros_like(xp)
                bd = jnp.concatenate([jnp.where(low_half, xp, zero), jnp.where(low_half, zero, xp)], axis=0)
                ydiag.append(jnp.dot(wg[:, pr * LANES:(pr + 1) * LANES], bd, preferred_element_type=F32))
            hp = h_ref[g]
            yoff = jnp.dot(cg, hp.astype(BF16), preferred_element_type=F32) * eacs[:, gs]
            st = jnp.dot(bg.T.astype(BF16), xw[:, gs], preferred_element_type=F32)
            h_ref[g] = hp * cdec[:, gs] + st
            y_ref[o:o + CHUNK, gs] = jnp.concatenate(ydiag, axis=1) + yoff + dx[:, gs] * xs[:, gs]

    xext[0:halo, :] = xext[lc:lc + halo, :]

    @pl.when(i == pl.num_programs(1) - 1)
    def _():
        for g in range(SSD_GROUPS):
            st_ref[g * SSD_GW:(g + 1) * SSD_GW, :] = h_ref[g].T


def _ssd_consts(a_log, d_skip):
    a = -jnp.exp(a_log.astype(F32))
    a_row = jnp.pad(a, (0, LANES - SSD_HEADS))[None, :]
    a_x = jnp.repeat(a, CHUNK)[None, :]
    d_x = jnp.repeat(d_skip.astype(F32), CHUNK)[None, :]
    emat = (jnp.arange(D_INNER)[None, :] // CHUNK == jnp.arange(LANES)[:, None]).astype(BF16)
    ltri = (jnp.arange(CHUNK)[:, None] >= jnp.arange(CHUNK)[None, :]).astype(BF16)
    return a_row, a_x, d_x, emat, ltri


def _ssd_prompt(rest, dt, conv_w, conv_b, a_row, d_x, emat, ltri, b, s, n_chunks=SSD_STEP_CHUNKS):
    lc = n_chunks * CHUNK
    rest_v = rest.reshape(b, s, REST_COLS)
    dt_v = dt.reshape(b, s, LANES)
    const = lambda shape: pl.BlockSpec(shape, lambda bb, i: (0,) * len(shape))
    body = functools.partial(_ssd_prompt_body, n_chunks=n_chunks)
    y, st = pl.pallas_call(
        body,
        out_shape=(jax.ShapeDtypeStruct((b, s, D_INNER), F32),
                   jax.ShapeDtypeStruct((b, D_INNER, D_STATE), F32)),
        grid=(b, s // lc),
        in_specs=[pl.BlockSpec((None, lc, CONV_CH), lambda bb, i: (bb, i, 0)),
                  pl.BlockSpec((None, lc, LANES), lambda bb, i: (bb, i, 0)),
                  const((CONV_W, CONV_CH)), const((1, CONV_CH)), const((1, LANES)),
                  const((1, D_INNER)), const((LANES, D_INNER)), const((CHUNK, CHUNK))],
        out_specs=(pl.BlockSpec((None, lc, D_INNER), lambda bb, i: (bb, i, 0)),
                   pl.BlockSpec((None, D_INNER, D_STATE), lambda bb, i: (bb, 0, 0))),
        scratch_shapes=[pltpu.VMEM((lc + SUBLANES, CONV_CH), F32),
                        pltpu.VMEM((SSD_GROUPS, D_STATE, SSD_GW), F32)],
        compiler_params=_params(("parallel", "arbitrary")),
        name="ssd_prompt",
    )(rest_v, dt_v, conv_w, conv_b, a_row, d_x, emat, ltri)
    return y.reshape(b * s, D_INNER), st


def _attn_step_body(q_ref, kn_ref, vn_ref, c0_ref, c1_ref, c2_ref, o_ref):
    caches = (c0_ref, c1_ref, c2_ref)
    hrow = lax.broadcasted_iota(jnp.int32, (N_HEADS, ATT_W), 0)
    hlane = lax.broadcasted_iota(jnp.int32, (N_HEADS, ATT_W), 1) // HEAD_DIM
    hmask = hrow == hlane
    s_list, sn_list = [], []
    for g, (win, dil) in enumerate(ATT_GROUPS):
        qbd = jnp.where(hmask, jnp.broadcast_to(q_ref[g:g + 1, :], (N_HEADS, ATT_W)), 0.0).astype(BF16)
        s = jnp.dot(qbd, caches[g][0].astype(BF16), preferred_element_type=F32)
        if dil > 1:
            wpos = lax.broadcasted_iota(jnp.int32, s.shape, 1)
            s = jnp.where((wpos & (dil - 1)) == 0, s, NEG)
        s_list.append(s)
        sn_list.append(jnp.sum(qbd.astype(F32) * kn_ref[g:g + 1, :].astype(BF16).astype(F32),
                               axis=1, keepdims=True))
    m = sn_list[0]
    for g in range(N_GROUPS):
        m = jnp.maximum(m, jnp.maximum(jnp.max(s_list[g], axis=1, keepdims=True), sn_list[g]))
    den = jnp.zeros((N_HEADS, 1), F32)
    acc = jnp.zeros((N_HEADS, ATT_W), F32)
    for g in range(N_GROUPS):
        e = jnp.exp(s_list[g] - m)
        en = jnp.exp(sn_list[g] - m)
        den = den + jnp.sum(e, axis=1, keepdims=True) + en
        acc = acc + _nt(e.astype(BF16), caches[g][1].astype(BF16))
        acc = acc + en.astype(BF16).astype(F32) * vn_ref[g:g + 1, :].astype(BF16).astype(F32)
    o_ref[...] = jnp.sum(jnp.where(hmask, acc / den, 0.0), axis=0, keepdims=True)


def _attn_step(q, kn, vn, caches):
    nb = q.shape[0]
    views, specs = [], []
    for (win, dil), c in zip(ATT_GROUPS, caches):
        w = c.shape[1]
        assert w == win, "cache must hold exactly one window of past rows"
        views.append(c.transpose(0, 2, 3, 4, 1).reshape(nb, 2, ATT_W, w))
        specs.append(pl.BlockSpec((None, 2, ATT_W, w), lambda i: (i, 0, 0, 0)))
    qspec = pl.BlockSpec((None, N_GROUPS, ATT_W), lambda i: (i, 0, 0))
    out = pl.pallas_call(
        _attn_step_body,
        out_shape=jax.ShapeDtypeStruct((nb, 1, ATT_W), F32),
        grid=(nb,),
        in_specs=[qspec, qspec, qspec, *specs],
        out_specs=pl.BlockSpec((None, 1, ATT_W), lambda i: (i, 0, 0)),
        compiler_params=_params(("parallel",)),
        name="attn_step",
    )(q, kn, vn, *views)
    return out.reshape(nb, ATT_W)


def _conv_step_body(rest_ref, sc_ref, dt_ref, cw_ref, cb_ref, ax_ref, e_ref,
                    xs_ref, bm_ref, cm_ref, xdt_ref, dec_ref):
    conv = cb_ref[...] + sc_ref[:, 0:CONV_CH] * cw_ref[0:1, :]
    conv = conv + sc_ref[:, CONV_CH:2 * CONV_CH] * cw_ref[1:2, :]
    conv = conv + sc_ref[:, 2 * CONV_CH:3 * CONV_CH] * cw_ref[2:3, :]
    conv = conv + rest_ref[...] * cw_ref[3:4, :]
    xc = conv * jax.nn.sigmoid(conv)
    xs = xc[:, :D_INNER]
    dtx = _expand_heads(dt_ref[...], e_ref[...])
    xs_ref[...] = xs
    bm_ref[...] = xc[:, D_INNER:D_INNER + SSD_GROUPS * D_STATE]
    cm_ref[...] = xc[:, D_INNER + SSD_GROUPS * D_STATE:]
    xdt_ref[...] = xs * dtx
    dec_ref[...] = jnp.exp(dtx * ax_ref[...])


def _state_step_body(h_ref, xdt_ref, dec_ref, b_ref, c_ref, xs_ref, dx_ref, ho_ref, y_ref, *, bb):
    grow = lax.broadcasted_iota(jnp.int32, (SSD_GROUPS, D_INNER), 0)
    glane = lax.broadcasted_iota(jnp.int32, (SSD_GROUPS, D_INNER), 1) // SSD_GW
    gmask = grow == glane
    for bi in range(bb):
        h = h_ref[bi]
        bmat = b_ref[bi]
        bx = jnp.concatenate([jnp.broadcast_to(bmat[g:g + 1, :], (SSD_GW, D_STATE))
                              for g in range(SSD_GROUPS)], axis=0)
        hn = h * dec_ref[:, bi:bi + 1] + xdt_ref[:, bi:bi + 1] * bx
        ho_ref[bi] = hn
        y8 = _nt(c_ref[bi].astype(BF16), hn.astype(BF16))
        y = jnp.sum(jnp.where(gmask, y8, 0.0), axis=0, keepdims=True)
        y_ref[bi:bi + 1, :] = y + dx_ref[...] * xs_ref[bi:bi + 1, :]


def _ssd_step(rest, dt, state_conv, state_ssm, conv_w, conv_b, a_x, d_x, emat, bb=4):
    nb = rest.shape[0]
    nblk = nb // bb
    full = lambda shape: pl.BlockSpec(shape, lambda i: (0,) * len(shape))
    xs, bm, cm, xdt, dec = pl.pallas_call(
        _conv_step_body,
        out_shape=(jax.ShapeDtypeStruct((nb, D_INNER), F32),
                   jax.ShapeDtypeStruct((nb, SSD_GROUPS * D_STATE), F32),
                   jax.ShapeDtypeStruct((nb, SSD_GROUPS * D_STATE), F32),
                   jax.ShapeDtypeStruct((nb, D_INNER), F32),
                   jax.ShapeDtypeStruct((nb, D_INNER), F32)),
        grid=(1,),
        in_specs=[pl.BlockSpec((nb, CONV_CH), lambda i: (0, 0)),
                  full((nb, (CONV_W - 1) * CONV_CH)), full((nb, LANES)),
                  full((CONV_W, CONV_CH)), full((1, CONV_CH)), full((1, D_INNER)),
                  full((LANES, D_INNER))],
        out_specs=(full((nb, D_INNER)), full((nb, SSD_GROUPS * D_STATE)),
                   full((nb, SSD_GROUPS * D_STATE)), full((nb, D_INNER)), full((nb, D_INNER))),
        compiler_params=_params(("arbitrary",)),
        name="conv_step",
    )(rest, state_conv.reshape(nb, (CONV_W - 1) * CONV_CH), dt, conv_w, conv_b, a_x, emat)

    to_cols = lambda a: a.reshape(nblk, bb, D_INNER).transpose(0, 2, 1)
    blk3 = lambda d1, d2: pl.BlockSpec((None, d1, d2), lambda i: (i, 0, 0))
    hspec = pl.BlockSpec((bb, D_INNER, D_STATE), lambda i: (i, 0, 0))
    gspec = pl.BlockSpec((bb, SSD_GROUPS, D_STATE), lambda i: (i, 0, 0))
    h_new, y = pl.pallas_call(
        functools.partial(_state_step_body, bb=bb),
        out_shape=(jax.ShapeDtypeStruct((nb, D_INNER, D_STATE), F32),
                   jax.ShapeDtypeStruct((nblk, bb, D_INNER), F32)),
        grid=(nblk,),
        in_specs=[hspec, blk3(D_INNER, bb), blk3(D_INNER, bb), gspec, gspec, blk3(bb, D_INNER),
                  pl.BlockSpec((1, D_INNER), lambda i: (0, 0))],
        out_specs=(hspec, blk3(bb, D_INNER)),
        compiler_params=_params(("parallel",)),
        name="state_step",
    )(state_ssm.reshape(nb, D_INNER, D_STATE), to_cols(xdt), to_cols(dec),
      bm.reshape(nb, SSD_GROUPS, D_STATE), cm.reshape(nb, SSD_GROUPS, D_STATE),
      xs.reshape(nblk, bb, D_INNER), d_x)
    return y.reshape(nb, D_INNER), h_new


def _merge_body(x_ref, att_ref, y_ref, z_ref, ga_ref, gb_ref, gn_ref, wa_ref, wb_ref, wo_ref, o_ref):
    att = jnp.concatenate([att_ref[c] for c in range(ATT_SLABS)], axis=1)
    out_a = jnp.dot(att.astype(BF16), wa_ref[...], preferred_element_type=F32)
    z = z_ref[...]
    y = y_ref[...] * (z * jax.nn.sigmoid(z))
    yn = _rmsnorm_rows(y, gn_ref[...]).astype(BF16)
    out_b = jnp.dot(yn, wb_ref[...], preferred_element_type=F32)
    merged = jax.nn.sigmoid(ga_ref[...]) * out_a + jax.nn.sigmoid(gb_ref[...]) * out_b
    o_ref[...] = x_ref[...] + jnp.dot(merged.astype(BF16), wo_ref[...], preferred_element_type=F32)


def _merge(x2d, att, y, rest, ssd_norm, w_att_out, w_ssd_out, w_out, tm=256):
    t = x2d.shape[0]
    tm = min(tm, t)
    row = lambda w, cb: pl.BlockSpec((tm, w), lambda i: (i, cb))
    full = lambda shape: pl.BlockSpec(shape, lambda i: (0, 0))
    return pl.pallas_call(
        _merge_body,
        out_shape=jax.ShapeDtypeStruct((t, D_MODEL), F32),
        grid=(t // tm,),
        in_specs=[row(D_MODEL, 0),
                  pl.BlockSpec((ATT_SLABS, tm, LANES), lambda i: (0, i, 0)),
                  row(D_INNER, 0),
                  row(D_INNER, REST_Z // D_INNER),
                  row(D_MODEL, REST_GATES // D_MODEL), row(D_MODEL, REST_GATES // D_MODEL + 1),
                  full((1, D_INNER)), full((ATT_W, D_MODEL)), full((D_INNER, D_MODEL)),
                  full((D_MODEL, D_MODEL))],
        out_specs=row(D_MODEL, 0),
        compiler_params=_params(("parallel",)),
        name="merge",
    )(x2d, att, y, rest, rest, rest, ssd_norm, w_att_out, w_ssd_out, w_out)


I1_BLOCK = SUBLANES
BF16_ROWS = 2 * SUBLANES
PEER_GATE_TT = 512
PEER_DENSE_TT = 512


def _gelu_tanh(x):
    return 0.5 * x * (1.0 + jnp.tanh(math.sqrt(2.0 / math.pi) * (x + 0.044715 * (x * x * x))))


def _peer_gate_body(x_ref, g_ref, wq_ref, k1_ref, k2_ref,
                    xnt_ref, r2_ref, e2_ref, n1_ref, w1_ref, s_all, v_all, rk1, *, tg):
    nch = tg // LANES
    nkb = N_KEYS // I1_BLOCK
    xn = _rmsnorm_rows(x_ref[...], g_ref[...])
    xnt = xn.T.astype(BF16)
    xnt_ref[...] = xnt
    qt = jnp.dot(wq_ref[...], xnt, preferred_element_type=F32)
    for h in range(PEER_HEADS):
        for sd, kref in enumerate((k1_ref, k2_ref)):
            r0 = (2 * h + sd) * N_KEYS
            s_all[2 * h + sd] = jnp.dot(kref[h], qt[r0:r0 + N_KEYS, :].astype(BF16),
                                        preferred_element_type=F32)

    rowid = lax.broadcasted_iota(jnp.int32, (N_KEYS, LANES), 0).astype(F32)
    row16 = lax.broadcasted_iota(jnp.int32, (PEER_TOPK, LANES), 0)
    row16f = row16.astype(F32)

    def extract(idx, carry):
        h = idx // nch
        off = pl.multiple_of((idx % nch) * LANES, LANES)
        s = [s_all[2 * h + sd, :, pl.ds(off, LANES)] for sd in range(2)]
        rank = [jnp.full((N_KEYS, LANES), float(PEER_TOPK), F32) for _ in range(2)]
        vals = [jnp.zeros((PEER_TOPK, LANES), F32) for _ in range(2)]
        for k in range(PEER_TOPK):
            for sd in range(2):
                m = jnp.max(s[sd], axis=0, keepdims=True)
                first = jnp.min(jnp.where(s[sd] == m, rowid, float(N_KEYS)), axis=0, keepdims=True)
                sel = rowid == first
                rank[sd] = jnp.where(sel, float(k), rank[sd])
                s[sd] = jnp.where(sel, -jnp.inf, s[sd])
                vals[sd] = jnp.where(row16 == k, m, vals[sd])
        rk1[h, :, pl.ds(off, LANES)] = rank[0]
        r2_ref[h, :, :, pl.ds(off, LANES)] = rank[1].reshape(N_KEYS // BF16_ROWS, BF16_ROWS, LANES).astype(BF16)
        v_all[2 * h, :, pl.ds(off, LANES)] = vals[0]
        v_all[2 * h + 1, :, pl.ds(off, LANES)] = vals[1]
        return carry

    lax.fori_loop(0, PEER_HEADS * nch, extract, 0)

    def finish(h, carry):
        for c in range(nch):
            lanes = slice(c * LANES, (c + 1) * LANES)
            v1 = v_all[2 * h, :, lanes]
            v2 = v_all[2 * h + 1, :, lanes]
            shifted = [jnp.broadcast_to(v2[b:b + 1, :], (PEER_TOPK, LANES)) for b in range(PEER_TOPK)]
            cnt = jnp.zeros((PEER_TOPK, LANES), F32)
            zsum = jnp.zeros((1, LANES), F32)
            top = v1[0:1, :] + v2[0:1, :]
            for step in range(PEER_TOPK):
                front = v1 + shifted[0]
                m = jnp.max(front, axis=0, keepdims=True)
                first = jnp.min(jnp.where(front == m, row16f, float(PEER_TOPK)), axis=0, keepdims=True)
                sel = row16f == first
                cnt = cnt + jnp.where(sel, 1.0, 0.0)
                zsum = zsum + jnp.exp(m - top)
                live = PEER_TOPK - 1 - step
                for b in range(live):
                    shifted[b] = jnp.where(sel, shifted[b + 1], shifted[b])
            rz = 1.0 / zsum
            r1 = rk1[h, :, lanes]
            s1 = s_all[2 * h, :, lanes]
            s2 = s_all[2 * h + 1, :, lanes]
            n_of = jnp.zeros((N_KEYS, LANES), F32)
            for a in range(PEER_TOPK):
                n_of = jnp.where(r1 == float(a), cnt[a:a + 1, :], n_of)
            wgt = jnp.exp(s1 - v1[0:1, :]) * rz
            e2v = jnp.exp(s2 - v2[0:1, :])
            e2_ref[h, :, :, lanes] = e2v.reshape(N_KEYS // BF16_ROWS, BF16_ROWS, LANES).astype(BF16)
            for kb in range(nkb):
                rs = slice(kb * I1_BLOCK, (kb + 1) * I1_BLOCK)
                n1_ref[kb, h, :, lanes] = n_of[rs, :]
                w1_ref[kb, h, :, lanes] = wgt[rs, :]
        return carry

    lax.fori_loop(0, PEER_HEADS, finish, 0)


def _peer_dense_body(x_ref, xnt_ref, r2_ref, e2_ref, n1_ref, w1_ref, u_ref, vt_ref, o_ref,
                     acct, act, hbuf, *, tt):
    j = pl.program_id(1)
    nch = tt // LANES
    half = I1_BLOCK // 2 * N_KEYS
    ktiles = N_KEYS // BF16_ROWS

    @pl.when(j == 0)
    def _():
        acct[...] = jnp.zeros_like(acct)

    xnt = xnt_ref[...]
    for hf in range(2):
        act[hf * half:(hf + 1) * half, :] = jnp.dot(u_ref[hf * half:(hf + 1) * half, :], xnt,
                                                    preferred_element_type=F32)
    for hf in range(2):
        for ii in range(hf * I1_BLOCK // 2, (hf + 1) * I1_BLOCK // 2):
            for c in range(nch):
                lanes = slice(c * LANES, (c + 1) * LANES)
                gate = None
                for h in range(PEER_HEADS):
                    n16 = jnp.broadcast_to(n1_ref[h, ii:ii + 1, lanes], (BF16_ROWS, LANES)).astype(BF16)
                    w16 = jnp.broadcast_to(w1_ref[h, ii:ii + 1, lanes], (BF16_ROWS, LANES)).astype(BF16)
                    prod = e2_ref[h, :, :, lanes] * w16[None]
                    term = jnp.where(r2_ref[h, :, :, lanes] < n16[None], prod, jnp.zeros_like(prod))
                    gate = term if gate is None else gate + term
                a = act[ii * N_KEYS:(ii + 1) * N_KEYS, lanes]
                hv = gate.astype(F32).reshape(N_KEYS, LANES) * _gelu_tanh(a)
                hbuf[ii * N_KEYS:(ii + 1) * N_KEYS, lanes] = hv.astype(BF16)
        acct[...] += jnp.dot(vt_ref[:, hf * half:(hf + 1) * half], hbuf[hf * half:(hf + 1) * half, :],
                             preferred_element_type=F32)

    @pl.when(j == pl.num_programs(1) - 1)
    def _():
        o_ref[...] = x_ref[...] + acct[...].T


def _peer(x2d, norm_ffn, wq_t, keys1, keys2, u_bf, vt_bf):
    t = x2d.shape[0]
    tg = min(PEER_GATE_TT, t)
    tt = min(PEER_DENSE_TT, t)
    nkb = N_KEYS // I1_BLOCK
    eb = I1_BLOCK * N_KEYS
    ktiles = N_KEYS // BF16_ROWS
    full1 = lambda shape: pl.BlockSpec(shape, lambda i: (0,) * len(shape))
    tab_shape = jax.ShapeDtypeStruct((PEER_HEADS, ktiles, BF16_ROWS, t), BF16)
    row_shape = jax.ShapeDtypeStruct((nkb, PEER_HEADS, I1_BLOCK, t), F32)
    tab_spec1 = pl.BlockSpec((PEER_HEADS, ktiles, BF16_ROWS, tg), lambda i: (0, 0, 0, i))
    row_spec1 = pl.BlockSpec((nkb, PEER_HEADS, I1_BLOCK, tg), lambda i: (0, 0, 0, i))
    xnt, r2, e2, n1, w1 = pl.pallas_call(
        functools.partial(_peer_gate_body, tg=tg),
        out_shape=(jax.ShapeDtypeStruct((D_MODEL, t), BF16), tab_shape, tab_shape, row_shape, row_shape),
        grid=(t // tg,),
        in_specs=[pl.BlockSpec((tg, D_MODEL), lambda i: (i, 0)),
                  full1((1, D_MODEL)), full1((2 * PEER_HEADS * N_KEYS, D_MODEL)),
                  full1((PEER_HEADS, N_KEYS, N_KEYS)), full1((PEER_HEADS, N_KEYS, N_KEYS))],
        out_specs=(pl.BlockSpec((D_MODEL, tg), lambda i: (0, i)), tab_spec1, tab_spec1, row_spec1, row_spec1),
        scratch_shapes=[pltpu.VMEM((2 * PEER_HEADS, N_KEYS, tg), F32),
                        pltpu.VMEM((2 * PEER_HEADS, PEER_TOPK, tg), F32),
                        pltpu.VMEM((PEER_HEADS, N_KEYS, tg), F32)],
        compiler_params=_params(("parallel",)),
        name="peer_gate",
    )(x2d, norm_ffn, wq_t, keys1, keys2)

    tab_spec = pl.BlockSpec((PEER_HEADS, ktiles, BF16_ROWS, tt), lambda i, j: (0, 0, 0, i))
    row_spec = pl.BlockSpec((None, PEER_HEADS, I1_BLOCK, tt), lambda i, j: (j, 0, 0, i))
    return pl.pallas_call(
        functools.partial(_peer_dense_body, tt=tt),
        out_shape=jax.ShapeDtypeStruct((t, D_MODEL), F32),
        grid=(t // tt, nkb),
        in_specs=[pl.BlockSpec((tt, D_MODEL), lambda i, j: (i, 0)),
                  pl.BlockSpec((D_MODEL, tt), lambda i, j: (0, i)),
                  tab_spec, tab_spec, row_spec, row_spec,
                  pl.BlockSpec((eb, D_MODEL), lambda i, j: (j, 0)),
                  pl.BlockSpec((D_MODEL, eb), lambda i, j: (0, j))],
        out_specs=pl.BlockSpec((tt, D_MODEL), lambda i, j: (i, 0)),
        scratch_shapes=[pltpu.VMEM((D_MODEL, tt), F32),
                        pltpu.VMEM((eb, tt), F32),
                        pltpu.VMEM((eb, tt), BF16)],
        compiler_params=_params(("parallel", "arbitrary")),
        name="peer_dense",
    )(x2d, xnt, r2, e2, n1, w1, u_bf, vt_bf)


def _ple_body(x_ref, p_ref, gp_ref, wg_ref, wp_ref, gf_ref, o_ref):
    x = x_ref[...]
    xn = _rmsnorm_rows(x, gp_ref[...]).astype(BF16)
    pg = jax.nn.sigmoid(jnp.dot(xn, wg_ref[...], preferred_element_type=F32))
    x3 = x + pg * jnp.dot(p_ref[...].astype(BF16), wp_ref[...], preferred_element_type=F32)
    o_ref[...] = _rmsnorm_rows(x3, gf_ref[...])


def _ple_final(x2d, p2d, norm_ple, w_gate, w_proj, norm_final, tm=512):
    t = x2d.shape[0]
    tm = min(tm, t)
    full = lambda shape: pl.BlockSpec(shape, lambda i: (0, 0))
    return pl.pallas_call(
        _ple_body,
        out_shape=jax.ShapeDtypeStruct((t, D_MODEL), F32),
        grid=(t // tm,),
        in_specs=[pl.BlockSpec((tm, D_MODEL), lambda i: (i, 0)),
                  pl.BlockSpec((tm, PLE_DIM), lambda i: (i, 0)),
                  full((1, D_MODEL)), full((D_MODEL, D_MODEL)), full((PLE_DIM, D_MODEL)),
                  full((1, D_MODEL))],
        out_specs=pl.BlockSpec((tm, D_MODEL), lambda i: (i, 0)),
        compiler_params=_params(("parallel",)),
        name="ple_final",
    )(x2d, p2d, norm_ple, w_gate, w_proj, norm_final)


def _prep_weights(norm_mix, w_in, conv_w, conv_b, dt_bias, a_log, d_skip, ssd_norm, w_att_out,
                  w_ssd_out, w_out, norm_ffn, w_peer_q, peer_keys1, peer_keys2, peer_u, peer_v,
                  norm_ple, w_ple_gate, w_ple_proj, norm_final):
    w = w_in[0]
    c = [0, Q_COLS, 2 * Q_COLS, 3 * Q_COLS, 3 * Q_COLS + D_INNER, 3 * Q_COLS + D_INNER + CONV_CH,
         3 * Q_COLS + D_INNER + CONV_CH + SSD_HEADS]
    wq, wk, wv, wz, wxbc, wdt, wgt = (w[:, c[0]:c[1]], w[:, c[1]:c[2]], w[:, c[2]:c[3]], w[:, c[3]:c[4]],
                                      w[:, c[4]:c[5]], w[:, c[5]:c[6]], w[:, c[6]:])
    row = lambda v: v.reshape(1, -1).astype(F32)
    p = {}
    p["norm_mix"] = row(norm_mix[0])
    p["w_qkv"] = jnp.concatenate([wq * (HEAD_DIM ** -0.5), wk, wv], axis=1).astype(BF16)
    p["w_rest"] = jnp.concatenate([wxbc, wz, wgt], axis=1).astype(BF16)
    p["w_dt"] = jnp.pad(wdt, ((0, 0), (0, LANES - SSD_HEADS))).astype(BF16)
    p["dt_bias"] = jnp.pad(row(dt_bias[0]), ((0, 0), (0, LANES - SSD_HEADS)))
    p["conv_w"] = conv_w[0].astype(F32)
    p["conv_b"] = row(conv_b[0])
    p["a_row"], p["a_x"], p["d_x"], p["emat"], p["ltri"] = _ssd_consts(a_log[0], d_skip[0])
    p["ssd_norm"] = row(ssd_norm[0])
    p["w_att_out"] = w_att_out[0].astype(BF16)
    p["w_ssd_out"] = w_ssd_out[0].astype(BF16)
    p["w_out"] = w_out[0].astype(BF16)
    p["norm_ffn"] = row(norm_ffn[0])
    p["wq_t"] = w_peer_q[0].T.astype(BF16)
    p["keys1"] = peer_keys1[0].astype(BF16)
    p["keys2"] = peer_keys2[0].astype(BF16)
    p["u"] = peer_u[0].astype(BF16)
    p["vt"] = peer_v[0].T.astype(BF16)
    p["norm_ple"] = row(norm_ple[0])
    p["w_ple_gate"] = w_ple_gate[0].astype(BF16)
    p["w_ple_proj"] = w_ple_proj[0].astype(BF16)
    p["norm_final"] = row(norm_final)
    return p


def _tail(x2d, att, y, rest, p2d, p):
    x1 = _merge(x2d, att, y, rest, p["ssd_norm"], p["w_att_out"], p["w_ssd_out"], p["w_out"])
    x2 = _peer(x1, p["norm_ffn"], p["wq_t"], p["keys1"], p["keys2"], p["u"], p["vt"])
    return _ple_final(x2, p2d, p["norm_ple"], p["w_ple_gate"], p["w_ple_proj"], p["norm_final"])


def _heads_from_slabs(qkv, which, g, b, s, rows):
    s0 = which * (Q_COLS // LANES) + g * ATT_SLABS
    x = qkv.reshape(QKV_COLS // LANES, b, s, LANES)[s0:s0 + ATT_SLABS, :, s - rows:]
    return x.transpose(1, 2, 0, 3).reshape(b, rows, N_HEADS, HEAD_DIM)


def _kv_rows(qkv, g, b, s, rows):
    return jnp.stack([_heads_from_slabs(qkv, 1, g, b, s, rows),
                      _heads_from_slabs(qkv, 2, g, b, s, rows)], axis=2)[None]


def _prompt(x, p_in, p):
    b, s, _ = x.shape
    x2d = x.reshape(b * s, D_MODEL)
    qkv, rest, dt = _projections(x2d, jnp.arange(s, dtype=jnp.int32), p["norm_mix"], p["w_qkv"],
                                 p["w_rest"], p["w_dt"], p["dt_bias"], tm=PROJ_TM)
    att = _attn_prompt(qkv, b, s)
    y, ssm = _ssd_prompt(rest, dt, p["conv_w"], p["conv_b"], p["a_row"], p["d_x"], p["emat"], p["ltri"], b, s)
    out = _tail(x2d, att, y, rest, p_in[0].reshape(b * s, PLE_DIM), p)
    kvs = [_kv_rows(qkv, g, b, s, min(win, s)) for g, (win, _) in enumerate(ATT_GROUPS)]
    conv = rest.reshape(b, s, REST_COLS)[:, s - (CONV_W - 1):, :CONV_CH][None]
    ssm = ssm.reshape(1, b, SSD_HEADS, CHUNK, D_STATE)
    return out.reshape(b, s, D_MODEL), kvs, conv, ssm


def _sample(x, p_in, caches, state_conv, state_ssm, p):
    b, s, _ = x.shape
    assert s == 1
    x2d = x.reshape(b, D_MODEL)
    pos = jnp.full((b,), PAST_LEN, dtype=jnp.int32)
    qkv, rest, dt = _projections(x2d, pos, p["norm_mix"], p["w_qkv"], p["w_rest"], p["w_dt"],
                                 p["dt_bias"], tm=b)
    rows = lambda which: jnp.stack([_heads_from_slabs(qkv, which, g, b, 1, 1).reshape(b, ATT_W)
                                    for g in range(N_GROUPS)], axis=1)
    q3, k3, v3 = rows(0), rows(1), rows(2)
    att = _attn_step(q3, k3, v3, [c[0] for c in caches])
    att = att.reshape(b, ATT_SLABS, LANES).transpose(1, 0, 2)
    xbc = rest[:, :CONV_CH]
    y, ssm = _ssd_step(xbc, dt, state_conv[0], state_ssm[0],
                       p["conv_w"], p["conv_b"], p["a_x"], p["d_x"], p["emat"])
    out = _tail(x2d, att, y, rest, p_in[0].reshape(b, PLE_DIM), p)
    hd = lambda a, g: a[:, g].reshape(b, N_HEADS, HEAD_DIM)
    kvs = [jnp.stack([hd(k3, g), hd(v3, g)], axis=1)[None, :, None] for g in range(N_GROUPS)]
    conv = jnp.concatenate([state_conv[0][:, 1:], xbc[:, None, :]], axis=1)[None]
    ssm = ssm.reshape(1, b, SSD_HEADS, CHUNK, D_STATE)
    return out.reshape(b, 1, D_MODEL), kvs, conv, ssm


def kernel(x_prompt, x_sample, cache_kv_w128, cache_kv_w512, cache_kv_w2048, state_conv, state_ssm, p_prompt, p_sample, norm_mix, w_in, conv_w, conv_b, dt_bias, a_log, d_skip, ssd_norm, w_att_out, w_ssd_out, w_out, norm_ffn, w_peer_q, peer_keys1, peer_keys2, peer_u, peer_v, norm_ple, w_ple_gate, w_ple_proj, norm_final):
    p = _prep_weights(norm_mix, w_in, conv_w, conv_b, dt_bias, a_log, d_skip, ssd_norm, w_att_out,
                      w_ssd_out, w_out, norm_ffn, w_peer_q, peer_keys1, peer_keys2, peer_u, peer_v,
                      norm_ple, w_ple_gate, w_ple_proj, norm_final)
    y_p, kv_p, conv_p, ssm_p = _prompt(x_prompt, p_prompt, p)
    y_s, kv_s, conv_s, ssm_s = _sample(x_sample, p_sample, (cache_kv_w128, cache_kv_w512, cache_kv_w2048),
                                       state_conv, state_ssm, p)
    return (y_p, y_s, kv_p[0], kv_p[1], kv_p[2], conv_p, ssm_p, kv_s[0], kv_s[1], kv_s[2], conv_s, ssm_s)
```

```python
import functools
import math

import jax
import jax.numpy as jnp
from jax import lax
from jax.experimental import pallas as pl
from jax.experimental.pallas import tpu as pltpu

F32 = jnp.float32
BF16 = jnp.bfloat16

LANES = 128
SUBLANES = 8
VMEM_BYTES_V7X = 64 * 1024 * 1024
VMEM_LIMIT = VMEM_BYTES_V7X * 3 // 4

D_MODEL = 1024
ATT_GROUPS = ((128, 1), (512, 4), (2048, 16))
N_GROUPS = len(ATT_GROUPS)
N_HEADS = 8
HEAD_DIM = 64
ATT_W = N_HEADS * HEAD_DIM
ATT_SLABS = ATT_W // LANES
Q_COLS = N_GROUPS * ATT_W
QKV_COLS = 3 * Q_COLS
ROPE_THETA = 10000.0
D_INNER = 2048
SSD_HEADS = 32
SSD_GROUPS = 8
SSD_GW = D_INNER // SSD_GROUPS
D_STATE = 128
CONV_W = 4
CONV_CH = D_INNER + 2 * SSD_GROUPS * D_STATE
CHUNK = 64
N_KEYS = 128
N_EXPERTS = N_KEYS * N_KEYS
PEER_HEADS = 8
PEER_TOPK = 16
PLE_DIM = 256
EPS = 1e-6
PAST_LEN = 8192
NEG = -1e30

REST_Z = CONV_CH
REST_GATES = REST_Z + D_INNER
REST_COLS = REST_GATES + 2 * D_MODEL
QKV_TN = 768
REST_TN = 1024
PROJ_TM = 1024
ATT_ROWS = 2048
ATT_ILP = 8
SSD_STEP_CHUNKS = 4


def _params(sem):
    return pltpu.CompilerParams(dimension_semantics=sem, vmem_limit_bytes=VMEM_LIMIT)


def _rmsnorm_rows(x, g):
    return x * lax.rsqrt(jnp.mean(x * x, axis=-1, keepdims=True) + EPS) * g


def _nt(a, b):
    return lax.dot_general(a, b, (((1,), (1,)), ((), ())), preferred_element_type=F32)


def _proj_qkv_body(x_ref, g_ref, w_ref, cos_ref, sin_ref, o_ref, xn_ref, *, rope_tiles):
    j = pl.program_id(1)

    @pl.when(j == 0)
    def _():
        xn_ref[...] = _rmsnorm_rows(x_ref[...], g_ref[...]).astype(BF16)

    acc = jnp.dot(xn_ref[...], w_ref[...], preferred_element_type=F32)
    n_slab = acc.shape[1] // LANES

    @pl.when(j < rope_tiles)
    def _():
        cos = cos_ref[...]
        sin = sin_ref[...]
        lane = lax.broadcasted_iota(jnp.int32, cos.shape, 1)
        first_half = (lane & (HEAD_DIM - 1)) < HEAD_DIM // 2
        for c in range(n_slab):
            a = acc[:, c * LANES:(c + 1) * LANES]
            partner = jnp.where(first_half, pltpu.roll(a, LANES - HEAD_DIM // 2, 1),
                                pltpu.roll(a, HEAD_DIM // 2, 1))
            o_ref[c] = a * cos + partner * sin

    @pl.when(j >= rope_tiles)
    def _():
        for c in range(n_slab):
            o_ref[c] = acc[:, c * LANES:(c + 1) * LANES]


def _proj_rest_body(x_ref, g_ref, w_ref, wdt_ref, bias_ref, o_ref, dt_ref, xn_ref):
    @pl.when(pl.program_id(1) == 0)
    def _():
        xn = _rmsnorm_rows(x_ref[...], g_ref[...]).astype(BF16)
        xn_ref[...] = xn
        v = jnp.dot(xn, wdt_ref[...], preferred_element_type=F32) + bias_ref[...]
        dt_ref[...] = jnp.maximum(v, 0.0) + jnp.log1p(jnp.exp(-jnp.abs(v)))

    o_ref[...] = jnp.dot(xn_ref[...], w_ref[...], preferred_element_type=F32)


def _rope_tables(pos):
    half = HEAD_DIM // 2
    inv = jnp.exp(jnp.arange(half, dtype=F32) * (-2.0 * math.log(ROPE_THETA) / HEAD_DIM))
    ang = pos.astype(F32)[:, None] * inv[None, :]
    cos, sin = jnp.cos(ang), jnp.sin(ang)
    cos_t = jnp.concatenate([cos, cos, cos, cos], axis=1)
    sin_t = jnp.concatenate([-sin, sin, -sin, sin], axis=1)
    return cos_t, sin_t


def _projections(x2d, pos, norm_mix, w_qkv, w_rest, w_dt, dt_bias_row, tm):
    t, k = x2d.shape
    tm = min(tm, t)
    cos_t, sin_t = _rope_tables(pos)
    n_pos_blocks = pos.shape[0] // tm
    x_spec = pl.BlockSpec((tm, k), lambda i, j: (i, 0))
    g_spec = pl.BlockSpec((1, k), lambda i, j: (0, 0))
    tab_spec = pl.BlockSpec((tm, LANES), lambda i, j: (i % n_pos_blocks, 0))
    slabs = QKV_TN // LANES
    qkv = pl.pallas_call(
        functools.partial(_proj_qkv_body, rope_tiles=2 * Q_COLS // QKV_TN),
        out_shape=jax.ShapeDtypeStruct((QKV_COLS // LANES, t, LANES), F32),
        grid=(t // tm, QKV_COLS // QKV_TN),
        in_specs=[x_spec, g_spec, pl.BlockSpec((k, QKV_TN), lambda i, j: (0, j)), tab_spec, tab_spec],
        out_specs=pl.BlockSpec((slabs, tm, LANES), lambda i, j: (j, i, 0)),
        scratch_shapes=[pltpu.VMEM((tm, k), BF16)],
        compiler_params=_params(("parallel", "arbitrary")),
        name="proj_qkv",
    )(x2d, norm_mix, w_qkv, cos_t, sin_t)
    rest, dt = pl.pallas_call(
        _proj_rest_body,
        out_shape=(jax.ShapeDtypeStruct((t, REST_COLS), F32), jax.ShapeDtypeStruct((t, LANES), F32)),
        grid=(t // tm, REST_COLS // REST_TN),
        in_specs=[x_spec, g_spec, pl.BlockSpec((k, REST_TN), lambda i, j: (0, j)),
                  pl.BlockSpec((k, LANES), lambda i, j: (0, 0)), pl.BlockSpec((1, LANES), lambda i, j: (0, 0))],
        out_specs=(pl.BlockSpec((tm, REST_TN), lambda i, j: (i, j)),
                   pl.BlockSpec((tm, LANES), lambda i, j: (i, 0))),
        scratch_shapes=[pltpu.VMEM((tm, k), BF16)],
        compiler_params=_params(("parallel", "arbitrary")),
        name="proj_rest",
    )(x2d, norm_mix, w_rest, w_dt, dt_bias_row)
    return qkv, rest, dt


def _attn_prompt_body(*refs, dil, has_prev_group, emit_lse):
    q_ref, kc_ref, kp_ref, vc_ref, vp_ref = refs[:5]
    pos = 5
    if has_prev_group:
        op_ref, lp_ref = refs[pos:pos + 2]
        pos += 2
    o_ref = refs[pos]
    pos += 1
    if emit_lse:
        l_ref = refs[pos]
        pos += 1
    kbuf, vbuf = refs[pos:pos + 2]

    i = pl.program_id(1)
    rows = q_ref.shape[0]
    span = LANES * dil
    kbuf[0:span, :] = kp_ref[rows - span:rows, :]
    kbuf[span:span + rows, :] = kc_ref[...]
    vbuf[0:span, :] = vp_ref[rows - span:rows, :]
    vbuf[span:span + rows, :] = vc_ref[...]

    krow = lax.broadcasted_iota(jnp.int32, (2 * LANES, 2 * LANES), 0)
    qidx = lax.broadcasted_iota(jnp.int32, (2 * LANES, 2 * LANES), 1) & (LANES - 1)
    own_ok = jnp.logical_and(krow >= LANES, krow - LANES <= qidx)
    prev_ok = jnp.logical_and(krow < LANES, krow >= qidx)
    low_half = lax.broadcasted_iota(jnp.int32, (LANES, LANES), 1) < HEAD_DIM

    def rd(ref, start):
        if dil == 1:
            return ref[pl.ds(start, LANES), :]
        return ref[pl.ds(start, LANES, stride=dil), :]

    def sub_blocks(p, carry):
        us = [p * ATT_ILP + k for k in range(ATT_ILP)]
        sps = [u // dil for u in us]
        starts = [sp * span + (u % dil) for u, sp in zip(us, sps)]
        scs, vsts = [], []
        for sp, start in zip(sps, starts):
            q2 = rd(q_ref, start)
            qs = jnp.concatenate([jnp.where(low_half, q2, 0.0), jnp.where(low_half, 0.0, q2)], axis=0).astype(BF16)
            ks = jnp.concatenate([rd(kbuf, start), rd(kbuf, span + start)], axis=0).astype(BF16)
            mask = jnp.logical_or(own_ok, jnp.logical_and(prev_ok, jnp.logical_or(i > 0, sp > 0)))
            scs.append(jnp.where(mask, _nt(ks, qs), NEG))
        for start in starts:
            vsts.append(jnp.concatenate([rd(vbuf, start), rd(vbuf, span + start)], axis=0).T.astype(BF16))
        es, dens, lses = [], [], []
        for sc in scs:
            m = jnp.max(sc, axis=0, keepdims=True)
            e = jnp.exp(sc - m)
            den = jnp.sum(e, axis=0, keepdims=True)
            es.append(e.astype(BF16))
            dens.append(den)
            lses.append(m + jnp.log(den))
        ots = [jnp.dot(vst, e, preferred_element_type=F32) for vst, e in zip(vsts, es)]
        for start, ot, den, lse in zip(starts, ots, dens, lses):
            ot = jnp.concatenate([ot[:HEAD_DIM, :LANES] / den[:, :LANES],
                                  ot[HEAD_DIM:, LANES:] / den[:, LANES:]], axis=0)
            lt = jnp.concatenate([jnp.broadcast_to(lse[:, :LANES], (HEAD_DIM, LANES)),
                                  jnp.broadcast_to(lse[:, LANES:], (HEAD_DIM, LANES))], axis=0)
            o2 = ot.T
            l2 = lt.T
            if has_prev_group:
                o_prev = rd(op_ref, start)
                l_prev = rd(lp_ref, start)
                mm = jnp.maximum(l_prev, l2)
                wp = jnp.exp(l_prev - mm)
                wc = jnp.exp(l2 - mm)
                tot = wp + wc
                o2 = (wp * o_prev + wc * o2) / tot
                l2 = mm + jnp.log(tot)
            if dil == 1:
                o_ref[pl.ds(start, LANES), :] = o2
                if emit_lse:
                    l_ref[pl.ds(start, LANES), :] = l2
            else:
                o_ref[pl.ds(start, LANES, stride=dil), :] = o2
                if emit_lse:
                    l_ref[pl.ds(start, LANES, stride=dil), :] = l2
        return carry

    lax.fori_loop(0, rows // (LANES * ATT_ILP), sub_blocks, 0)


def _attn_prompt_group(qkv, g, dil, b, s, prev):
    rows = min(ATT_ROWS, s)
    assert rows % (LANES * dil) == 0 and rows % (LANES * ATT_ILP) == 0 and s % rows == 0
    qkv4 = qkv.reshape(QKV_COLS // LANES, b, s, LANES)
    kslab = Q_COLS // LANES
    vslab = 2 * Q_COLS // LANES
    blk = (None, None, rows, LANES)
    in_specs = [
        pl.BlockSpec(blk, lambda bb, i, c: (g * ATT_SLABS + c, bb, i, 0)),
        pl.BlockSpec(blk, lambda bb, i, c: (kslab + g * ATT_SLABS + c, bb, i, 0)),
        pl.BlockSpec(blk, lambda bb, i, c: (kslab + g * ATT_SLABS + c, bb, jnp.maximum(i - 1, 0), 0)),
        pl.BlockSpec(blk, lambda bb, i, c: (vslab + g * ATT_SLABS + c, bb, i, 0)),
        pl.BlockSpec(blk, lambda bb, i, c: (vslab + g * ATT_SLABS + c, bb, jnp.maximum(i - 1, 0), 0)),
    ]
    args = [qkv4] * 5
    o_spec = pl.BlockSpec(blk, lambda bb, i, c: (c, bb, i, 0))
    if prev is not None:
        in_specs += [o_spec, o_spec]
        args += [prev[0].reshape(ATT_SLABS, b, s, LANES), prev[1].reshape(ATT_SLABS, b, s, LANES)]
    emit_lse = g < N_GROUPS - 1
    shape = jax.ShapeDtypeStruct((ATT_SLABS, b, s, LANES), F32)
    body = functools.partial(_attn_prompt_body, dil=dil, has_prev_group=prev is not None, emit_lse=emit_lse)
    out = pl.pallas_call(
        body,
        out_shape=(shape, shape) if emit_lse else shape,
        grid=(b, s // rows, ATT_SLABS),
        in_specs=in_specs,
        out_specs=(o_spec, o_spec) if emit_lse else o_spec,
        scratch_shapes=[pltpu.VMEM((LANES * dil + rows, LANES), F32),
                        pltpu.VMEM((LANES * dil + rows, LANES), F32)],
        compiler_params=_params(("parallel", "parallel", "parallel")),
        name="attn_prompt_g%d" % g,
    )(*args)
    if emit_lse:
        return out[0].reshape(ATT_SLABS, b * s, LANES), out[1].reshape(ATT_SLABS, b * s, LANES)
    return out.reshape(ATT_SLABS, b * s, LANES)


def _attn_prompt(qkv, b, s):
    prev = None
    for g, (win, dil) in enumerate(ATT_GROUPS):
        assert win // dil == LANES
        prev = _attn_prompt_group(qkv, g, dil, b, s, prev)
    return prev


def _split3(a):
    hi = a.astype(BF16)
    r1 = a - hi.astype(F32)
    mid = r1.astype(BF16)
    lo = (r1 - mid.astype(F32)).astype(BF16)
    return hi, mid, lo


def _expand_heads(a, emat):
    return sum(jnp.dot(part, emat, preferred_element_type=F32) for part in _split3(a))


def _ssd_prompt_body(xbc_ref, dt_ref, cw_ref, cb_ref, a_ref, dx_ref, e_ref, ltri_ref,
                     y_ref, st_ref, xext, h_ref, *, n_chunks):
    i = pl.program_id(1)
    lc = n_chunks * CHUNK
    halo = SUBLANES

    @pl.when(i == 0)
    def _():
        xext[0:halo, :] = jnp.zeros((halo, CONV_CH), F32)
        h_ref[...] = jnp.zeros_like(h_ref)

    xext[halo:halo + lc, :] = xbc_ref[...]

    lane_x = lax.broadcasted_iota(jnp.int32, (CHUNK, D_INNER), 1) & (CHUNK - 1)
    row_x = lax.broadcasted_iota(jnp.int32, (CHUNK, D_INNER), 0)
    diag = lane_x == row_x
    tril = row_x >= lane_x
    lane_p = lax.broadcasted_iota(jnp.int32, (CHUNK, LANES), 1)
    low_half = lane_p < CHUNK
    a_row = a_ref[...]
    dx = dx_ref[...]
    emat = e_ref[...]
    ltri = ltri_ref[...]

    for c in range(n_chunks):
        o = c * CHUNK
        conv = cb_ref[...] + xext[halo - 3 + o:halo - 3 + o + CHUNK, :] * cw_ref[0:1, :]
        conv = conv + xext[halo - 2 + o:halo - 2 + o + CHUNK, :] * cw_ref[1:2, :]
        conv = conv + xext[halo - 1 + o:halo - 1 + o + CHUNK, :] * cw_ref[2:3, :]
        conv = conv + xext[halo + o:halo + o + CHUNK, :] * cw_ref[3:4, :]
        xc = conv * jax.nn.sigmoid(conv)
        xs = xc[:, :D_INNER]
        bm = xc[:, D_INNER:D_INNER + SSD_GROUPS * D_STATE]
        cm = xc[:, D_INNER + SSD_GROUPS * D_STATE:]

        dt = dt_ref[o:o + CHUNK, :]
        dtx = _expand_heads(dt, emat)
        acs = sum(jnp.dot(ltri, part, preferred_element_type=F32) for part in _split3(dt * a_row))
        acsx = _expand_heads(acs, emat)
        last = acsx[CHUNK - 1:CHUNK, :]
        xdt = xs * dtx
        xw = (xdt * jnp.exp(last - acsx)).astype(BF16)
        eacs = jnp.exp(acsx)
        cdec = jnp.exp(last)
        rrow = jnp.sum(jnp.where(diag, acsx, 0.0), axis=0, keepdims=True)
        lm = jnp.exp(jnp.where(tril, acsx - rrow, NEG))
        xdtb = xdt.astype(BF16)
        cmb = cm.astype(BF16)

        for g in range(SSD_GROUPS):
            gs = slice(g * SSD_GW, (g + 1) * SSD_GW)
            bg = bm[:, g * D_STATE:(g + 1) * D_STATE]
            bgb = bg.astype(BF16)
            cg = cmb[:, g * D_STATE:(g + 1) * D_STATE]
            cbx = _nt(cg, jnp.concatenate([bgb] * (SSD_GW // CHUNK), axis=0))
            wg = (cbx * lm[:, gs]).astype(BF16)
            ydiag = []
            for pr in range(SSD_GW // LANES):
                xp = xdtb[:, g * SSD_GW + pr * LANES:g * SSD_GW + (pr + 1) * LANES]
                zero = jnp.zeros_like(xp)
                bd = jnp.concatenate([jnp.where(low_half, xp, zero), jnp.where(low_half, zero, xp)], axis=0)
                ydiag.append(jnp.dot(wg[:, pr * LANES:(pr + 1) * LANES], bd, preferred_element_type=F32))
            hp = h_ref[g]
            yoff = jnp.dot(cg, hp.astype(BF16), preferred_element_type=F32) * eacs[:, gs]
            st = jnp.dot(bg.T.astype(BF16), xw[:, gs], preferred_element_type=F32)
            h_ref[g] = hp * cdec[:, gs] + st
            y_ref[o:o + CHUNK, gs] = jnp.concatenate(ydiag, axis=1) + yoff + dx[:, gs] * xs[:, gs]

    xext[0:halo, :] = xext[lc:lc + halo, :]

    @pl.when(i == pl.num_programs(1) - 1)
    def _():
        for g in range(SSD_GROUPS):
            st_ref[g * SSD_GW:(g + 1) * SSD_GW, :] = h_ref[g].T


def _ssd_consts(a_log, d_skip):
    a = -jnp.exp(a_log.astype(F32))
    a_row = jnp.pad(a, (0, LANES - SSD_HEADS))[None, :]
    a_x = jnp.repeat(a, CHUNK)[None, :]
    d_x = jnp.repeat(d_skip.astype(F32), CHUNK)[None, :]
    emat = (jnp.arange(D_INNER)[None, :] // CHUNK == jnp.arange(LANES)[:, None]).astype(BF16)
    ltri = (jnp.arange(CHUNK)[:, None] >= jnp.arange(CHUNK)[None, :]).astype(BF16)
    return a_row, a_x, d_x, emat, ltri


def _ssd_prompt(rest, dt, conv_w, conv_b, a_row, d_x, emat, ltri, b, s, n_chunks=SSD_STEP_CHUNKS):
    lc = n_chunks * CHUNK
    rest_v = rest.reshape(b, s, REST_COLS)
    dt_v = dt.reshape(b, s, LANES)
    const = lambda shape: pl.BlockSpec(shape, lambda bb, i: (0,) * len(shape))
    body = functools.partial(_ssd_prompt_body, n_chunks=n_chunks)
    y, st = pl.pallas_call(
        body,
        out_shape=(jax.ShapeDtypeStruct((b, s, D_INNER), F32),
                   jax.ShapeDtypeStruct((b, D_INNER, D_STATE), F32)),
        grid=(b, s // lc),
        in_specs=[pl.BlockSpec((None, lc, CONV_CH), lambda bb, i: (bb, i, 0)),
                  pl.BlockSpec((None, lc, LANES), lambda bb, i: (bb, i, 0)),
                  const((CONV_W, CONV_CH)), const((1, CONV_CH)), const((1, LANES)),
                  const((1, D_INNER)), const((LANES, D_INNER)), const((CHUNK, CHUNK))],
        out_specs=(pl.BlockSpec((None, lc, D_INNER), lambda bb, i: (bb, i, 0)),
                   pl.BlockSpec((None, D_INNER, D_STATE), lambda bb, i: (bb, 0, 0))),
        scratch_shapes=[pltpu.VMEM((lc + SUBLANES, CONV_CH), F32),
                        pltpu.VMEM((SSD_GROUPS, D_STATE, SSD_GW), F32)],
        compiler_params=_params(("parallel", "arbitrary")),
        name="ssd_prompt",
    )(rest_v, dt_v, conv_w, conv_b, a_row, d_x, emat, ltri)
    return y.reshape(b * s, D_INNER), st


def _attn_step_body(q_ref, kn_ref, vn_ref, c0_ref, c1_ref, c2_ref, o_ref):
    caches = (c0_ref, c1_ref, c2_ref)
    hrow = lax.broadcasted_iota(jnp.int32, (N_HEADS, ATT_W), 0)
    hlane = lax.broadcasted_iota(jnp.int32, (N_HEADS, ATT_W), 1) // HEAD_DIM
    hmask = hrow == hlane
    s_list, sn_list = [], []
    for g, (win, dil) in enumerate(ATT_GROUPS):
        qbd = jnp.where(hmask, jnp.broadcast_to(q_ref[g:g + 1, :], (N_HEADS, ATT_W)), 0.0).astype(BF16)
        s = jnp.dot(qbd, caches[g][0].astype(BF16), preferred_element_type=F32)
        if dil > 1:
            wpos = lax.broadcasted_iota(jnp.int32, s.shape, 1)
            s = jnp.where((wpos & (dil - 1)) == 0, s, NEG)
        s_list.append(s)
        sn_list.append(jnp.sum(qbd.astype(F32) * kn_ref[g:g + 1, :].astype(BF16).astype(F32),
                               axis=1, keepdims=True))
    m = sn_list[0]
    for g in range(N_GROUPS):
        m = jnp.maximum(m, jnp.maximum(jnp.max(s_list[g], axis=1, keepdims=True), sn_list[g]))
    den = jnp.zeros((N_HEADS, 1), F32)
    acc = jnp.zeros((N_HEADS, ATT_W), F32)
    for g in range(N_GROUPS):
        e = jnp.exp(s_list[g] - m)
        en = jnp.exp(sn_list[g] - m)
        den = den + jnp.sum(e, axis=1, keepdims=True) + en
        acc = acc + _nt(e.astype(BF16), caches[g][1].astype(BF16))
        acc = acc + en.astype(BF16).astype(F32) * vn_ref[g:g + 1, :].astype(BF16).astype(F32)
    o_ref[...] = jnp.sum(jnp.where(hmask, acc / den, 0.0), axis=0, keepdims=True)


def _attn_step(q, kn, vn, caches):
    nb = q.shape[0]
    views, specs = [], []
    for (win, dil), c in zip(ATT_GROUPS, caches):
        w = c.shape[1]
        assert w == win, "cache must hold exactly one window of past rows"
        views.append(c.transpose(0, 2, 3, 4, 1).reshape(nb, 2, ATT_W, w))
        specs.append(pl.BlockSpec((None, 2, ATT_W, w), lambda i: (i, 0, 0, 0)))
    qspec = pl.BlockSpec((None, N_GROUPS, ATT_W), lambda i: (i, 0, 0))
    out = pl.pallas_call(
        _attn_step_body,
        out_shape=jax.ShapeDtypeStruct((nb, 1, ATT_W), F32),
        grid=(nb,),
        in_specs=[qspec, qspec, qspec, *specs],
        out_specs=pl.BlockSpec((None, 1, ATT_W), lambda i: (i, 0, 0)),
        compiler_params=_params(("parallel",)),
        name="attn_step",
    )(q, kn, vn, *views)
    return out.reshape(nb, ATT_W)


def _conv_step_body(rest_ref, sc_ref, dt_ref, cw_ref, cb_ref, ax_ref, e_ref,
                    xs_ref, bm_ref, cm_ref, xdt_ref, dec_ref):
    conv = cb_ref[...] + sc_ref[:, 0:CONV_CH] * cw_ref[0:1, :]
    conv = conv + sc_ref[:, CONV_CH:2 * CONV_CH] * cw_ref[1:2, :]
    conv = conv + sc_ref[:, 2 * CONV_CH:3 * CONV_CH] * cw_ref[2:3, :]
    conv = conv + rest_ref[...] * cw_ref[3:4, :]
    xc = conv * jax.nn.sigmoid(conv)
    xs = xc[:, :D_INNER]
    dtx = _expand_heads(dt_ref[...], e_ref[...])
    xs_ref[...] = xs
    bm_ref[...] = xc[:, D_INNER:D_INNER + SSD_GROUPS * D_STATE]
    cm_ref[...] = xc[:, D_INNER + SSD_GROUPS * D_STATE:]
    xdt_ref[...] = xs * dtx
    dec_ref[...] = jnp.exp(dtx * ax_ref[...])


def _state_step_body(h_ref, xdt_ref, dec_ref, b_ref, c_ref, xs_ref, dx_ref, ho_ref, y_ref, *, bb):
    grow = lax.broadcasted_iota(jnp.int32, (SSD_GROUPS, D_INNER), 0)
    glane = lax.broadcasted_iota(jnp.int32, (SSD_GROUPS, D_INNER), 1) // SSD_GW
    gmask = grow == glane
    for bi in range(bb):
        h = h_ref[bi]
        bmat = b_ref[bi]
        bx = jnp.concatenate([jnp.broadcast_to(bmat[g:g + 1, :], (SSD_GW, D_STATE))
                              for g in range(SSD_GROUPS)], axis=0)
        hn = h * dec_ref[:, bi:bi + 1] + xdt_ref[:, bi:bi + 1] * bx
        ho_ref[bi] = hn
        y8 = _nt(c_ref[bi].astype(BF16), hn.astype(BF16))
        y = jnp.sum(jnp.where(gmask, y8, 0.0), axis=0, keepdims=True)
        y_ref[bi:bi + 1, :] = y + dx_ref[...] * xs_ref[bi:bi + 1, :]


def _ssd_step(rest, dt, state_conv, state_ssm, conv_w, conv_b, a_x, d_x, emat, bb=4):
    nb = rest.shape[0]
    nblk = nb // bb
    full = lambda shape: pl.BlockSpec(shape, lambda i: (0,) * len(shape))
    xs, bm, cm, xdt, dec = pl.pallas_call(
        _conv_step_body,
        out_shape=(jax.ShapeDtypeStruct((nb, D_INNER), F32),
                   jax.ShapeDtypeStruct((nb, SSD_GROUPS * D_STATE), F32),
                   jax.ShapeDtypeStruct((nb, SSD_GROUPS * D_STATE), F32),
                   jax.ShapeDtypeStruct((nb, D_INNER), F32),
                   jax.ShapeDtypeStruct((nb, D_INNER), F32)),
        grid=(1,),
        in_specs=[pl.BlockSpec((nb, CONV_CH), lambda i: (0, 0)),
                  full((nb, (CONV_W - 1) * CONV_CH)), full((nb, LANES)),
                  full((CONV_W, CONV_CH)), full((1, CONV_CH)), full((1, D_INNER)),
                  full((LANES, D_INNER))],
        out_specs=(full((nb, D_INNER)), full((nb, SSD_GROUPS * D_STATE)),
                   full((nb, SSD_GROUPS * D_STATE)), full((nb, D_INNER)), full((nb, D_INNER))),
        compiler_params=_params(("arbitrary",)),
        name="conv_step",
    )(rest, state_conv.reshape(nb, (CONV_W - 1) * CONV_CH), dt, conv_w, conv_b, a_x, emat)

    to_cols = lambda a: a.reshape(nblk, bb, D_INNER).transpose(0, 2, 1)
    blk3 = lambda d1, d2: pl.BlockSpec((None, d1, d2), lambda i: (i, 0, 0))
    hspec = pl.BlockSpec((bb, D_INNER, D_STATE), lambda i: (i, 0, 0))
    gspec = pl.BlockSpec((bb, SSD_GROUPS, D_STATE), lambda i: (i, 0, 0))
    h_new, y = pl.pallas_call(
        functools.partial(_state_step_body, bb=bb),
        out_shape=(jax.ShapeDtypeStruct((nb, D_INNER, D_STATE), F32),
                   jax.ShapeDtypeStruct((nblk, bb, D_INNER), F32)),
        grid=(nblk,),
        in_specs=[hspec, blk3(D_INNER, bb), blk3(D_INNER, bb), gspec, gspec, blk3(bb, D_INNER),
                  pl.BlockSpec((1, D_INNER), lambda i: (0, 0))],
        out_specs=(hspec, blk3(bb, D_INNER)),
        compiler_params=_params(("parallel",)),
        name="state_step",
    )(state_ssm.reshape(nb, D_INNER, D_STATE), to_cols(xdt), to_cols(dec),
      bm.reshape(nb, SSD_GROUPS, D_STATE), cm.reshape(nb, SSD_GROUPS, D_STATE),
      xs.reshape(nblk, bb, D_INNER), d_x)
    return y.reshape(nb, D_INNER), h_new


def _merge_body(x_ref, att_ref, y_ref, z_ref, ga_ref, gb_ref, gn_ref, wa_ref, wb_ref, wo_ref, o_ref):
    att = jnp.concatenate([att_ref[c] for c in range(ATT_SLABS)], axis=1)
    out_a = jnp.dot(att.astype(BF16), wa_ref[...], preferred_element_type=F32)
    z = z_ref[...]
    y = y_ref[...] * (z * jax.nn.sigmoid(z))
    yn = _rmsnorm_rows(y, gn_ref[...]).astype(BF16)
    out_b = jnp.dot(yn, wb_ref[...], preferred_element_type=F32)
    merged = jax.nn.sigmoid(ga_ref[...]) * out_a + jax.nn.sigmoid(gb_ref[...]) * out_b
    o_ref[...] = x_ref[...] + jnp.dot(merged.astype(BF16), wo_ref[...], preferred_element_type=F32)


def _merge(x2d, att, y, rest, ssd_norm, w_att_out, w_ssd_out, w_out, tm=256):
    t = x2d.shape[0]
    tm = min(tm, t)
    row = lambda w, cb: pl.BlockSpec((tm, w), lambda i: (i, cb))
    full = lambda shape: pl.BlockSpec(shape, lambda i: (0, 0))
    return pl.pallas_call(
        _merge_body,
        out_shape=jax.ShapeDtypeStruct((t, D_MODEL), F32),
        grid=(t // tm,),
        in_specs=[row(D_MODEL, 0),
                  pl.BlockSpec((ATT_SLABS, tm, LANES), lambda i: (0, i, 0)),
                  row(D_INNER, 0),
                  row(D_INNER, REST_Z // D_INNER),
                  row(D_MODEL, REST_GATES // D_MODEL), row(D_MODEL, REST_GATES // D_MODEL + 1),
                  full((1, D_INNER)), full((ATT_W, D_MODEL)), full((D_INNER, D_MODEL)),
                  full((D_MODEL, D_MODEL))],
        out_specs=row(D_MODEL, 0),
        compiler_params=_params(("parallel",)),
        name="merge",
    )(x2d, att, y, rest, rest, rest, ssd_norm, w_att_out, w_ssd_out, w_out)


I1_BLOCK = SUBLANES
BF16_ROWS = 2 * SUBLANES
PEER_GATE_TT = 512
PEER_DENSE_TT = 512


def _gelu_tanh(x):
    return 0.5 * x * (1.0 + jnp.tanh(math.sqrt(2.0 / math.pi) * (x + 0.044715 * (x * x * x))))


def _peer_gate_body(x_ref, g_ref, wq_ref, k1_ref, k2_ref,
                    xnt_ref, r2_ref, e2_ref, n1_ref, w1_ref, s_all, v_all, rk1, *, tg):
    nch = tg // LANES
    nkb = N_KEYS // I1_BLOCK
    xn = _rmsnorm_rows(x_ref[...], g_ref[...])
    xnt = xn.T.astype(BF16)
    xnt_ref[...] = xnt
    qt = jnp.dot(wq_ref[...], xnt, preferred_element_type=F32)
    for h in range(PEER_HEADS):
        for sd, kref in enumerate((k1_ref, k2_ref)):
            r0 = (2 * h + sd) * N_KEYS
            s_all[2 * h + sd] = jnp.dot(kref[h], qt[r0:r0 + N_KEYS, :].astype(BF16),
                                        preferred_element_type=F32)

    rowid = lax.broadcasted_iota(jnp.int32, (N_KEYS, LANES), 0).astype(F32)
    row16 = lax.broadcasted_iota(jnp.int32, (PEER_TOPK, LANES), 0)
    row16f = row16.astype(F32)

    def make_extract(exact):
        def extract(idx, ties):
            h = idx // nch
            off = pl.multiple_of((idx % nch) * LANES, LANES)
            s = [s_all[2 * h + sd, :, pl.ds(off, LANES)] for sd in range(2)]
            rank = [jnp.full((N_KEYS, LANES), float(PEER_TOPK), F32) for _ in range(2)]
            vals = [jnp.zeros((PEER_TOPK, LANES), F32) for _ in range(2)]
            for k in range(PEER_TOPK):
                for sd in range(2):
                    m = jnp.max(s[sd], axis=0, keepdims=True)
                    sel = s[sd] == m
                    if exact:
                        first = jnp.min(jnp.where(sel, rowid, float(N_KEYS)), axis=0, keepdims=True)
                        sel = rowid == first
                    rank[sd] = jnp.where(sel, float(k), rank[sd])
                    s[sd] = jnp.where(sel, -jnp.inf, s[sd])
                    vals[sd] = jnp.where(row16 == k, m, vals[sd])
            rk1[h, :, pl.ds(off, LANES)] = rank[0]
            r2_ref[h, :, :, pl.ds(off, LANES)] = rank[1].reshape(N_KEYS // BF16_ROWS, BF16_ROWS, LANES).astype(BF16)
            v_all[2 * h, :, pl.ds(off, LANES)] = vals[0]
            v_all[2 * h + 1, :, pl.ds(off, LANES)] = vals[1]
            if not exact:
                for sd in range(2):
                    taken = jnp.sum(jnp.where(rank[sd] < float(PEER_TOPK), 1.0, 0.0), axis=0, keepdims=True)
                    ties = jnp.maximum(ties, jnp.where(taken != float(PEER_TOPK), 1.0, 0.0))
            return ties
        return extract

    no_ties = jnp.zeros((1, LANES), F32)
    ties = lax.fori_loop(0, PEER_HEADS * nch, make_extract(False), no_ties)

    @pl.when(jnp.max(ties) > 0.0)
    def _():
        lax.fori_loop(0, PEER_HEADS * nch, make_extract(True), no_ties)

    def finish(h, carry):
        for c in range(nch):
            lanes = slice(c * LANES, (c + 1) * LANES)
            v1 = v_all[2 * h, :, lanes]
            v2 = v_all[2 * h + 1, :, lanes]
            shifted = [jnp.broadcast_to(v2[b:b + 1, :], (PEER_TOPK, LANES)) for b in range(PEER_TOPK)]
            cnt = jnp.zeros((PEER_TOPK, LANES), F32)
            zsum = jnp.zeros((1, LANES), F32)
            top = v1[0:1, :] + v2[0:1, :]
            for step in range(PEER_TOPK):
                front = v1 + shifted[0]
                m = jnp.max(front, axis=0, keepdims=True)
                first = jnp.min(jnp.where(front == m, row16f, float(PEER_TOPK)), axis=0, keepdims=True)
                sel = row16f == first
                cnt = cnt + jnp.where(sel, 1.0, 0.0)
                zsum = zsum + jnp.exp(m - top)
                live = PEER_TOPK - 1 - step
                for b in range(live):
                    shifted[b] = jnp.where(sel, shifted[b + 1], shifted[b])
            rz = 1.0 / zsum
            r1 = rk1[h, :, lanes]
            s1 = s_all[2 * h, :, lanes]
            s2 = s_all[2 * h + 1, :, lanes]
            n_of = jnp.zeros((N_KEYS, LANES), F32)
            for a in range(PEER_TOPK):
                n_of = jnp.where(r1 == float(a), cnt[a:a + 1, :], n_of)
            wgt = jnp.exp(s1 - v1[0:1, :]) * rz
            e2v = jnp.exp(s2 - v2[0:1, :])
            e2_ref[h, :, :, lanes] = e2v.reshape(N_KEYS // BF16_ROWS, BF16_ROWS, LANES).astype(BF16)
            for kb in range(nkb):
                rs = slice(kb * I1_BLOCK, (kb + 1) * I1_BLOCK)
                n1_ref[kb, h, :, lanes] = n_of[rs, :]
                w1_ref[kb, h, :, lanes] = wgt[rs, :]
        return carry

    lax.fori_loop(0, PEER_HEADS, finish, 0)


def _peer_dense_body(x_ref, xnt_ref, r2_ref, e2_ref, n1_ref, w1_ref, u_ref, vt_ref, o_ref,
                     acct, act, hbuf, *, tt):
    j = pl.program_id(1)
    nch = tt // LANES
    half = I1_BLOCK // 2 * N_KEYS
    ktiles = N_KEYS // BF16_ROWS

    @pl.when(j == 0)
    def _():
        acct[...] = jnp.zeros_like(acct)

    xnt = xnt_ref[...]
    for hf in range(2):
        act[hf * half:(hf + 1) * half, :] = jnp.dot(u_ref[hf * half:(hf + 1) * half, :], xnt,
                                                    preferred_element_type=F32)
    for hf in range(2):
        for ii in range(hf * I1_BLOCK // 2, (hf + 1) * I1_BLOCK // 2):
            for c in range(nch):
                lanes = slice(c * LANES, (c + 1) * LANES)
                gate = None
                for h in range(PEER_HEADS):
                    n16 = jnp.broadcast_to(n1_ref[h, ii:ii + 1, lanes], (BF16_ROWS, LANES)).astype(BF16)
                    w16 = jnp.broadcast_to(w1_ref[h, ii:ii + 1, lanes], (BF16_ROWS, LANES)).astype(BF16)
                    prod = e2_ref[h, :, :, lanes] * w16[None]
                    term = jnp.where(r2_ref[h, :, :, lanes] < n16[None], prod, jnp.zeros_like(prod))
                    gate = term if gate is None else gate + term
                a = act[ii * N_KEYS:(ii + 1) * N_KEYS, lanes]
                hv = gate.astype(F32).reshape(N_KEYS, LANES) * _gelu_tanh(a)
                hbuf[ii * N_KEYS:(ii + 1) * N_KEYS, lanes] = hv.astype(BF16)
        acct[...] += jnp.dot(vt_ref[:, hf * half:(hf + 1) * half], hbuf[hf * half:(hf + 1) * half, :],
                             preferred_element_type=F32)

    @pl.when(j == pl.num_programs(1) - 1)
    def _():
        o_ref[...] = x_ref[...] + acct[...].T


def _peer(x2d, norm_ffn, wq_t, keys1, keys2, u_bf, vt_bf):
    t = x2d.shape[0]
    tg = min(PEER_GATE_TT, t)
    tt = min(PEER_DENSE_TT, t)
    nkb = N_KEYS // I1_BLOCK
    eb = I1_BLOCK * N_KEYS
    ktiles = N_KEYS // BF16_ROWS
    full1 = lambda shape: pl.BlockSpec(shape, lambda i: (0,) * len(shape))
    tab_shape = jax.ShapeDtypeStruct((PEER_HEADS, ktiles, BF16_ROWS, t), BF16)
    row_shape = jax.ShapeDtypeStruct((nkb, PEER_HEADS, I1_BLOCK, t), F32)
    tab_spec1 = pl.BlockSpec((PEER_HEADS, ktiles, BF16_ROWS, tg), lambda i: (0, 0, 0, i))
    row_spec1 = pl.BlockSpec((nkb, PEER_HEADS, I1_BLOCK, tg), lambda i: (0, 0, 0, i))
    xnt, r2, e2, n1, w1 = pl.pallas_call(
        functools.partial(_peer_gate_body, tg=tg),
        out_shape=(jax.ShapeDtypeStruct((D_MODEL, t), BF16), tab_shape, tab_shape, row_shape, row_shape),
        grid=(t // tg,),
        in_specs=[pl.BlockSpec((tg, D_MODEL), lambda i: (i, 0)),
                  full1((1, D_MODEL)), full1((2 * PEER_HEADS * N_KEYS, D_MODEL)),
                  full1((PEER_HEADS, N_KEYS, N_KEYS)), full1((PEER_HEADS, N_KEYS, N_KEYS))],
        out_specs=(pl.BlockSpec((D_MODEL, tg), lambda i: (0, i)), tab_spec1, tab_spec1, row_spec1, row_spec1),
        scratch_shapes=[pltpu.VMEM((2 * PEER_HEADS, N_KEYS, tg), F32),
                        pltpu.VMEM((2 * PEER_HEADS, PEER_TOPK, tg), F32),
                        pltpu.VMEM((PEER_HEADS, N_KEYS, tg), F32)],
        compiler_params=_params(("parallel",)),
        name="peer_gate",
    )(x2d, norm_ffn, wq_t, keys1, keys2)

    tab_spec = pl.BlockSpec((PEER_HEADS, ktiles, BF16_ROWS, tt), lambda i, j: (0, 0, 0, i))
    row_spec = pl.BlockSpec((None, PEER_HEADS, I1_BLOCK, tt), lambda i, j: (j, 0, 0, i))
    return pl.pallas_call(
        functools.partial(_peer_dense_body, tt=tt),
        out_shape=jax.ShapeDtypeStruct((t, D_MODEL), F32),
        grid=(t // tt, nkb),
        in_specs=[pl.BlockSpec((tt, D_MODEL), lambda i, j: (i, 0)),
                  pl.BlockSpec((D_MODEL, tt), lambda i, j: (0, i)),
                  tab_spec, tab_spec, row_spec, row_spec,
                  pl.BlockSpec((eb, D_MODEL), lambda i, j: (j, 0)),
                  pl.BlockSpec((D_MODEL, eb), lambda i, j: (0, j))],
        out_specs=pl.BlockSpec((tt, D_MODEL), lambda i, j: (i, 0)),
        scratch_shapes=[pltpu.VMEM((D_MODEL, tt), F32),
                        pltpu.VMEM((eb, tt), F32),
                        pltpu.VMEM((eb, tt), BF16)],
        compiler_params=_params(("parallel", "arbitrary")),
        name="peer_dense",
    )(x2d, xnt, r2, e2, n1, w1, u_bf, vt_bf)


def _ple_body(x_ref, p_ref, gp_ref, wg_ref, wp_ref, gf_ref, o_ref):
    x = x_ref[...]
    xn = _rmsnorm_rows(x, gp_ref[...]).astype(BF16)
    pg = jax.nn.sigmoid(jnp.dot(xn, wg_ref[...], preferred_element_type=F32))
    x3 = x + pg * jnp.dot(p_ref[...].astype(BF16), wp_ref[...], preferred_element_type=F32)
    o_ref[...] = _rmsnorm_rows(x3, gf_ref[...])


def _ple_final(x2d, p2d, norm_ple, w_gate, w_proj, norm_final, tm=512):
    t = x2d.shape[0]
    tm = min(tm, t)
    full = lambda shape: pl.BlockSpec(shape, lambda i: (0, 0))
    return pl.pallas_call(
        _ple_body,
        out_shape=jax.ShapeDtypeStruct((t, D_MODEL), F32),
        grid=(t // tm,),
        in_specs=[pl.BlockSpec((tm, D_MODEL), lambda i: (i, 0)),
                  pl.BlockSpec((tm, PLE_DIM), lambda i: (i, 0)),
                  full((1, D_MODEL)), full((D_MODEL, D_MODEL)), full((PLE_DIM, D_MODEL)),
                  full((1, D_MODEL))],
        out_specs=pl.BlockSpec((tm, D_MODEL), lambda i: (i, 0)),
        compiler_params=_params(("parallel",)),
        name="ple_final",
    )(x2d, p2d, norm_ple, w_gate, w_proj, norm_final)


def _prep_weights(norm_mix, w_in, conv_w, conv_b, dt_bias, a_log, d_skip, ssd_norm, w_att_out,
                  w_ssd_out, w_out, norm_ffn, w_peer_q, peer_keys1, peer_keys2, peer_u, peer_v,
                  norm_ple, w_ple_gate, w_ple_proj, norm_final):
    w = w_in[0]
    c = [0, Q_COLS, 2 * Q_COLS, 3 * Q_COLS, 3 * Q_COLS + D_INNER, 3 * Q_COLS + D_INNER + CONV_CH,
         3 * Q_COLS + D_INNER + CONV_CH + SSD_HEADS]
    wq, wk, wv, wz, wxbc, wdt, wgt = (w[:, c[0]:c[1]], w[:, c[1]:c[2]], w[:, c[2]:c[3]], w[:, c[3]:c[4]],
                                      w[:, c[4]:c[5]], w[:, c[5]:c[6]], w[:, c[6]:])
    row = lambda v: v.reshape(1, -1).astype(F32)
    p = {}
    p["norm_mix"] = row(norm_mix[0])
    p["w_qkv"] = jnp.concatenate([wq * (HEAD_DIM ** -0.5), wk, wv], axis=1).astype(BF16)
    p["w_rest"] = jnp.concatenate([wxbc, wz, wgt], axis=1).astype(BF16)
    p["w_dt"] = jnp.pad(wdt, ((0, 0), (0, LANES - SSD_HEADS))).astype(BF16)
    p["dt_bias"] = jnp.pad(row(dt_bias[0]), ((0, 0), (0, LANES - SSD_HEADS)))
    p["conv_w"] = conv_w[0].astype(F32)
    p["conv_b"] = row(conv_b[0])
    p["a_row"], p["a_x"], p["d_x"], p["emat"], p["ltri"] = _ssd_consts(a_log[0], d_skip[0])
    p["ssd_norm"] = row(ssd_norm[0])
    p["w_att_out"] = w_att_out[0].astype(BF16)
    p["w_ssd_out"] = w_ssd_out[0].astype(BF16)
    p["w_out"] = w_out[0].astype(BF16)
    p["norm_ffn"] = row(norm_ffn[0])
    p["wq_t"] = w_peer_q[0].T.astype(BF16)
    p["keys1"] = peer_keys1[0].astype(BF16)
    p["keys2"] = peer_keys2[0].astype(BF16)
    p["u"] = peer_u[0].astype(BF16)
    p["vt"] = peer_v[0].T.astype(BF16)
    p["norm_ple"] = row(norm_ple[0])
    p["w_ple_gate"] = w_ple_gate[0].astype(BF16)
    p["w_ple_proj"] = w_ple_proj[0].astype(BF16)
    p["norm_final"] = row(norm_final)
    return p


def _tail(x2d, att, y, rest, p2d, p):
    x1 = _merge(x2d, att, y, rest, p["ssd_norm"], p["w_att_out"], p["w_ssd_out"], p["w_out"])
    x2 = _peer(x1, p["norm_ffn"], p["wq_t"], p["keys1"], p["keys2"], p["u"], p["vt"])
    return _ple_final(x2, p2d, p["norm_ple"], p["w_ple_gate"], p["w_ple_proj"], p["norm_final"])


def _heads_from_slabs(qkv, which, g, b, s, rows):
    s0 = which * (Q_COLS // LANES) + g * ATT_SLABS
    x = qkv.reshape(QKV_COLS // LANES, b, s, LANES)[s0:s0 + ATT_SLABS, :, s - rows:]
    return x.transpose(1, 2, 0, 3).reshape(b, rows, N_HEADS, HEAD_DIM)


def _kv_rows(qkv, g, b, s, rows):
    return jnp.stack([_heads_from_slabs(qkv, 1, g, b, s, rows),
                      _heads_from_slabs(qkv, 2, g, b, s, rows)], axis=2)[None]


def _prompt(x, p_in, p):
    b, s, _ = x.shape
    x2d = x.reshape(b * s, D_MODEL)
    qkv, rest, dt = _projections(x2d, jnp.arange(s, dtype=jnp.int32), p["norm_mix"], p["w_qkv"],
                                 p["w_rest"], p["w_dt"], p["dt_bias"], tm=PROJ_TM)
    att = _attn_prompt(qkv, b, s)
    y, ssm = _ssd_prompt(rest, dt, p["conv_w"], p["conv_b"], p["a_row"], p["d_x"], p["emat"], p["ltri"], b, s)
    out = _tail(x2d, att, y, rest, p_in[0].reshape(b * s, PLE_DIM), p)
    kvs = [_kv_rows(qkv, g, b, s, min(win, s)) for g, (win, _) in enumerate(ATT_GROUPS)]
    conv = rest.reshape(b, s, REST_COLS)[:, s - (CONV_W - 1):, :CONV_CH][None]
    ssm = ssm.reshape(1, b, SSD_HEADS, CHUNK, D_STATE)
    return out.reshape(b, s, D_MODEL), kvs, conv, ssm


def _sample(x, p_in, caches, state_conv, state_ssm, p):
    b, s, _ = x.shape
    assert s == 1
    x2d = x.reshape(b, D_MODEL)
    pos = jnp.full((b,), PAST_LEN, dtype=jnp.int32)
    qkv, rest, dt = _projections(x2d, pos, p["norm_mix"], p["w_qkv"], p["w_rest"], p["w_dt"],
                                 p["dt_bias"], tm=b)
    rows = lambda which: jnp.stack([_heads_from_slabs(qkv, which, g, b, 1, 1).reshape(b, ATT_W)
                                    for g in range(N_GROUPS)], axis=1)
    q3, k3, v3 = rows(0), rows(1), rows(2)
    att = _attn_step(q3, k3, v3, [c[0] for c in caches])
    att = att.reshape(b, ATT_SLABS, LANES).transpose(1, 0, 2)
    xbc = rest[:, :CONV_CH]
    y, ssm = _ssd_step(xbc, dt, state_conv[0], state_ssm[0],
                       p["conv_w"], p["conv_b"], p["a_x"], p["d_x"], p["emat"])
    out = _tail(x2d, att, y, rest, p_in[0].reshape(b, PLE_DIM), p)
    hd = lambda a, g: a[:, g].reshape(b, N_HEADS, HEAD_DIM)
    kvs = [jnp.stack([hd(k3, g), hd(v3, g)], axis=1)[None, :, None] for g in range(N_GROUPS)]
    conv = jnp.concatenate([state_conv[0][:, 1:], xbc[:, None, :]], axis=1)[None]
    ssm = ssm.reshape(1, b, SSD_HEADS, CHUNK, D_STATE)
    return out.reshape(b, 1, D_MODEL), kvs, conv, ssm


def kernel(x_prompt, x_sample, cache_kv_w128, cache_kv_w512, cache_kv_w2048, state_conv, state_ssm, p_prompt, p_sample, norm_mix, w_in, conv_w, conv_b, dt_bias, a_log, d_skip, ssd_norm, w_att_out, w_ssd_out, w_out, norm_ffn, w_peer_q, peer_keys1, peer_keys2, peer_u, peer_v, norm_ple, w_ple_gate, w_ple_proj, norm_final):
    p = _prep_weights(norm_mix, w_in, conv_w, conv_b, dt_bias, a_log, d_skip, ssd_norm, w_att_out,
                      w_ssd_out, w_out, norm_ffn, w_peer_q, peer_keys1, peer_keys2, peer_u, peer_v,
                      norm_ple, w_ple_gate, w_ple_proj, norm_final)
    y_p, kv_p, conv_p, ssm_p = _prompt(x_prompt, p_prompt, p)
    y_s, kv_s, conv_s, ssm_s = _sample(x_sample, p_sample, (cache_kv_w128, cache_kv_w512, cache_kv_w2048),
                                       state_conv, state_ssm, p)
    return (y_p, y_s, kv_p[0], kv_p[1], kv_p[2], conv_p, ssm_p, kv_s[0], kv_s[1], kv_s[2], conv_s, ssm_s)
```

```python
import functools
import math

import jax
import jax.numpy as jnp
from jax import lax
from jax.experimental import pallas as pl
from jax.experimental.pallas import tpu as pltpu

F32 = jnp.float32
BF16 = jnp.bfloat16

LANES = 128
SUBLANES = 8
VMEM_BYTES_V7X = 64 * 1024 * 1024
VMEM_LIMIT = VMEM_BYTES_V7X * 3 // 4

D_MODEL = 1024
ATT_GROUPS = ((128, 1), (512, 4), (2048, 16))
N_GROUPS = len(ATT_GROUPS)
N_HEADS = 8
HEAD_DIM = 64
ATT_W = N_HEADS * HEAD_DIM
ATT_SLABS = ATT_W // LANES
Q_COLS = N_GROUPS * ATT_W
QKV_COLS = 3 * Q_COLS
ROPE_THETA = 10000.0
D_INNER = 2048
SSD_HEADS = 32
SSD_GROUPS = 8
SSD_GW = D_INNER // SSD_GROUPS
D_STATE = 128
CONV_W = 4
CONV_CH = D_INNER + 2 * SSD_GROUPS * D_STATE
CHUNK = 64
N_KEYS = 128
N_EXPERTS = N_KEYS * N_KEYS
PEER_HEADS = 8
PEER_TOPK = 16
PLE_DIM = 256
EPS = 1e-6
PAST_LEN = 8192
NEG = -1e30

REST_Z = CONV_CH
REST_GATES = REST_Z + D_INNER
REST_COLS = REST_GATES + 2 * D_MODEL
QKV_TN = 768
REST_TN = 1024
PROJ_TM = 1024
ATT_ROWS = 2048
ATT_ILP = 8
SSD_STEP_CHUNKS = 4


def _params(sem):
    return pltpu.CompilerParams(dimension_semantics=sem, vmem_limit_bytes=VMEM_LIMIT)


def _rmsnorm_rows(x, g):
    return x * lax.rsqrt(jnp.mean(x * x, axis=-1, keepdims=True) + EPS) * g


def _nt(a, b):
    return lax.dot_general(a, b, (((1,), (1,)), ((), ())), preferred_element_type=F32)


def _proj_qkv_body(x_ref, g_ref, w_ref, cos_ref, sin_ref, o_ref, xn_ref, *, rope_tiles):
    j = pl.program_id(1)

    @pl.when(j == 0)
    def _():
        xn_ref[...] = _rmsnorm_rows(x_ref[...], g_ref[...]).astype(BF16)

    acc = jnp.dot(xn_ref[...], w_ref[...], preferred_element_type=F32)
    n_slab = acc.shape[1] // LANES

    @pl.when(j < rope_tiles)
    def _():
        cos = cos_ref[...]
        sin = sin_ref[...]
        for c in range(n_slab):
            a = acc[:, c * LANES:(c + 1) * LANES]
            o_ref[c] = a * cos + pltpu.roll(a, LANES // 2, 1) * sin

    @pl.when(j >= rope_tiles)
    def _():
        for c in range(n_slab):
            o_ref[c] = acc[:, c * LANES:(c + 1) * LANES]


def _proj_rest_body(x_ref, g_ref, w_ref, wdt_ref, bias_ref, o_ref, dt_ref, xn_ref):
    @pl.when(pl.program_id(1) == 0)
    def _():
        xn = _rmsnorm_rows(x_ref[...], g_ref[...]).astype(BF16)
        xn_ref[...] = xn
        v = jnp.dot(xn, wdt_ref[...], preferred_element_type=F32) + bias_ref[...]
        dt_ref[...] = jnp.maximum(v, 0.0) + jnp.log1p(jnp.exp(-jnp.abs(v)))

    o_ref[...] = jnp.dot(xn_ref[...], w_ref[...], preferred_element_type=F32)


def _rope_tables(pos):
    half = HEAD_DIM // 2
    inv = jnp.exp(jnp.arange(half, dtype=F32) * (-2.0 * math.log(ROPE_THETA) / HEAD_DIM))
    ang = pos.astype(F32)[:, None] * inv[None, :]
    cos, sin = jnp.cos(ang), jnp.sin(ang)
    cos_t = jnp.concatenate([cos, cos, cos, cos], axis=1)
    sin_t = jnp.concatenate([-sin, -sin, sin, sin], axis=1)
    return cos_t, sin_t


def _projections(x2d, pos, norm_mix, w_qkv, w_rest, w_dt, dt_bias_row, tm):
    t, k = x2d.shape
    tm = min(tm, t)
    cos_t, sin_t = _rope_tables(pos)
    n_pos_blocks = pos.shape[0] // tm
    x_spec = pl.BlockSpec((tm, k), lambda i, j: (i, 0))
    g_spec = pl.BlockSpec((1, k), lambda i, j: (0, 0))
    tab_spec = pl.BlockSpec((tm, LANES), lambda i, j: (i % n_pos_blocks, 0))
    slabs = QKV_TN // LANES
    qkv = pl.pallas_call(
        functools.partial(_proj_qkv_body, rope_tiles=2 * Q_COLS // QKV_TN),
        out_shape=jax.ShapeDtypeStruct((QKV_COLS // LANES, t, LANES), F32),
        grid=(t // tm, QKV_COLS // QKV_TN),
        in_specs=[x_spec, g_spec, pl.BlockSpec((k, QKV_TN), lambda i, j: (0, j)), tab_spec, tab_spec],
        out_specs=pl.BlockSpec((slabs, tm, LANES), lambda i, j: (j, i, 0)),
        scratch_shapes=[pltpu.VMEM((tm, k), BF16)],
        compiler_params=_params(("parallel", "arbitrary")),
        name="proj_qkv",
    )(x2d, norm_mix, w_qkv, cos_t, sin_t)
    rest, dt = pl.pallas_call(
        _proj_rest_body,
        out_shape=(jax.ShapeDtypeStruct((t, REST_COLS), F32), jax.ShapeDtypeStruct((t, LANES), F32)),
        grid=(t // tm, REST_COLS // REST_TN),
        in_specs=[x_spec, g_spec, pl.BlockSpec((k, REST_TN), lambda i, j: (0, j)),
                  pl.BlockSpec((k, LANES), lambda i, j: (0, 0)), pl.BlockSpec((1, LANES), lambda i, j: (0, 0))],
        out_specs=(pl.BlockSpec((tm, REST_TN), lambda i, j: (i, j)),
                   pl.BlockSpec((tm, LANES), lambda i, j: (i, 0))),
        scratch_shapes=[pltpu.VMEM((tm, k), BF16)],
        compiler_params=_params(("parallel", "arbitrary")),
        name="proj_rest",
    )(x2d, norm_mix, w_rest, w_dt, dt_bias_row)
    return qkv, rest, dt


def _attn_prompt_body(*refs, dil, has_prev_group, emit_lse):
    q_ref, kc_ref, kp_ref, vc_ref, vp_ref = refs[:5]
    pos = 5
    if has_prev_group:
        op_ref, lp_ref = refs[pos:pos + 2]
        pos += 2
    o_ref = refs[pos]
    pos += 1
    if emit_lse:
        l_ref = refs[pos]
        pos += 1
    kbuf, vbuf = refs[pos:pos + 2]

    i = pl.program_id(1)
    rows = q_ref.shape[0]
    span = LANES * dil
    kbuf[0:span, :] = kp_ref[rows - span:rows, :]
    kbuf[span:span + rows, :] = kc_ref[...]
    vbuf[0:span, :] = vp_ref[rows - span:rows, :]
    vbuf[span:span + rows, :] = vc_ref[...]

    krow = lax.broadcasted_iota(jnp.int32, (2 * LANES, 2 * LANES), 0)
    qidx = lax.broadcasted_iota(jnp.int32, (2 * LANES, 2 * LANES), 1) & (LANES - 1)
    own_ok = jnp.logical_and(krow >= LANES, krow - LANES <= qidx)
    prev_ok = jnp.logical_and(krow < LANES, krow >= qidx)
    low_half = (lax.broadcasted_iota(jnp.int32, (LANES, LANES), 1) & (HEAD_DIM // 2)) == 0

    def rd(ref, start):
        if dil == 1:
            return ref[pl.ds(start, LANES), :]
        return ref[pl.ds(start, LANES, stride=dil), :]

    def sub_blocks(p, carry):
        us = [p * ATT_ILP + k for k in range(ATT_ILP)]
        sps = [u // dil for u in us]
        starts = [sp * span + (u % dil) for u, sp in zip(us, sps)]
        scs, vsts = [], []
        for sp, start in zip(sps, starts):
            q2 = rd(q_ref, start)
            qs = jnp.concatenate([jnp.where(low_half, q2, 0.0), jnp.where(low_half, 0.0, q2)], axis=0).astype(BF16)
            ks = jnp.concatenate([rd(kbuf, start), rd(kbuf, span + start)], axis=0).astype(BF16)
            mask = jnp.logical_or(own_ok, jnp.logical_and(prev_ok, jnp.logical_or(i > 0, sp > 0)))
            scs.append(jnp.where(mask, _nt(ks, qs), NEG))
        for start in starts:
            vsts.append(jnp.concatenate([rd(vbuf, start), rd(vbuf, span + start)], axis=0).T.astype(BF16))
        es, dens, lses = [], [], []
        for sc in scs:
            m = jnp.max(sc, axis=0, keepdims=True)
            e = jnp.exp(sc - m)
            den = jnp.sum(e, axis=0, keepdims=True)
            es.append(e.astype(BF16))
            dens.append(den)
            lses.append(m + jnp.log(den))
        ots = [jnp.dot(vst, e, preferred_element_type=F32) for vst, e in zip(vsts, es)]
        for start, ot, den, lse in zip(starts, ots, dens, lses):
            ot = jnp.concatenate([ot[:HEAD_DIM, :LANES] / den[:, :LANES],
                                  ot[HEAD_DIM:, LANES:] / den[:, LANES:]], axis=0)
            lt = jnp.concatenate([jnp.broadcast_to(lse[:, :LANES], (HEAD_DIM, LANES)),
                                  jnp.broadcast_to(lse[:, LANES:], (HEAD_DIM, LANES))], axis=0)
            o2 = ot.T
            l2 = lt.T
            if has_prev_group:
                o_prev = rd(op_ref, start)
                l_prev = rd(lp_ref, start)
                mm = jnp.maximum(l_prev, l2)
                wp = jnp.exp(l_prev - mm)
                wc = jnp.exp(l2 - mm)
                tot = wp + wc
                o2 = (wp * o_prev + wc * o2) / tot
                l2 = mm + jnp.log(tot)
            if dil == 1:
                o_ref[pl.ds(start, LANES), :] = o2
                if emit_lse:
                    l_ref[pl.ds(start, LANES), :] = l2
            else:
                o_ref[pl.ds(start, LANES, stride=dil), :] = o2
                if emit_lse:
                    l_ref[pl.ds(start, LANES, stride=dil), :] = l2
        return carry

    lax.fori_loop(0, rows // (LANES * ATT_ILP), sub_blocks, 0)


def _attn_prompt_group(qkv, g, dil, b, s, prev):
    rows = min(ATT_ROWS, s)
    assert rows % (LANES * dil) == 0 and rows % (LANES * ATT_ILP) == 0 and s % rows == 0
    qkv4 = qkv.reshape(QKV_COLS // LANES, b, s, LANES)
    kslab = Q_COLS // LANES
    vslab = 2 * Q_COLS // LANES
    blk = (None, None, rows, LANES)
    in_specs = [
        pl.BlockSpec(blk, lambda bb, i, c: (g * ATT_SLABS + c, bb, i, 0)),
        pl.BlockSpec(blk, lambda bb, i, c: (kslab + g * ATT_SLABS + c, bb, i, 0)),
        pl.BlockSpec(blk, lambda bb, i, c: (kslab + g * ATT_SLABS + c, bb, jnp.maximum(i - 1, 0), 0)),
        pl.BlockSpec(blk, lambda bb, i, c: (vslab + g * ATT_SLABS + c, bb, i, 0)),
        pl.BlockSpec(blk, lambda bb, i, c: (vslab + g * ATT_SLABS + c, bb, jnp.maximum(i - 1, 0), 0)),
    ]
    args = [qkv4] * 5
    o_spec = pl.BlockSpec(blk, lambda bb, i, c: (c, bb, i, 0))
    if prev is not None:
        in_specs += [o_spec, o_spec]
        args += [prev[0].reshape(ATT_SLABS, b, s, LANES), prev[1].reshape(ATT_SLABS, b, s, LANES)]
    emit_lse = g < N_GROUPS - 1
    shape = jax.ShapeDtypeStruct((ATT_SLABS, b, s, LANES), F32)
    body = functools.partial(_attn_prompt_body, dil=dil, has_prev_group=prev is not None, emit_lse=emit_lse)
    out = pl.pallas_call(
        body,
        out_shape=(shape, shape) if emit_lse else shape,
        grid=(b, s // rows, ATT_SLABS),
        in_specs=in_specs,
        out_specs=(o_spec, o_spec) if emit_lse else o_spec,
        scratch_shapes=[pltpu.VMEM((LANES * dil + rows, LANES), F32),
                        pltpu.VMEM((LANES * dil + rows, LANES), F32)],
        compiler_params=_params(("parallel", "parallel", "parallel")),
        name="attn_prompt_g%d" % g,
    )(*args)
    if emit_lse:
        return out[0].reshape(ATT_SLABS, b * s, LANES), out[1].reshape(ATT_SLABS, b * s, LANES)
    return out.reshape(ATT_SLABS, b * s, LANES)


def _attn_prompt(qkv, b, s):
    prev = None
    for g, (win, dil) in enumerate(ATT_GROUPS):
        assert win // dil == LANES
        prev = _attn_prompt_group(qkv, g, dil, b, s, prev)
    return prev


def _split3(a):
    hi = a.astype(BF16)
    r1 = a - hi.astype(F32)
    mid = r1.astype(BF16)
    lo = (r1 - mid.astype(F32)).astype(BF16)
    return hi, mid, lo


def _expand_heads(a, emat):
    return sum(jnp.dot(part, emat, preferred_element_type=F32) for part in _split3(a))


def _ssd_prompt_body(xbc_ref, dt_ref, cw_ref, cb_ref, a_ref, dx_ref, e_ref, ltri_ref,
                     y_ref, st_ref, xext, h_ref, *, n_chunks):
    i = pl.program_id(1)
    lc = n_chunks * CHUNK
    halo = SUBLANES

    @pl.when(i == 0)
    def _():
        xext[0:halo, :] = jnp.zeros((halo, CONV_CH), F32)
        h_ref[...] = jnp.zeros_like(h_ref)

    xext[halo:halo + lc, :] = xbc_ref[...]

    lane_x = lax.broadcasted_iota(jnp.int32, (CHUNK, D_INNER), 1) & (CHUNK - 1)
    row_x = lax.broadcasted_iota(jnp.int32, (CHUNK, D_INNER), 0)
    diag = lane_x == row_x
    tril = row_x >= lane_x
    lane_p = lax.broadcasted_iota(jnp.int32, (CHUNK, LANES), 1)
    low_half = lane_p < CHUNK
    a_row = a_ref[...]
    dx = dx_ref[...]
    emat = e_ref[...]
    ltri = ltri_ref[...]

    for c in range(n_chunks):
        o = c * CHUNK
        conv = cb_ref[...] + xext[halo - 3 + o:halo - 3 + o + CHUNK, :] * cw_ref[0:1, :]
        conv = conv + xext[halo - 2 + o:halo - 2 + o + CHUNK, :] * cw_ref[1:2, :]
        conv = conv + xext[halo - 1 + o:halo - 1 + o + CHUNK, :] * cw_ref[2:3, :]
        conv = conv + xext[halo + o:halo + o + CHUNK, :] * cw_ref[3:4, :]
        xc = conv * jax.nn.sigmoid(conv)
        xs = xc[:, :D_INNER]
        bm = xc[:, D_INNER:D_INNER + SSD_GROUPS * D_STATE]
        cm = xc[:, D_INNER + SSD_GROUPS * D_STATE:]

        dt = dt_ref[o:o + CHUNK, :]
        dtx = _expand_heads(dt, emat)
        acs = sum(jnp.dot(ltri, part, preferred_element_type=F32) for part in _split3(dt * a_row))
        acsx = _expand_heads(acs, emat)
        last = acsx[CHUNK - 1:CHUNK, :]
        xdt = xs * dtx
        xw = (xdt * jnp.exp(last - acsx)).astype(BF16)
        eacs = jnp.exp(acsx)
        cdec = jnp.exp(last)
        rrow = jnp.sum(jnp.where(diag, acsx, 0.0), axis=0, keepdims=True)
        lm = jnp.exp(jnp.where(tril, acsx - rrow, NEG))
        xdtb = xdt.astype(BF16)
        cmb = cm.astype(BF16)

        for g in range(SSD_GROUPS):
            gs = slice(g * SSD_GW, (g + 1) * SSD_GW)
            bg = bm[:, g * D_STATE:(g + 1) * D_STATE]
            bgb = bg.astype(BF16)
            cg = cmb[:, g * D_STATE:(g + 1) * D_STATE]
            cbx = _nt(cg, jnp.concatenate([bgb] * (SSD_GW // CHUNK), axis=0))
            wg = (cbx * lm[:, gs]).astype(BF16)
            ydiag = []
            for pr in range(SSD_GW // LANES):
                xp = xdtb[:, g * SSD_GW + pr * LANES:g * SSD_GW + (pr + 1) * LANES]
                zero = jnp.zeros_like(xp)
                bd = jnp.concatenate([jnp.where(low_half, xp, zero), jnp.where(low_half, zero, xp)], axis=0)
                ydiag.append(jnp.dot(wg[:, pr * LANES:(pr + 1) * LANES], bd, preferred_element_type=F32))
            hp = h_ref[g]
            yoff = jnp.dot(cg, hp.astype(BF16), preferred_element_type=F32) * eacs[:, gs]
            st = jnp.dot(bg.T.astype(BF16), xw[:, gs], preferred_element_type=F32)
            h_ref[g] = hp * cdec[:, gs] + st
            y_ref[o:o + CHUNK, gs] = jnp.concatenate(ydiag, axis=1) + yoff + dx[:, gs] * xs[:, gs]

    xext[0:halo, :] = xext[lc:lc + halo, :]

    @pl.when(i == pl.num_programs(1) - 1)
    def _():
        for g in range(SSD_GROUPS):
            st_ref[g * SSD_GW:(g + 1) * SSD_GW, :] = h_ref[g].T


def _ssd_consts(a_log, d_skip):
    a = -jnp.exp(a_log.astype(F32))
    a_row = jnp.pad(a, (0, LANES - SSD_HEADS))[None, :]
    a_x = jnp.repeat(a, CHUNK)[None, :]
    d_x = jnp.repeat(d_skip.astype(F32), CHUNK)[None, :]
    emat = (jnp.arange(D_INNER)[None, :] // CHUNK == jnp.arange(LANES)[:, None]).astype(BF16)
    ltri = (jnp.arange(CHUNK)[:, None] >= jnp.arange(CHUNK)[None, :]).astype(BF16)
    return a_row, a_x, d_x, emat, ltri


def _ssd_prompt(rest, dt, conv_w, conv_b, a_row, d_x, emat, ltri, b, s, n_chunks=SSD_STEP_CHUNKS):
    lc = n_chunks * CHUNK
    rest_v = rest.reshape(b, s, REST_COLS)
    dt_v = dt.reshape(b, s, LANES)
    const = lambda shape: pl.BlockSpec(shape, lambda bb, i: (0,) * len(shape))
    body = functools.partial(_ssd_prompt_body, n_chunks=n_chunks)
    y, st = pl.pallas_call(
        body,
        out_shape=(jax.ShapeDtypeStruct((b, s, D_INNER), F32),
                   jax.ShapeDtypeStruct((b, D_INNER, D_STATE), F32)),
        grid=(b, s // lc),
        in_specs=[pl.BlockSpec((None, lc, CONV_CH), lambda bb, i: (bb, i, 0)),
                  pl.BlockSpec((None, lc, LANES), lambda bb, i: (bb, i, 0)),
                  const((CONV_W, CONV_CH)), const((1, CONV_CH)), const((1, LANES)),
                  const((1, D_INNER)), const((LANES, D_INNER)), const((CHUNK, CHUNK))],
        out_specs=(pl.BlockSpec((None, lc, D_INNER), lambda bb, i: (bb, i, 0)),
                   pl.BlockSpec((None, D_INNER, D_STATE), lambda bb, i: (bb, 0, 0))),
        scratch_shapes=[pltpu.VMEM((lc + SUBLANES, CONV_CH), F32),
                        pltpu.VMEM((SSD_GROUPS, D_STATE, SSD_GW), F32)],
        compiler_params=_params(("parallel", "arbitrary")),
        name="ssd_prompt",
    )(rest_v, dt_v, conv_w, conv_b, a_row, d_x, emat, ltri)
    return y.reshape(b * s, D_INNER), st


def _attn_step_body(q_ref, kn_ref, vn_ref, c0_ref, c1_ref, c2_ref, o_ref):
    caches = (c0_ref, c1_ref, c2_ref)
    hrow = lax.broadcasted_iota(jnp.int32, (N_HEADS, ATT_W), 0)
    hlane = lax.broadcasted_iota(jnp.int32, (N_HEADS, ATT_W), 1) // HEAD_DIM
    hmask = hrow == hlane
    s_list, sn_list = [], []
    for g, (win, dil) in enumerate(ATT_GROUPS):
        qbd = jnp.where(hmask, jnp.broadcast_to(q_ref[g:g + 1, :], (N_HEADS, ATT_W)), 0.0).astype(BF16)
        s = jnp.dot(qbd, caches[g][0].astype(BF16), preferred_element_type=F32)
        if dil > 1:
            wpos = lax.broadcasted_iota(jnp.int32, s.shape, 1)
            s = jnp.where((wpos & (dil - 1)) == 0, s, NEG)
        s_list.append(s)
        sn_list.append(jnp.sum(qbd.astype(F32) * kn_ref[g:g + 1, :].astype(BF16).astype(F32),
                               axis=1, keepdims=True))
    m = sn_list[0]
    for g in range(N_GROUPS):
        m = jnp.maximum(m, jnp.maximum(jnp.max(s_list[g], axis=1, keepdims=True), sn_list[g]))
    den = jnp.zeros((N_HEADS, 1), F32)
    acc = jnp.zeros((N_HEADS, ATT_W), F32)
    for g in range(N_GROUPS):
        e = jnp.exp(s_list[g] - m)
        en = jnp.exp(sn_list[g] - m)
        den = den + jnp.sum(e, axis=1, keepdims=True) + en
        acc = acc + _nt(e.astype(BF16), caches[g][1].astype(BF16))
        acc = acc + en.astype(BF16).astype(F32) * vn_ref[g:g + 1, :].astype(BF16).astype(F32)
    o_ref[...] = jnp.sum(jnp.where(hmask, acc / den, 0.0), axis=0, keepdims=True)


def _attn_step(q, kn, vn, caches):
    nb = q.shape[0]
    views, specs = [], []
    for (win, dil), c in zip(ATT_GROUPS, caches):
        w = c.shape[1]
        assert w == win, "cache must hold exactly one window of past rows"
        views.append(c.transpose(0, 2, 3, 4, 1).reshape(nb, 2, ATT_W, w))
        specs.append(pl.BlockSpec((None, 2, ATT_W, w), lambda i: (i, 0, 0, 0)))
    qspec = pl.BlockSpec((None, N_GROUPS, ATT_W), lambda i: (i, 0, 0))
    out = pl.pallas_call(
        _attn_step_body,
        out_shape=jax.ShapeDtypeStruct((nb, 1, ATT_W), F32),
        grid=(nb,),
        in_specs=[qspec, qspec, qspec, *specs],
        out_specs=pl.BlockSpec((None, 1, ATT_W), lambda i: (i, 0, 0)),
        compiler_params=_params(("parallel",)),
        name="attn_step",
    )(q, kn, vn, *views)
    return out.reshape(nb, ATT_W)


def _conv_step_body(rest_ref, sc_ref, dt_ref, cw_ref, cb_ref, ax_ref, e_ref,
                    xs_ref, bm_ref, cm_ref, xdt_ref, dec_ref):
    conv = cb_ref[...] + sc_ref[:, 0:CONV_CH] * cw_ref[0:1, :]
    conv = conv + sc_ref[:, CONV_CH:2 * CONV_CH] * cw_ref[1:2, :]
    conv = conv + sc_ref[:, 2 * CONV_CH:3 * CONV_CH] * cw_ref[2:3, :]
    conv = conv + rest_ref[...] * cw_ref[3:4, :]
    xc = conv * jax.nn.sigmoid(conv)
    xs = xc[:, :D_INNER]
    dtx = _expand_heads(dt_ref[...], e_ref[...])
    xs_ref[...] = xs
    bm_ref[...] = xc[:, D_INNER:D_INNER + SSD_GROUPS * D_STATE]
    cm_ref[...] = xc[:, D_INNER + SSD_GROUPS * D_STATE:]
    xdt_ref[...] = xs * dtx
    dec_ref[...] = jnp.exp(dtx * ax_ref[...])


def _state_step_body(h_ref, xdt_ref, dec_ref, b_ref, c_ref, xs_ref, dx_ref, ho_ref, y_ref, *, bb):
    grow = lax.broadcasted_iota(jnp.int32, (SSD_GROUPS, D_INNER), 0)
    glane = lax.broadcasted_iota(jnp.int32, (SSD_GROUPS, D_INNER), 1) // SSD_GW
    gmask = grow == glane
    for bi in range(bb):
        h = h_ref[bi]
        bmat = b_ref[bi]
        bx = jnp.concatenate([jnp.broadcast_to(bmat[g:g + 1, :], (SSD_GW, D_STATE))
                              for g in range(SSD_GROUPS)], axis=0)
        hn = h * dec_ref[:, bi:bi + 1] + xdt_ref[:, bi:bi + 1] * bx
        ho_ref[bi] = hn
        y8 = _nt(c_ref[bi].astype(BF16), hn.astype(BF16))
        y = jnp.sum(jnp.where(gmask, y8, 0.0), axis=0, keepdims=True)
        y_ref[bi:bi + 1, :] = y + dx_ref[...] * xs_ref[bi:bi + 1, :]


def _ssd_step(rest, dt, state_conv, state_ssm, conv_w, conv_b, a_x, d_x, emat, bb=4):
    nb = rest.shape[0]
    nblk = nb // bb
    full = lambda shape: pl.BlockSpec(shape, lambda i: (0,) * len(shape))
    xs, bm, cm, xdt, dec = pl.pallas_call(
        _conv_step_body,
        out_shape=(jax.ShapeDtypeStruct((nb, D_INNER), F32),
                   jax.ShapeDtypeStruct((nb, SSD_GROUPS * D_STATE), F32),
                   jax.ShapeDtypeStruct((nb, SSD_GROUPS * D_STATE), F32),
                   jax.ShapeDtypeStruct((nb, D_INNER), F32),
                   jax.ShapeDtypeStruct((nb, D_INNER), F32)),
        grid=(1,),
        in_specs=[pl.BlockSpec((nb, CONV_CH), lambda i: (0, 0)),
                  full((nb, (CONV_W - 1) * CONV_CH)), full((nb, LANES)),
                  full((CONV_W, CONV_CH)), full((1, CONV_CH)), full((1, D_INNER)),
                  full((LANES, D_INNER))],
        out_specs=(full((nb, D_INNER)), full((nb, SSD_GROUPS * D_STATE)),
                   full((nb, SSD_GROUPS * D_STATE)), full((nb, D_INNER)), full((nb, D_INNER))),
        compiler_params=_params(("arbitrary",)),
        name="conv_step",
    )(rest, state_conv.reshape(nb, (CONV_W - 1) * CONV_CH), dt, conv_w, conv_b, a_x, emat)

    to_cols = lambda a: a.reshape(nblk, bb, D_INNER).transpose(0, 2, 1)
    blk3 = lambda d1, d2: pl.BlockSpec((None, d1, d2), lambda i: (i, 0, 0))
    hspec = pl.BlockSpec((bb, D_INNER, D_STATE), lambda i: (i, 0, 0))
    gspec = pl.BlockSpec((bb, SSD_GROUPS, D_STATE), lambda i: (i, 0, 0))
    h_new, y = pl.pallas_call(
        functools.partial(_state_step_body, bb=bb),
        out_shape=(jax.ShapeDtypeStruct((nb, D_INNER, D_STATE), F32),
                   jax.ShapeDtypeStruct((nblk, bb, D_INNER), F32)),
        grid=(nblk,),
        in_specs=[hspec, blk3(D_INNER, bb), blk3(D_INNER, bb), gspec, gspec, blk3(bb, D_INNER),
                  pl.BlockSpec((1, D_INNER), lambda i: (0, 0))],
        out_specs=(hspec, blk3(bb, D_INNER)),
        compiler_params=_params(("parallel",)),
        name="state_step",
    )(state_ssm.reshape(nb, D_INNER, D_STATE), to_cols(xdt), to_cols(dec),
      bm.reshape(nb, SSD_GROUPS, D_STATE), cm.reshape(nb, SSD_GROUPS, D_STATE),
      xs.reshape(nblk, bb, D_INNER), d_x)
    return y.reshape(nb, D_INNER), h_new


def _merge_body(x_ref, att_ref, y_ref, z_ref, ga_ref, gb_ref, gn_ref, wa_ref, wb_ref, wo_ref, o_ref):
    att = jnp.concatenate([att_ref[c] for c in range(ATT_SLABS)], axis=1)
    out_a = jnp.dot(att.astype(BF16), wa_ref[...], preferred_element_type=F32)
    z = z_ref[...]
    y = y_ref[...] * (z * jax.nn.sigmoid(z))
    yn = _rmsnorm_rows(y, gn_ref[...]).astype(BF16)
    out_b = jnp.dot(yn, wb_ref[...], preferred_element_type=F32)
    merged = jax.nn.sigmoid(ga_ref[...]) * out_a + jax.nn.sigmoid(gb_ref[...]) * out_b
    o_ref[...] = x_ref[...] + jnp.dot(merged.astype(BF16), wo_ref[...], preferred_element_type=F32)


def _merge(x2d, att, y, rest, ssd_norm, w_att_out, w_ssd_out, w_out, tm=256):
    t = x2d.shape[0]
    tm = min(tm, t)
    row = lambda w, cb: pl.BlockSpec((tm, w), lambda i: (i, cb))
    full = lambda shape: pl.BlockSpec(shape, lambda i: (0, 0))
    return pl.pallas_call(
        _merge_body,
        out_shape=jax.ShapeDtypeStruct((t, D_MODEL), F32),
        grid=(t // tm,),
        in_specs=[row(D_MODEL, 0),
                  pl.BlockSpec((ATT_SLABS, tm, LANES), lambda i: (0, i, 0)),
                  row(D_INNER, 0),
                  row(D_INNER, REST_Z // D_INNER),
                  row(D_MODEL, REST_GATES // D_MODEL), row(D_MODEL, REST_GATES // D_MODEL + 1),
                  full((1, D_INNER)), full((ATT_W, D_MODEL)), full((D_INNER, D_MODEL)),
                  full((D_MODEL, D_MODEL))],
        out_specs=row(D_MODEL, 0),
        compiler_params=_params(("parallel",)),
        name="merge",
    )(x2d, att, y, rest, rest, rest, ssd_norm, w_att_out, w_ssd_out, w_out)


I1_BLOCK = SUBLANES
BF16_ROWS = 2 * SUBLANES
PEER_GATE_TT = 512
PEER_DENSE_TT = 1024
PEER_DENSE_VMEM = VMEM_BYTES_V7X * 7 // 8


def _gelu_tanh(x):
    return 0.5 * x * (1.0 + jnp.tanh(math.sqrt(2.0 / math.pi) * (x + 0.044715 * (x * x * x))))


def _peer_gate_body(x_ref, g_ref, wq_ref, k1_ref, k2_ref,
                    xnt_ref, r2_ref, e2_ref, n1_ref, w1_ref, s_all, v_all, rk1, *, tg):
    nch = tg // LANES
    nkb = N_KEYS // I1_BLOCK
    xn = _rmsnorm_rows(x_ref[...], g_ref[...])
    xnt = xn.T.astype(BF16)
    xnt_ref[...] = xnt
    qt = jnp.dot(wq_ref[...], xnt, preferred_element_type=F32)
    for h in range(PEER_HEADS):
        for sd, kref in enumerate((k1_ref, k2_ref)):
            r0 = (2 * h + sd) * N_KEYS
            s_all[2 * h + sd] = jnp.dot(kref[h], qt[r0:r0 + N_KEYS, :].astype(BF16),
                                        preferred_element_type=F32)

    rowid = lax.broadcasted_iota(jnp.int32, (N_KEYS, LANES), 0).astype(F32)
    row16 = lax.broadcasted_iota(jnp.int32, (PEER_TOPK, LANES), 0)
    row16f = row16.astype(F32)

    def make_extract(exact):
        def extract(idx, ties):
            h = idx // nch
            off = pl.multiple_of((idx % nch) * LANES, LANES)
            s = [s_all[2 * h + sd, :, pl.ds(off, LANES)] for sd in range(2)]
            rank = [jnp.full((N_KEYS, LANES), float(PEER_TOPK), F32) for _ in range(2)]
            vals = [jnp.zeros((PEER_TOPK, LANES), F32) for _ in range(2)]
            for k in range(PEER_TOPK):
                for sd in range(2):
                    m = jnp.max(s[sd], axis=0, keepdims=True)
                    sel = s[sd] == m
                    if exact:
                        first = jnp.min(jnp.where(sel, rowid, float(N_KEYS)), axis=0, keepdims=True)
                        sel = rowid == first
                    rank[sd] = jnp.where(sel, float(k), rank[sd])
                    s[sd] = jnp.where(sel, -jnp.inf, s[sd])
                    vals[sd] = jnp.where(row16 == k, m, vals[sd])
            rk1[h, :, pl.ds(off, LANES)] = rank[0]
            r2_ref[h, :, :, pl.ds(off, LANES)] = rank[1].reshape(N_KEYS // BF16_ROWS, BF16_ROWS, LANES).astype(BF16)
            v_all[2 * h, :, pl.ds(off, LANES)] = vals[0]
            v_all[2 * h + 1, :, pl.ds(off, LANES)] = vals[1]
            if not exact:
                for sd in range(2):
                    taken = jnp.sum(jnp.where(rank[sd] < float(PEER_TOPK), 1.0, 0.0), axis=0, keepdims=True)
                    ties = jnp.maximum(ties, jnp.where(taken != float(PEER_TOPK), 1.0, 0.0))
            return ties
        return extract

    no_ties = jnp.zeros((1, LANES), F32)
    ties = lax.fori_loop(0, PEER_HEADS * nch, make_extract(False), no_ties)

    @pl.when(jnp.max(ties) > 0.0)
    def _():
        lax.fori_loop(0, PEER_HEADS * nch, make_extract(True), no_ties)

    def finish(h, carry):
        for c in range(nch):
            lanes = slice(c * LANES, (c + 1) * LANES)
            v1 = v_all[2 * h, :, lanes]
            v2 = v_all[2 * h + 1, :, lanes]
            shifted = [jnp.broadcast_to(v2[b:b + 1, :], (PEER_TOPK, LANES)) for b in range(PEER_TOPK)]
            cnt = jnp.zeros((PEER_TOPK, LANES), F32)
            zsum = jnp.zeros((1, LANES), F32)
            top = v1[0:1, :] + v2[0:1, :]
            for step in range(PEER_TOPK):
                front = v1 + shifted[0]
                m = jnp.max(front, axis=0, keepdims=True)
                first = jnp.min(jnp.where(front == m, row16f, float(PEER_TOPK)), axis=0, keepdims=True)
                sel = row16f == first
                cnt = cnt + jnp.where(sel, 1.0, 0.0)
                zsum = zsum + jnp.exp(m - top)
                live = PEER_TOPK - 1 - step
                for b in range(live):
                    shifted[b] = jnp.where(sel, shifted[b + 1], shifted[b])
            rz = 1.0 / zsum
            r1 = rk1[h, :, lanes]
            s1 = s_all[2 * h, :, lanes]
            s2 = s_all[2 * h + 1, :, lanes]
            n_of = jnp.zeros((N_KEYS, LANES), F32)
            for a in range(PEER_TOPK):
                n_of = jnp.where(r1 == float(a), cnt[a:a + 1, :], n_of)
            wgt = jnp.exp(s1 - v1[0:1, :]) * rz
            e2v = jnp.exp(s2 - v2[0:1, :])
            e2_ref[h, :, :, lanes] = e2v.reshape(N_KEYS // BF16_ROWS, BF16_ROWS, LANES).astype(BF16)
            for kb in range(nkb):
                rs = slice(kb * I1_BLOCK, (kb + 1) * I1_BLOCK)
                n1_ref[kb, h, :, lanes] = n_of[rs, :]
                w1_ref[kb, h, :, lanes] = wgt[rs, :]
        return carry

    lax.fori_loop(0, PEER_HEADS, finish, 0)


def _peer_dense_body(x_ref, xnt_ref, r2_ref, e2_ref, n1_ref, w1_ref, u_ref, vt_ref, o_ref,
                     acct, act, hbuf, *, tt):
    j = pl.program_id(1)
    nch = tt // LANES
    half = I1_BLOCK // 2 * N_KEYS
    ktiles = N_KEYS // BF16_ROWS

    @pl.when(j == 0)
    def _():
        acct[...] = jnp.zeros_like(acct)

    xnt = xnt_ref[...]
    for hf in range(2):
        act[hf * half:(hf + 1) * half, :] = jnp.dot(u_ref[hf * half:(hf + 1) * half, :], xnt,
                                                    preferred_element_type=F32)
    for hf in range(2):
        for ii in range(hf * I1_BLOCK // 2, (hf + 1) * I1_BLOCK // 2):
            for c in range(nch):
                lanes = slice(c * LANES, (c + 1) * LANES)
                gate = None
                for h in range(PEER_HEADS):
                    n16 = jnp.broadcast_to(n1_ref[h, ii:ii + 1, lanes], (BF16_ROWS, LANES)).astype(BF16)
                    w16 = jnp.broadcast_to(w1_ref[h, ii:ii + 1, lanes], (BF16_ROWS, LANES)).astype(BF16)
                    prod = e2_ref[h, :, :, lanes] * w16[None]
                    term = jnp.where(r2_ref[h, :, :, lanes] < n16[None], prod, jnp.zeros_like(prod))
                    gate = term if gate is None else gate + term
                a = act[ii * N_KEYS:(ii + 1) * N_KEYS, lanes]
                hv = gate.astype(F32).reshape(N_KEYS, LANES) * _gelu_tanh(a)
                hbuf[ii * N_KEYS:(ii + 1) * N_KEYS, lanes] = hv.astype(BF16)
        acct[...] += jnp.dot(vt_ref[:, hf * half:(hf + 1) * half], hbuf[hf * half:(hf + 1) * half, :],
                             preferred_element_type=F32)

    @pl.when(j == pl.num_programs(1) - 1)
    def _():
        o_ref[...] = x_ref[...] + acct[...].T


def _peer(x2d, norm_ffn, wq_t, keys1, keys2, u_bf, vt_bf):
    t = x2d.shape[0]
    tg = min(PEER_GATE_TT, t)
    tt = min(PEER_DENSE_TT, t)
    nkb = N_KEYS // I1_BLOCK
    eb = I1_BLOCK * N_KEYS
    ktiles = N_KEYS // BF16_ROWS
    full1 = lambda shape: pl.BlockSpec(shape, lambda i: (0,) * len(shape))
    tab_shape = jax.ShapeDtypeStruct((PEER_HEADS, ktiles, BF16_ROWS, t), BF16)
    row_shape = jax.ShapeDtypeStruct((nkb, PEER_HEADS, I1_BLOCK, t), F32)
    tab_spec1 = pl.BlockSpec((PEER_HEADS, ktiles, BF16_ROWS, tg), lambda i: (0, 0, 0, i))
    row_spec1 = pl.BlockSpec((nkb, PEER_HEADS, I1_BLOCK, tg), lambda i: (0, 0, 0, i))
    xnt, r2, e2, n1, w1 = pl.pallas_call(
        functools.partial(_peer_gate_body, tg=tg),
        out_shape=(jax.ShapeDtypeStruct((D_MODEL, t), BF16), tab_shape, tab_shape, row_shape, row_shape),
        grid=(t // tg,),
        in_specs=[pl.BlockSpec((tg, D_MODEL), lambda i: (i, 0)),
                  full1((1, D_MODEL)), full1((2 * PEER_HEADS * N_KEYS, D_MODEL)),
                  full1((PEER_HEADS, N_KEYS, N_KEYS)), full1((PEER_HEADS, N_KEYS, N_KEYS))],
        out_specs=(pl.BlockSpec((D_MODEL, tg), lambda i: (0, i)), tab_spec1, tab_spec1, row_spec1, row_spec1),
        scratch_shapes=[pltpu.VMEM((2 * PEER_HEADS, N_KEYS, tg), F32),
                        pltpu.VMEM((2 * PEER_HEADS, PEER_TOPK, tg), F32),
                        pltpu.VMEM((PEER_HEADS, N_KEYS, tg), F32)],
        compiler_params=_params(("parallel",)),
        name="peer_gate",
    )(x2d, norm_ffn, wq_t, keys1, keys2)

    tab_spec = pl.BlockSpec((PEER_HEADS, ktiles, BF16_ROWS, tt), lambda i, j: (0, 0, 0, i))
    row_spec = pl.BlockSpec((None, PEER_HEADS, I1_BLOCK, tt), lambda i, j: (j, 0, 0, i))
    return pl.pallas_call(
        functools.partial(_peer_dense_body, tt=tt),
        out_shape=jax.ShapeDtypeStruct((t, D_MODEL), F32),
        grid=(t // tt, nkb),
        in_specs=[pl.BlockSpec((tt, D_MODEL), lambda i, j: (i, 0)),
                  pl.BlockSpec((D_MODEL, tt), lambda i, j: (0, i)),
                  tab_spec, tab_spec, row_spec, row_spec,
                  pl.BlockSpec((eb, D_MODEL), lambda i, j: (j, 0)),
                  pl.BlockSpec((D_MODEL, eb), lambda i, j: (0, j))],
        out_specs=pl.BlockSpec((tt, D_MODEL), lambda i, j: (i, 0)),
        scratch_shapes=[pltpu.VMEM((D_MODEL, tt), F32),
                        pltpu.VMEM((eb, tt), F32),
                        pltpu.VMEM((eb, tt), BF16)],
        compiler_params=pltpu.CompilerParams(dimension_semantics=("parallel", "arbitrary"),
                                             vmem_limit_bytes=PEER_DENSE_VMEM),
        name="peer_dense",
    )(x2d, xnt, r2, e2, n1, w1, u_bf, vt_bf)


def _ple_body(x_ref, p_ref, gp_ref, wg_ref, wp_ref, gf_ref, o_ref):
    x = x_ref[...]
    xn = _rmsnorm_rows(x, gp_ref[...]).astype(BF16)
    pg = jax.nn.sigmoid(jnp.dot(xn, wg_ref[...], preferred_element_type=F32))
    x3 = x + pg * jnp.dot(p_ref[...].astype(BF16), wp_ref[...], preferred_element_type=F32)
    o_ref[...] = _rmsnorm_rows(x3, gf_ref[...])


def _ple_final(x2d, p2d, norm_ple, w_gate, w_proj, norm_final, tm=512):
    t = x2d.shape[0]
    tm = min(tm, t)
    full = lambda shape: pl.BlockSpec(shape, lambda i: (0, 0))
    return pl.pallas_call(
        _ple_body,
        out_shape=jax.ShapeDtypeStruct((t, D_MODEL), F32),
        grid=(t // tm,),
        in_specs=[pl.BlockSpec((tm, D_MODEL), lambda i: (i, 0)),
                  pl.BlockSpec((tm, PLE_DIM), lambda i: (i, 0)),
                  full((1, D_MODEL)), full((D_MODEL, D_MODEL)), full((PLE_DIM, D_MODEL)),
                  full((1, D_MODEL))],
        out_specs=pl.BlockSpec((tm, D_MODEL), lambda i: (i, 0)),
        compiler_params=_params(("parallel",)),
        name="ple_final",
    )(x2d, p2d, norm_ple, w_gate, w_proj, norm_final)


def _prep_weights(norm_mix, w_in, conv_w, conv_b, dt_bias, a_log, d_skip, ssd_norm, w_att_out,
                  w_ssd_out, w_out, norm_ffn, w_peer_q, peer_keys1, peer_keys2, peer_u, peer_v,
                  norm_ple, w_ple_gate, w_ple_proj, norm_final):
    w = w_in[0]
    c = [0, Q_COLS, 2 * Q_COLS, 3 * Q_COLS, 3 * Q_COLS + D_INNER, 3 * Q_COLS + D_INNER + CONV_CH,
         3 * Q_COLS + D_INNER + CONV_CH + SSD_HEADS]
    wq, wk, wv, wz, wxbc, wdt, wgt = (w[:, c[0]:c[1]], w[:, c[1]:c[2]], w[:, c[2]:c[3]], w[:, c[3]:c[4]],
                                      w[:, c[4]:c[5]], w[:, c[5]:c[6]], w[:, c[6]:])
    row = lambda v: v.reshape(1, -1).astype(F32)
    p = {}
    p["norm_mix"] = row(norm_mix[0])
    p["w_qkv"] = jnp.concatenate([_split_halves(wq * (HEAD_DIM ** -0.5)), _split_halves(wk), wv],
                                 axis=1).astype(BF16)
    p["w_rest"] = jnp.concatenate([wxbc, wz, wgt], axis=1).astype(BF16)
    p["w_dt"] = jnp.pad(wdt, ((0, 0), (0, LANES - SSD_HEADS))).astype(BF16)
    p["dt_bias"] = jnp.pad(row(dt_bias[0]), ((0, 0), (0, LANES - SSD_HEADS)))
    p["conv_w"] = conv_w[0].astype(F32)
    p["conv_b"] = row(conv_b[0])
    p["a_row"], p["a_x"], p["d_x"], p["emat"], p["ltri"] = _ssd_consts(a_log[0], d_skip[0])
    p["ssd_norm"] = row(ssd_norm[0])
    p["w_att_out"] = w_att_out[0].astype(BF16)
    p["w_ssd_out"] = w_ssd_out[0].astype(BF16)
    p["w_out"] = w_out[0].astype(BF16)
    p["norm_ffn"] = row(norm_ffn[0])
    p["wq_t"] = w_peer_q[0].T.astype(BF16)
    p["keys1"] = peer_keys1[0].astype(BF16)
    p["keys2"] = peer_keys2[0].astype(BF16)
    p["u"] = peer_u[0].astype(BF16)
    p["vt"] = peer_v[0].T.astype(BF16)
    p["norm_ple"] = row(norm_ple[0])
    p["w_ple_gate"] = w_ple_gate[0].astype(BF16)
    p["w_ple_proj"] = w_ple_proj[0].astype(BF16)
    p["norm_final"] = row(norm_final)
    return p


def _split_halves(w):
    k, n = w.shape
    return w.reshape(k, n // LANES, 2, 2, HEAD_DIM // 2).transpose(0, 1, 3, 2, 4).reshape(k, n)


def _tail(x2d, att, y, rest, p2d, p):
    x1 = _merge(x2d, att, y, rest, p["ssd_norm"], p["w_att_out"], p["w_ssd_out"], p["w_out"])
    x2 = _peer(x1, p["norm_ffn"], p["wq_t"], p["keys1"], p["keys2"], p["u"], p["vt"])
    return _ple_final(x2, p2d, p["norm_ple"], p["w_ple_gate"], p["w_ple_proj"], p["norm_final"])


def _heads_from_slabs(qkv, which, g, b, s, rows):
    s0 = which * (Q_COLS // LANES) + g * ATT_SLABS
    x = qkv.reshape(QKV_COLS // LANES, b, s, LANES)[s0:s0 + ATT_SLABS, :, s - rows:]
    x = x.transpose(1, 2, 0, 3)
    if which < 2:
        x = x.reshape(b, rows, ATT_SLABS, 2, 2, HEAD_DIM // 2).transpose(0, 1, 2, 4, 3, 5)
    return x.reshape(b, rows, N_HEADS, HEAD_DIM)


def _kv_rows(qkv, g, b, s, rows):
    return jnp.stack([_heads_from_slabs(qkv, 1, g, b, s, rows),
                      _heads_from_slabs(qkv, 2, g, b, s, rows)], axis=2)[None]


def _prompt(x, p_in, p):
    b, s, _ = x.shape
    x2d = x.reshape(b * s, D_MODEL)
    qkv, rest, dt = _projections(x2d, jnp.arange(s, dtype=jnp.int32), p["norm_mix"], p["w_qkv"],
                                 p["w_rest"], p["w_dt"], p["dt_bias"], tm=PROJ_TM)
    att = _attn_prompt(qkv, b, s)
    y, ssm = _ssd_prompt(rest, dt, p["conv_w"], p["conv_b"], p["a_row"], p["d_x"], p["emat"], p["ltri"], b, s)
    out = _tail(x2d, att, y, rest, p_in[0].reshape(b * s, PLE_DIM), p)
    kvs = [_kv_rows(qkv, g, b, s, min(win, s)) for g, (win, _) in enumerate(ATT_GROUPS)]
    conv = rest.reshape(b, s, REST_COLS)[:, s - (CONV_W - 1):, :CONV_CH][None]
    ssm = ssm.reshape(1, b, SSD_HEADS, CHUNK, D_STATE)
    return out.reshape(b, s, D_MODEL), kvs, conv, ssm


def _sample(x, p_in, caches, state_conv, state_ssm, p):
    b, s, _ = x.shape
    assert s == 1
    x2d = x.reshape(b, D_MODEL)
    pos = jnp.full((b,), PAST_LEN, dtype=jnp.int32)
    qkv, rest, dt = _projections(x2d, pos, p["norm_mix"], p["w_qkv"], p["w_rest"], p["w_dt"],
                                 p["dt_bias"], tm=b)
    rows = lambda which: jnp.stack([_heads_from_slabs(qkv, which, g, b, 1, 1).reshape(b, ATT_W)
                                    for g in range(N_GROUPS)], axis=1)
    q3, k3, v3 = rows(0), rows(1), rows(2)
    att = _attn_step(q3, k3, v3, [c[0] for c in caches])
    att = att.reshape(b, ATT_SLABS, LANES).transpose(1, 0, 2)
    xbc = rest[:, :CONV_CH]
    y, ssm = _ssd_step(xbc, dt, state_conv[0], state_ssm[0],
                       p["conv_w"], p["conv_b"], p["a_x"], p["d_x"], p["emat"])
    out = _tail(x2d, att, y, rest, p_in[0].reshape(b, PLE_DIM), p)
    hd = lambda a, g: a[:, g].reshape(b, N_HEADS, HEAD_DIM)
    kvs = [jnp.stack([hd(k3, g), hd(v3, g)], axis=1)[None, :, None] for g in range(N_GROUPS)]
    conv = jnp.concatenate([state_conv[0][:, 1:], xbc[:, None, :]], axis=1)[None]
    ssm = ssm.reshape(1, b, SSD_HEADS, CHUNK, D_STATE)
    return out.reshape(b, 1, D_MODEL), kvs, conv, ssm


def kernel(x_prompt, x_sample, cache_kv_w128, cache_kv_w512, cache_kv_w2048, state_conv, state_ssm, p_prompt, p_sample, norm_mix, w_in, conv_w, conv_b, dt_bias, a_log, d_skip, ssd_norm, w_att_out, w_ssd_out, w_out, norm_ffn, w_peer_q, peer_keys1, peer_keys2, peer_u, peer_v, norm_ple, w_ple_gate, w_ple_proj, norm_final):
    p = _prep_weights(norm_mix, w_in, conv_w, conv_b, dt_bias, a_log, d_skip, ssd_norm, w_att_out,
                      w_ssd_out, w_out, norm_ffn, w_peer_q, peer_keys1, peer_keys2, peer_u, peer_v,
                      norm_ple, w_ple_gate, w_ple_proj, norm_final)
    y_p, kv_p, conv_p, ssm_p = _prompt(x_prompt, p_prompt, p)
    y_s, kv_s, conv_s, ssm_s = _sample(x_sample, p_sample, (cache_kv_w128, cache_kv_w512, cache_kv_w2048),
                                       state_conv, state_ssm, p)
    return (y_p, y_s, kv_p[0], kv_p[1], kv_p[2], conv_p, ssm_p, kv_s[0], kv_s[1], kv_s[2], conv_s, ssm_s)
```

```python
import functools
import math

import jax
import jax.numpy as jnp
from jax import lax
from jax.experimental import pallas as pl
from jax.experimental.pallas import tpu as pltpu

F32 = jnp.float32
BF16 = jnp.bfloat16

LANES = 128
SUBLANES = 8
VMEM_BYTES_V7X = 64 * 1024 * 1024
VMEM_LIMIT = VMEM_BYTES_V7X * 3 // 4

D_MODEL = 1024
ATT_GROUPS = ((128, 1), (512, 4), (2048, 16))
N_GROUPS = len(ATT_GROUPS)
N_HEADS = 8
HEAD_DIM = 64
ATT_W = N_HEADS * HEAD_DIM
ATT_SLABS = ATT_W // LANES
Q_COLS = N_GROUPS * ATT_W
QKV_COLS = 3 * Q_COLS
ROPE_THETA = 10000.0
D_INNER = 2048
SSD_HEADS = 32
SSD_GROUPS = 8
SSD_GW = D_INNER // SSD_GROUPS
D_STATE = 128
CONV_W = 4
CONV_CH = D_INNER + 2 * SSD_GROUPS * D_STATE
CHUNK = 64
N_KEYS = 128
N_EXPERTS = N_KEYS * N_KEYS
PEER_HEADS = 8
PEER_TOPK = 16
PLE_DIM = 256
EPS = 1e-6
PAST_LEN = 8192
NEG = -1e30

REST_Z = CONV_CH
REST_GATES = REST_Z + D_INNER
REST_COLS = REST_GATES + 2 * D_MODEL
QKV_TN = 1536
REST_TN = 2048
PROJ_TM = 1024
ATT_ROWS = 4096
ATT_ILP = 8
SSD_STEP_CHUNKS = 4


def _params(sem):
    return pltpu.CompilerParams(dimension_semantics=sem, vmem_limit_bytes=VMEM_LIMIT)


def _rmsnorm_rows(x, g):
    return x * lax.rsqrt(jnp.mean(x * x, axis=-1, keepdims=True) + EPS) * g


def _nt(a, b):
    return lax.dot_general(a, b, (((1,), (1,)), ((), ())), preferred_element_type=F32)


def _proj_qkv_body(x_ref, g_ref, w_ref, cos_ref, sin_ref, o_ref, xn_ref, *, rope_tiles):
    j = pl.program_id(1)

    @pl.when(j == 0)
    def _():
        xn_ref[...] = _rmsnorm_rows(x_ref[...], g_ref[...]).astype(BF16)

    acc = jnp.dot(xn_ref[...], w_ref[...], preferred_element_type=F32)
    n_slab = acc.shape[1] // LANES

    @pl.when(j < rope_tiles)
    def _():
        cos = cos_ref[...]
        sin = sin_ref[...]
        for c in range(n_slab):
            a = acc[:, c * LANES:(c + 1) * LANES]
            o_ref[c] = a * cos + pltpu.roll(a, LANES // 2, 1) * sin

    @pl.when(j >= rope_tiles)
    def _():
        for c in range(n_slab):
            o_ref[c] = acc[:, c * LANES:(c + 1) * LANES]


def _proj_rest_body(x_ref, g_ref, w_ref, wdt_ref, bias_ref, o_ref, dt_ref, xn_ref):
    @pl.when(pl.program_id(1) == 0)
    def _():
        xn = _rmsnorm_rows(x_ref[...], g_ref[...]).astype(BF16)
        xn_ref[...] = xn
        v = jnp.dot(xn, wdt_ref[...], preferred_element_type=F32) + bias_ref[...]
        dt_ref[...] = jnp.maximum(v, 0.0) + jnp.log1p(jnp.exp(-jnp.abs(v)))

    o_ref[...] = jnp.dot(xn_ref[...], w_ref[...], preferred_element_type=F32)


def _rope_tables(pos):
    half = HEAD_DIM // 2
    inv = jnp.exp(jnp.arange(half, dtype=F32) * (-2.0 * math.log(ROPE_THETA) / HEAD_DIM))
    ang = pos.astype(F32)[:, None] * inv[None, :]
    cos, sin = jnp.cos(ang), jnp.sin(ang)
    cos_t = jnp.concatenate([cos, cos, cos, cos], axis=1)
    sin_t = jnp.concatenate([-sin, -sin, sin, sin], axis=1)
    return cos_t, sin_t


def _projections(x2d, pos, norm_mix, w_qkv, w_rest, w_dt, dt_bias_row, tm):
    t, k = x2d.shape
    tm = min(tm, t)
    cos_t, sin_t = _rope_tables(pos)
    n_pos_blocks = pos.shape[0] // tm
    x_spec = pl.BlockSpec((tm, k), lambda i, j: (i, 0))
    g_spec = pl.BlockSpec((1, k), lambda i, j: (0, 0))
    tab_spec = pl.BlockSpec((tm, LANES), lambda i, j: (i % n_pos_blocks, 0))
    slabs = QKV_TN // LANES
    qkv = pl.pallas_call(
        functools.partial(_proj_qkv_body, rope_tiles=2 * Q_COLS // QKV_TN),
        out_shape=jax.ShapeDtypeStruct((QKV_COLS // LANES, t, LANES), F32),
        grid=(t // tm, QKV_COLS // QKV_TN),
        in_specs=[x_spec, g_spec, pl.BlockSpec((k, QKV_TN), lambda i, j: (0, j)), tab_spec, tab_spec],
        out_specs=pl.BlockSpec((slabs, tm, LANES), lambda i, j: (j, i, 0)),
        scratch_shapes=[pltpu.VMEM((tm, k), BF16)],
        compiler_params=_params(("parallel", "arbitrary")),
        name="proj_qkv",
    )(x2d, norm_mix, w_qkv, cos_t, sin_t)
    rest, dt = pl.pallas_call(
        _proj_rest_body,
        out_shape=(jax.ShapeDtypeStruct((t, REST_COLS), F32), jax.ShapeDtypeStruct((t, LANES), F32)),
        grid=(t // tm, REST_COLS // REST_TN),
        in_specs=[x_spec, g_spec, pl.BlockSpec((k, REST_TN), lambda i, j: (0, j)),
                  pl.BlockSpec((k, LANES), lambda i, j: (0, 0)), pl.BlockSpec((1, LANES), lambda i, j: (0, 0))],
        out_specs=(pl.BlockSpec((tm, REST_TN), lambda i, j: (i, j)),
                   pl.BlockSpec((tm, LANES), lambda i, j: (i, 0))),
        scratch_shapes=[pltpu.VMEM((tm, k), BF16)],
        compiler_params=_params(("parallel", "arbitrary")),
        name="proj_rest",
    )(x2d, norm_mix, w_rest, w_dt, dt_bias_row)
    return qkv, rest, dt


def _attn_prompt_body(*refs, dil, has_prev_group, emit_lse):
    q_ref, kc_ref, kp_ref, vc_ref, vp_ref = refs[:5]
    pos = 5
    if has_prev_group:
        op_ref, lp_ref = refs[pos:pos + 2]
        pos += 2
    o_ref = refs[pos]
    pos += 1
    if emit_lse:
        l_ref = refs[pos]
        pos += 1
    kbuf, vbuf = refs[pos:pos + 2]

    i = pl.program_id(1)
    rows = q_ref.shape[0]
    span = LANES * dil
    kbuf[0:span, :] = kp_ref[rows - span:rows, :]
    kbuf[span:span + rows, :] = kc_ref[...]
    vbuf[0:span, :] = vp_ref[rows - span:rows, :]
    vbuf[span:span + rows, :] = vc_ref[...]

    krow = lax.broadcasted_iota(jnp.int32, (2 * LANES, 2 * LANES), 0)
    qidx = lax.broadcasted_iota(jnp.int32, (2 * LANES, 2 * LANES), 1) & (LANES - 1)
    own_ok = jnp.logical_and(krow >= LANES, krow - LANES <= qidx)
    prev_ok = jnp.logical_and(krow < LANES, krow >= qidx)
    low_half = (lax.broadcasted_iota(jnp.int32, (LANES, LANES), 1) & (HEAD_DIM // 2)) == 0

    def rd(ref, start):
        if dil == 1:
            return ref[pl.ds(start, LANES), :]
        return ref[pl.ds(start, LANES, stride=dil), :]

    def sub_blocks(p, carry):
        us = [p * ATT_ILP + k for k in range(ATT_ILP)]
        sps = [u // dil for u in us]
        starts = [sp * span + (u % dil) for u, sp in zip(us, sps)]
        scs, vsts = [], []
        for sp, start in zip(sps, starts):
            q2 = rd(q_ref, start)
            qs = jnp.concatenate([jnp.where(low_half, q2, 0.0), jnp.where(low_half, 0.0, q2)], axis=0).astype(BF16)
            ks = jnp.concatenate([rd(kbuf, start), rd(kbuf, span + start)], axis=0).astype(BF16)
            mask = jnp.logical_or(own_ok, jnp.logical_and(prev_ok, jnp.logical_or(i > 0, sp > 0)))
            scs.append(jnp.where(mask, _nt(ks, qs), NEG))
        for start in starts:
            vsts.append(jnp.concatenate([rd(vbuf, start), rd(vbuf, span + start)], axis=0).T.astype(BF16))
        es, dens, lses = [], [], []
        for sc in scs:
            m = jnp.max(sc, axis=0, keepdims=True)
            e = jnp.exp(sc - m)
            den = jnp.sum(e, axis=0, keepdims=True)
            es.append(e.astype(BF16))
            dens.append(den)
            lses.append(m + jnp.log(den))
        ots = [jnp.dot(vst, e, preferred_element_type=F32) for vst, e in zip(vsts, es)]
        for start, ot, den, lse in zip(starts, ots, dens, lses):
            ot = jnp.concatenate([ot[:HEAD_DIM, :LANES] / den[:, :LANES],
                                  ot[HEAD_DIM:, LANES:] / den[:, LANES:]], axis=0)
            lt = jnp.concatenate([jnp.broadcast_to(lse[:, :LANES], (HEAD_DIM, LANES)),
                                  jnp.broadcast_to(lse[:, LANES:], (HEAD_DIM, LANES))], axis=0)
            o2 = ot.T
            l2 = lt.T
            if has_prev_group:
                o_prev = rd(op_ref, start)
                l_prev = rd(lp_ref, start)
                mm = jnp.maximum(l_prev, l2)
                wp = jnp.exp(l_prev - mm)
                wc = jnp.exp(l2 - mm)
                tot = wp + wc
                o2 = (wp * o_prev + wc * o2) / tot
                l2 = mm + jnp.log(tot)
            if dil == 1:
                o_ref[pl.ds(start, LANES), :] = o2
                if emit_lse:
                    l_ref[pl.ds(start, LANES), :] = l2
            else:
                o_ref[pl.ds(start, LANES, stride=dil), :] = o2
                if emit_lse:
                    l_ref[pl.ds(start, LANES, stride=dil), :] = l2
        return carry

    lax.fori_loop(0, rows // (LANES * ATT_ILP), sub_blocks, 0)


def _attn_prompt_group(qkv, g, dil, b, s, prev):
    rows = min(ATT_ROWS, s)
    assert rows % (LANES * dil) == 0 and rows % (LANES * ATT_ILP) == 0 and s % rows == 0
    qkv4 = qkv.reshape(QKV_COLS // LANES, b, s, LANES)
    kslab = Q_COLS // LANES
    vslab = 2 * Q_COLS // LANES
    blk = (None, None, rows, LANES)
    in_specs = [
        pl.BlockSpec(blk, lambda bb, i, c: (g * ATT_SLABS + c, bb, i, 0)),
        pl.BlockSpec(blk, lambda bb, i, c: (kslab + g * ATT_SLABS + c, bb, i, 0)),
        pl.BlockSpec(blk, lambda bb, i, c: (kslab + g * ATT_SLABS + c, bb, jnp.maximum(i - 1, 0), 0)),
        pl.BlockSpec(blk, lambda bb, i, c: (vslab + g * ATT_SLABS + c, bb, i, 0)),
        pl.BlockSpec(blk, lambda bb, i, c: (vslab + g * ATT_SLABS + c, bb, jnp.maximum(i - 1, 0), 0)),
    ]
    args = [qkv4] * 5
    o_spec = pl.BlockSpec(blk, lambda bb, i, c: (c, bb, i, 0))
    if prev is not None:
        in_specs += [o_spec, o_spec]
        args += [prev[0].reshape(ATT_SLABS, b, s, LANES), prev[1].reshape(ATT_SLABS, b, s, LANES)]
    emit_lse = g < N_GROUPS - 1
    shape = jax.ShapeDtypeStruct((ATT_SLABS, b, s, LANES), F32)
    body = functools.partial(_attn_prompt_body, dil=dil, has_prev_group=prev is not None, emit_lse=emit_lse)
    out = pl.pallas_call(
        body,
        out_shape=(shape, shape) if emit_lse else shape,
        grid=(b, s // rows, ATT_SLABS),
        in_specs=in_specs,
        out_specs=(o_spec, o_spec) if emit_lse else o_spec,
        scratch_shapes=[pltpu.VMEM((LANES * dil + rows, LANES), F32),
                        pltpu.VMEM((LANES * dil + rows, LANES), F32)],
        compiler_params=_params(("parallel", "parallel", "parallel")),
        name="attn_prompt_g%d" % g,
    )(*args)
    if emit_lse:
        return out[0].reshape(ATT_SLABS, b * s, LANES), out[1].reshape(ATT_SLABS, b * s, LANES)
    return out.reshape(ATT_SLABS, b * s, LANES)


def _attn_prompt(qkv, b, s):
    prev = None
    for g, (win, dil) in enumerate(ATT_GROUPS):
        assert win // dil == LANES
        prev = _attn_prompt_group(qkv, g, dil, b, s, prev)
    return prev


def _split3(a):
    hi = a.astype(BF16)
    r1 = a - hi.astype(F32)
    mid = r1.astype(BF16)
    lo = (r1 - mid.astype(F32)).astype(BF16)
    return hi, mid, lo


def _expand_heads(a, emat):
    return sum(jnp.dot(part, emat, preferred_element_type=F32) for part in _split3(a))


def _ssd_prompt_body(xbc_ref, dt_ref, cw_ref, cb_ref, a_ref, dx_ref, e_ref, ltri_ref,
                     y_ref, st_ref, xext, h_ref, *, n_chunks):
    i = pl.program_id(1)
    lc = n_chunks * CHUNK
    halo = SUBLANES

    @pl.when(i == 0)
    def _():
        xext[0:halo, :] = jnp.zeros((halo, CONV_CH), F32)
        h_ref[...] = jnp.zeros_like(h_ref)

    xext[halo:halo + lc, :] = xbc_ref[...]

    lane_x = lax.broadcasted_iota(jnp.int32, (CHUNK, D_INNER), 1) & (CHUNK - 1)
    row_x = lax.broadcasted_iota(jnp.int32, (CHUNK, D_INNER), 0)
    diag = lane_x == row_x
    tril = row_x >= lane_x
    lane_p = lax.broadcasted_iota(jnp.int32, (CHUNK, LANES), 1)
    low_half = lane_p < CHUNK
    a_row = a_ref[...]
    dx = dx_ref[...]
    emat = e_ref[...]
    ltri = ltri_ref[...]

    for c in range(n_chunks):
        o = c * CHUNK
        conv = cb_ref[...] + xext[halo - 3 + o:halo - 3 + o + CHUNK, :] * cw_ref[0:1, :]
        conv = conv + xext[halo - 2 + o:halo - 2 + o + CHUNK, :] * cw_ref[1:2, :]
        conv = conv + xext[halo - 1 + o:halo - 1 + o + CHUNK, :] * cw_ref[2:3, :]
        conv = conv + xext[halo + o:halo + o + CHUNK, :] * cw_ref[3:4, :]
        xc = conv * jax.nn.sigmoid(conv)
        xs = xc[:, :D_INNER]
        bm = xc[:, D_INNER:D_INNER + SSD_GROUPS * D_STATE]
        cm = xc[:, D_INNER + SSD_GROUPS * D_STATE:]

        dt = dt_ref[o:o + CHUNK, :]
        dtx = _expand_heads(dt, emat)
        acs = sum(jnp.dot(ltri, part, preferred_element_type=F32) for part in _split3(dt * a_row))
        acsx = _expand_heads(acs, emat)
        last = acsx[CHUNK - 1:CHUNK, :]
        xdt = xs * dtx
        xw = (xdt * jnp.exp(last - acsx)).astype(BF16)
        eacs = jnp.exp(acsx)
        cdec = jnp.exp(last)
        rrow = jnp.sum(jnp.where(diag, acsx, 0.0), axis=0, keepdims=True)
        lm = jnp.exp(jnp.where(tril, acsx - rrow, NEG))
        xdtb = xdt.astype(BF16)
        cmb = cm.astype(BF16)

        for g in range(SSD_GROUPS):
            gs = slice(g * SSD_GW, (g + 1) * SSD_GW)
            bg = bm[:, g * D_STATE:(g + 1) * D_STATE]
            bgb = bg.astype(BF16)
            cg = cmb[:, g * D_STATE:(g + 1) * D_STATE]
            cbx = _nt(cg, jnp.concatenate([bgb] * (SSD_GW // CHUNK), axis=0))
            wg = (cbx * lm[:, gs]).astype(BF16)
            ydiag = []
            for pr in range(SSD_GW // LANES):
                xp = xdtb[:, g * SSD_GW + pr * LANES:g * SSD_GW + (pr + 1) * LANES]
                zero = jnp.zeros_like(xp)
                bd = jnp.concatenate([jnp.where(low_half, xp, zero), jnp.where(low_half, zero, xp)], axis=0)
                ydiag.append(jnp.dot(wg[:, pr * LANES:(pr + 1) * LANES], bd, preferred_element_type=F32))
            hp = h_ref[g]
            yoff = jnp.dot(cg, hp.astype(BF16), preferred_element_type=F32) * eacs[:, gs]
            st = jnp.dot(bg.T.astype(BF16), xw[:, gs], preferred_element_type=F32)
            h_ref[g] = hp * cdec[:, gs] + st
            y_ref[o:o + CHUNK, gs] = jnp.concatenate(ydiag, axis=1) + yoff + dx[:, gs] * xs[:, gs]

    xext[0:halo, :] = xext[lc:lc + halo, :]

    @pl.when(i == pl.num_programs(1) - 1)
    def _():
        for g in range(SSD_GROUPS):
            st_ref[g * SSD_GW:(g + 1) * SSD_GW, :] = h_ref[g].T


def _ssd_consts(a_log, d_skip):
    a = -jnp.exp(a_log.astype(F32))
    a_row = jnp.pad(a, (0, LANES - SSD_HEADS))[None, :]
    a_x = jnp.repeat(a, CHUNK)[None, :]
    d_x = jnp.repeat(d_skip.astype(F32), CHUNK)[None, :]
    emat = (jnp.arange(D_INNER)[None, :] // CHUNK == jnp.arange(LANES)[:, None]).astype(BF16)
    ltri = (jnp.arange(CHUNK)[:, None] >= jnp.arange(CHUNK)[None, :]).astype(BF16)
    return a_row, a_x, d_x, emat, ltri


def _ssd_prompt(rest, dt, conv_w, conv_b, a_row, d_x, emat, ltri, b, s, n_chunks=SSD_STEP_CHUNKS):
    lc = n_chunks * CHUNK
    rest_v = rest.reshape(b, s, REST_COLS)
    dt_v = dt.reshape(b, s, LANES)
    const = lambda shape: pl.BlockSpec(shape, lambda bb, i: (0,) * len(shape))
    body = functools.partial(_ssd_prompt_body, n_chunks=n_chunks)
    y, st = pl.pallas_call(
        body,
        out_shape=(jax.ShapeDtypeStruct((b, s, D_INNER), F32),
                   jax.ShapeDtypeStruct((b, D_INNER, D_STATE), F32)),
        grid=(b, s // lc),
        in_specs=[pl.BlockSpec((None, lc, CONV_CH), lambda bb, i: (bb, i, 0)),
                  pl.BlockSpec((None, lc, LANES), lambda bb, i: (bb, i, 0)),
                  const((CONV_W, CONV_CH)), const((1, CONV_CH)), const((1, LANES)),
                  const((1, D_INNER)), const((LANES, D_INNER)), const((CHUNK, CHUNK))],
        out_specs=(pl.BlockSpec((None, lc, D_INNER), lambda bb, i: (bb, i, 0)),
                   pl.BlockSpec((None, D_INNER, D_STATE), lambda bb, i: (bb, 0, 0))),
        scratch_shapes=[pltpu.VMEM((lc + SUBLANES, CONV_CH), F32),
                        pltpu.VMEM((SSD_GROUPS, D_STATE, SSD_GW), F32)],
        compiler_params=_params(("parallel", "arbitrary")),
        name="ssd_prompt",
    )(rest_v, dt_v, conv_w, conv_b, a_row, d_x, emat, ltri)
    return y.reshape(b * s, D_INNER), st


def _attn_step_body(q_ref, kn_ref, vn_ref, c0_ref, c1_ref, c2_ref, o_ref):
    caches = (c0_ref, c1_ref, c2_ref)
    hrow = lax.broadcasted_iota(jnp.int32, (N_HEADS, ATT_W), 0)
    hlane = lax.broadcasted_iota(jnp.int32, (N_HEADS, ATT_W), 1) // HEAD_DIM
    hmask = hrow == hlane
    s_list, sn_list = [], []
    for g, (win, dil) in enumerate(ATT_GROUPS):
        qbd = jnp.where(hmask, jnp.broadcast_to(q_ref[g:g + 1, :], (N_HEADS, ATT_W)), 0.0).astype(BF16)
        s = jnp.dot(qbd, caches[g][0].astype(BF16), preferred_element_type=F32)
        if dil > 1:
            wpos = lax.broadcasted_iota(jnp.int32, s.shape, 1)
            s = jnp.where((wpos & (dil - 1)) == 0, s, NEG)
        s_list.append(s)
        sn_list.append(jnp.sum(qbd.astype(F32) * kn_ref[g:g + 1, :].astype(BF16).astype(F32),
                               axis=1, keepdims=True))
    m = sn_list[0]
    for g in range(N_GROUPS):
        m = jnp.maximum(m, jnp.maximum(jnp.max(s_list[g], axis=1, keepdims=True), sn_list[g]))
    den = jnp.zeros((N_HEADS, 1), F32)
    acc = jnp.zeros((N_HEADS, ATT_W), F32)
    for g in range(N_GROUPS):
        e = jnp.exp(s_list[g] - m)
        en = jnp.exp(sn_list[g] - m)
        den = den + jnp.sum(e, axis=1, keepdims=True) + en
        acc = acc + _nt(e.astype(BF16), caches[g][1].astype(BF16))
        acc = acc + en.astype(BF16).astype(F32) * vn_ref[g:g + 1, :].astype(BF16).astype(F32)
    o_ref[...] = jnp.sum(jnp.where(hmask, acc / den, 0.0), axis=0, keepdims=True)


def _attn_step(q, kn, vn, caches):
    nb = q.shape[0]
    views, specs = [], []
    for (win, dil), c in zip(ATT_GROUPS, caches):
        w = c.shape[1]
        assert w == win, "cache must hold exactly one window of past rows"
        views.append(c.transpose(0, 2, 3, 4, 1).reshape(nb, 2, ATT_W, w))
        specs.append(pl.BlockSpec((None, 2, ATT_W, w), lambda i: (i, 0, 0, 0)))
    qspec = pl.BlockSpec((None, N_GROUPS, ATT_W), lambda i: (i, 0, 0))
    out = pl.pallas_call(
        _attn_step_body,
        out_shape=jax.ShapeDtypeStruct((nb, 1, ATT_W), F32),
        grid=(nb,),
        in_specs=[qspec, qspec, qspec, *specs],
        out_specs=pl.BlockSpec((None, 1, ATT_W), lambda i: (i, 0, 0)),
        compiler_params=_params(("parallel",)),
        name="attn_step",
    )(q, kn, vn, *views)
    return out.reshape(nb, ATT_W)


def _conv_step_body(rest_ref, sc_ref, dt_ref, cw_ref, cb_ref, ax_ref, e_ref,
                    xs_ref, bm_ref, cm_ref, xdt_ref, dec_ref):
    conv = cb_ref[...] + sc_ref[:, 0:CONV_CH] * cw_ref[0:1, :]
    conv = conv + sc_ref[:, CONV_CH:2 * CONV_CH] * cw_ref[1:2, :]
    conv = conv + sc_ref[:, 2 * CONV_CH:3 * CONV_CH] * cw_ref[2:3, :]
    conv = conv + rest_ref[...] * cw_ref[3:4, :]
    xc = conv * jax.nn.sigmoid(conv)
    xs = xc[:, :D_INNER]
    dtx = _expand_heads(dt_ref[...], e_ref[...])
    xs_ref[...] = xs
    bm_ref[...] = xc[:, D_INNER:D_INNER + SSD_GROUPS * D_STATE]
    cm_ref[...] = xc[:, D_INNER + SSD_GROUPS * D_STATE:]
    xdt_ref[...] = xs * dtx
    dec_ref[...] = jnp.exp(dtx * ax_ref[...])


def _state_step_body(h_ref, xdt_ref, dec_ref, b_ref, c_ref, xs_ref, dx_ref, ho_ref, y_ref, *, bb):
    grow = lax.broadcasted_iota(jnp.int32, (SSD_GROUPS, D_INNER), 0)
    glane = lax.broadcasted_iota(jnp.int32, (SSD_GROUPS, D_INNER), 1) // SSD_GW
    gmask = grow == glane
    for bi in range(bb):
        h = h_ref[bi]
        bmat = b_ref[bi]
        bx = jnp.concatenate([jnp.broadcast_to(bmat[g:g + 1, :], (SSD_GW, D_STATE))
                              for g in range(SSD_GROUPS)], axis=0)
        hn = h * dec_ref[:, bi:bi + 1] + xdt_ref[:, bi:bi + 1] * bx
        ho_ref[bi] = hn
        y8 = _nt(c_ref[bi].astype(BF16), hn.astype(BF16))
        y = jnp.sum(jnp.where(gmask, y8, 0.0), axis=0, keepdims=True)
        y_ref[bi:bi + 1, :] = y + dx_ref[...] * xs_ref[bi:bi + 1, :]


def _ssd_step(rest, dt, state_conv, state_ssm, conv_w, conv_b, a_x, d_x, emat, bb=4):
    nb = rest.shape[0]
    nblk = nb // bb
    full = lambda shape: pl.BlockSpec(shape, lambda i: (0,) * len(shape))
    xs, bm, cm, xdt, dec = pl.pallas_call(
        _conv_step_body,
        out_shape=(jax.ShapeDtypeStruct((nb, D_INNER), F32),
                   jax.ShapeDtypeStruct((nb, SSD_GROUPS * D_STATE), F32),
                   jax.ShapeDtypeStruct((nb, SSD_GROUPS * D_STATE), F32),
                   jax.ShapeDtypeStruct((nb, D_INNER), F32),
                   jax.ShapeDtypeStruct((nb, D_INNER), F32)),
        grid=(1,),
        in_specs=[pl.BlockSpec((nb, CONV_CH), lambda i: (0, 0)),
                  full((nb, (CONV_W - 1) * CONV_CH)), full((nb, LANES)),
                  full((CONV_W, CONV_CH)), full((1, CONV_CH)), full((1, D_INNER)),
                  full((LANES, D_INNER))],
        out_specs=(full((nb, D_INNER)), full((nb, SSD_GROUPS * D_STATE)),
                   full((nb, SSD_GROUPS * D_STATE)), full((nb, D_INNER)), full((nb, D_INNER))),
        compiler_params=_params(("arbitrary",)),
        name="conv_step",
    )(rest, state_conv.reshape(nb, (CONV_W - 1) * CONV_CH), dt, conv_w, conv_b, a_x, emat)

    to_cols = lambda a: a.reshape(nblk, bb, D_INNER).transpose(0, 2, 1)
    blk3 = lambda d1, d2: pl.BlockSpec((None, d1, d2), lambda i: (i, 0, 0))
    hspec = pl.BlockSpec((bb, D_INNER, D_STATE), lambda i: (i, 0, 0))
    gspec = pl.BlockSpec((bb, SSD_GROUPS, D_STATE), lambda i: (i, 0, 0))
    h_new, y = pl.pallas_call(
        functools.partial(_state_step_body, bb=bb),
        out_shape=(jax.ShapeDtypeStruct((nb, D_INNER, D_STATE), F32),
                   jax.ShapeDtypeStruct((nblk, bb, D_INNER), F32)),
        grid=(nblk,),
        in_specs=[hspec, blk3(D_INNER, bb), blk3(D_INNER, bb), gspec, gspec, blk3(bb, D_INNER),
                  pl.BlockSpec((1, D_INNER), lambda i: (0, 0))],
        out_specs=(hspec, blk3(bb, D_INNER)),
        compiler_params=_params(("parallel",)),
        name="state_step",
    )(state_ssm.reshape(nb, D_INNER, D_STATE), to_cols(xdt), to_cols(dec),
      bm.reshape(nb, SSD_GROUPS, D_STATE), cm.reshape(nb, SSD_GROUPS, D_STATE),
      xs.reshape(nblk, bb, D_INNER), d_x)
    return y.reshape(nb, D_INNER), h_new


def _merge_body(x_ref, att_ref, y_ref, z_ref, ga_ref, gb_ref, gn_ref, wa_ref, wb_ref, wo_ref, o_ref):
    att = jnp.concatenate([att_ref[c] for c in range(ATT_SLABS)], axis=1)
    out_a = jnp.dot(att.astype(BF16), wa_ref[...], preferred_element_type=F32)
    z = z_ref[...]
    y = y_ref[...] * (z * jax.nn.sigmoid(z))
    yn = _rmsnorm_rows(y, gn_ref[...]).astype(BF16)
    out_b = jnp.dot(yn, wb_ref[...], preferred_element_type=F32)
    merged = jax.nn.sigmoid(ga_ref[...]) * out_a + jax.nn.sigmoid(gb_ref[...]) * out_b
    o_ref[...] = x_ref[...] + jnp.dot(merged.astype(BF16), wo_ref[...], preferred_element_type=F32)


def _merge(x2d, att, y, rest, ssd_norm, w_att_out, w_ssd_out, w_out, tm=256):
    t = x2d.shape[0]
    tm = min(tm, t)
    row = lambda w, cb: pl.BlockSpec((tm, w), lambda i: (i, cb))
    full = lambda shape: pl.BlockSpec(shape, lambda i: (0, 0))
    return pl.pallas_call(
        _merge_body,
        out_shape=jax.ShapeDtypeStruct((t, D_MODEL), F32),
        grid=(t // tm,),
        in_specs=[row(D_MODEL, 0),
                  pl.BlockSpec((ATT_SLABS, tm, LANES), lambda i: (0, i, 0)),
                  row(D_INNER, 0),
                  row(D_INNER, REST_Z // D_INNER),
                  row(D_MODEL, REST_GATES // D_MODEL), row(D_MODEL, REST_GATES // D_MODEL + 1),
                  full((1, D_INNER)), full((ATT_W, D_MODEL)), full((D_INNER, D_MODEL)),
                  full((D_MODEL, D_MODEL))],
        out_specs=row(D_MODEL, 0),
        compiler_params=_params(("parallel",)),
        name="merge",
    )(x2d, att, y, rest, rest, rest, ssd_norm, w_att_out, w_ssd_out, w_out)


I1_BLOCK = SUBLANES
BF16_ROWS = 2 * SUBLANES
PEER_GATE_TT = 512
PEER_DENSE_TT = 1024
PEER_DENSE_VMEM = VMEM_BYTES_V7X * 7 // 8


def _gelu_tanh(x):
    return 0.5 * x * (1.0 + jnp.tanh(math.sqrt(2.0 / math.pi) * (x + 0.044715 * (x * x * x))))


def _peer_gate_body(x_ref, g_ref, wq_ref, k1_ref, k2_ref,
                    xnt_ref, r2_ref, e2_ref, n1_ref, w1_ref, s_all, v_all, rk1, *, tg):
    nch = tg // LANES
    nkb = N_KEYS // I1_BLOCK
    xn = _rmsnorm_rows(x_ref[...], g_ref[...])
    xnt = xn.T.astype(BF16)
    xnt_ref[...] = xnt
    qt = jnp.dot(wq_ref[...], xnt, preferred_element_type=F32)
    for h in range(PEER_HEADS):
        for sd, kref in enumerate((k1_ref, k2_ref)):
            r0 = (2 * h + sd) * N_KEYS
            s_all[2 * h + sd] = jnp.dot(kref[h], qt[r0:r0 + N_KEYS, :].astype(BF16),
                                        preferred_element_type=F32)

    rowid = lax.broadcasted_iota(jnp.int32, (N_KEYS, LANES), 0).astype(F32)
    row16 = lax.broadcasted_iota(jnp.int32, (PEER_TOPK, LANES), 0)
    row16f = row16.astype(F32)

    def make_extract(exact):
        def extract(idx, ties):
            h = idx // nch
            off = pl.multiple_of((idx % nch) * LANES, LANES)
            s = [s_all[2 * h + sd, :, pl.ds(off, LANES)] for sd in range(2)]
            rank = [jnp.full((N_KEYS, LANES), float(PEER_TOPK), F32) for _ in range(2)]
            vals = [jnp.zeros((PEER_TOPK, LANES), F32) for _ in range(2)]
            for k in range(PEER_TOPK):
                for sd in range(2):
                    m = jnp.max(s[sd], axis=0, keepdims=True)
                    sel = s[sd] == m
                    if exact:
                        first = jnp.min(jnp.where(sel, rowid, float(N_KEYS)), axis=0, keepdims=True)
                        sel = rowid == first
                    rank[sd] = jnp.where(sel, float(k), rank[sd])
                    s[sd] = jnp.where(sel, -jnp.inf, s[sd])
                    vals[sd] = jnp.where(row16 == k, m, vals[sd])
            rk1[h, :, pl.ds(off, LANES)] = rank[0]
            r2_ref[h, :, :, pl.ds(off, LANES)] = rank[1].reshape(N_KEYS // BF16_ROWS, BF16_ROWS, LANES).astype(BF16)
            v_all[2 * h, :, pl.ds(off, LANES)] = vals[0]
            v_all[2 * h + 1, :, pl.ds(off, LANES)] = vals[1]
            if not exact:
                for sd in range(2):
                    taken = jnp.sum(jnp.where(rank[sd] < float(PEER_TOPK), 1.0, 0.0), axis=0, keepdims=True)
                    ties = jnp.maximum(ties, jnp.where(taken != float(PEER_TOPK), 1.0, 0.0))
            return ties
        return extract

    no_ties = jnp.zeros((1, LANES), F32)
    ties = lax.fori_loop(0, PEER_HEADS * nch, make_extract(False), no_ties)

    @pl.when(jnp.max(ties) > 0.0)
    def _():
        lax.fori_loop(0, PEER_HEADS * nch, make_extract(True), no_ties)

    def finish(h, carry):
        for c in range(nch):
            lanes = slice(c * LANES, (c + 1) * LANES)
            v1 = v_all[2 * h, :, lanes]
            v2 = v_all[2 * h + 1, :, lanes]
            shifted = [jnp.broadcast_to(v2[b:b + 1, :], (PEER_TOPK, LANES)) for b in range(PEER_TOPK)]
            cnt = jnp.zeros((PEER_TOPK, LANES), F32)
            zsum = jnp.zeros((1, LANES), F32)
            top = v1[0:1, :] + v2[0:1, :]
            for step in range(PEER_TOPK):
                front = v1 + shifted[0]
                m = jnp.max(front, axis=0, keepdims=True)
                first = jnp.min(jnp.where(front == m, row16f, float(PEER_TOPK)), axis=0, keepdims=True)
                sel = row16f == first
                cnt = cnt + jnp.where(sel, 1.0, 0.0)
                zsum = zsum + jnp.exp(m - top)
                live = PEER_TOPK - 1 - step
                for b in range(live):
                    shifted[b] = jnp.where(sel, shifted[b + 1], shifted[b])
            rz = 1.0 / zsum
            r1 = rk1[h, :, lanes]
            s1 = s_all[2 * h, :, lanes]
            s2 = s_all[2 * h + 1, :, lanes]
            n_of = jnp.zeros((N_KEYS, LANES), F32)
            for a in range(PEER_TOPK):
                n_of = jnp.where(r1 == float(a), cnt[a:a + 1, :], n_of)
            wgt = jnp.exp(s1 - v1[0:1, :]) * rz
            e2v = jnp.exp(s2 - v2[0:1, :])
            e2_ref[h, :, :, lanes] = e2v.reshape(N_KEYS // BF16_ROWS, BF16_ROWS, LANES).astype(BF16)
            for kb in range(nkb):
                rs = slice(kb * I1_BLOCK, (kb + 1) * I1_BLOCK)
                n1_ref[kb, h, :, lanes] = n_of[rs, :]
                w1_ref[kb, h, :, lanes] = wgt[rs, :]
        return carry

    lax.fori_loop(0, PEER_HEADS, finish, 0)


def _peer_dense_body(x_ref, xnt_ref, r2_ref, e2_ref, n1_ref, w1_ref, u_ref, vt_ref, o_ref,
                     acct, act, hbuf, *, tt):
    j = pl.program_id(1)
    nch = tt // LANES
    half = I1_BLOCK // 2 * N_KEYS
    ktiles = N_KEYS // BF16_ROWS

    @pl.when(j == 0)
    def _():
        acct[...] = jnp.zeros_like(acct)

    xnt = xnt_ref[...]
    for hf in range(2):
        act[hf * half:(hf + 1) * half, :] = jnp.dot(u_ref[hf * half:(hf + 1) * half, :], xnt,
                                                    preferred_element_type=F32)
    for hf in range(2):
        for ii in range(hf * I1_BLOCK // 2, (hf + 1) * I1_BLOCK // 2):
            for c in range(nch):
                lanes = slice(c * LANES, (c + 1) * LANES)
                gate = None
                for h in range(PEER_HEADS):
                    n16 = jnp.broadcast_to(n1_ref[h, ii:ii + 1, lanes], (BF16_ROWS, LANES)).astype(BF16)
                    w16 = jnp.broadcast_to(w1_ref[h, ii:ii + 1, lanes], (BF16_ROWS, LANES)).astype(BF16)
                    prod = e2_ref[h, :, :, lanes] * w16[None]
                    term = jnp.where(r2_ref[h, :, :, lanes] < n16[None], prod, jnp.zeros_like(prod))
                    gate = term if gate is None else gate + term
                a = act[ii * N_KEYS:(ii + 1) * N_KEYS, lanes]
                hv = gate.astype(F32).reshape(N_KEYS, LANES) * _gelu_tanh(a)
                hbuf[ii * N_KEYS:(ii + 1) * N_KEYS, lanes] = hv.astype(BF16)
        acct[...] += jnp.dot(vt_ref[:, hf * half:(hf + 1) * half], hbuf[hf * half:(hf + 1) * half, :],
                             preferred_element_type=F32)

    @pl.when(j == pl.num_programs(1) - 1)
    def _():
        o_ref[...] = x_ref[...] + acct[...].T


def _peer(x2d, norm_ffn, wq_t, keys1, keys2, u_bf, vt_bf):
    t = x2d.shape[0]
    tg = min(PEER_GATE_TT, t)
    tt = min(PEER_DENSE_TT, t)
    nkb = N_KEYS // I1_BLOCK
    eb = I1_BLOCK * N_KEYS
    ktiles = N_KEYS // BF16_ROWS
    full1 = lambda shape: pl.BlockSpec(shape, lambda i: (0,) * len(shape))
    tab_shape = jax.ShapeDtypeStruct((PEER_HEADS, ktiles, BF16_ROWS, t), BF16)
    row_shape = jax.ShapeDtypeStruct((nkb, PEER_HEADS, I1_BLOCK, t), F32)
    tab_spec1 = pl.BlockSpec((PEER_HEADS, ktiles, BF16_ROWS, tg), lambda i: (0, 0, 0, i))
    row_spec1 = pl.BlockSpec((nkb, PEER_HEADS, I1_BLOCK, tg), lambda i: (0, 0, 0, i))
    xnt, r2, e2, n1, w1 = pl.pallas_call(
        functools.partial(_peer_gate_body, tg=tg),
        out_shape=(jax.ShapeDtypeStruct((D_MODEL, t), BF16), tab_shape, tab_shape, row_shape, row_shape),
        grid=(t // tg,),
        in_specs=[pl.BlockSpec((tg, D_MODEL), lambda i: (i, 0)),
                  full1((1, D_MODEL)), full1((2 * PEER_HEADS * N_KEYS, D_MODEL)),
                  full1((PEER_HEADS, N_KEYS, N_KEYS)), full1((PEER_HEADS, N_KEYS, N_KEYS))],
        out_specs=(pl.BlockSpec((D_MODEL, tg), lambda i: (0, i)), tab_spec1, tab_spec1, row_spec1, row_spec1),
        scratch_shapes=[pltpu.VMEM((2 * PEER_HEADS, N_KEYS, tg), F32),
                        pltpu.VMEM((2 * PEER_HEADS, PEER_TOPK, tg), F32),
                        pltpu.VMEM((PEER_HEADS, N_KEYS, tg), F32)],
        compiler_params=_params(("parallel",)),
        name="peer_gate",
    )(x2d, norm_ffn, wq_t, keys1, keys2)

    tab_spec = pl.BlockSpec((PEER_HEADS, ktiles, BF16_ROWS, tt), lambda i, j: (0, 0, 0, i))
    row_spec = pl.BlockSpec((None, PEER_HEADS, I1_BLOCK, tt), lambda i, j: (j, 0, 0, i))
    return pl.pallas_call(
        functools.partial(_peer_dense_body, tt=tt),
        out_shape=jax.ShapeDtypeStruct((t, D_MODEL), F32),
        grid=(t // tt, nkb),
        in_specs=[pl.BlockSpec((tt, D_MODEL), lambda i, j: (i, 0)),
                  pl.BlockSpec((D_MODEL, tt), lambda i, j: (0, i)),
                  tab_spec, tab_spec, row_spec, row_spec,
                  pl.BlockSpec((eb, D_MODEL), lambda i, j: (j, 0)),
                  pl.BlockSpec((D_MODEL, eb), lambda i, j: (0, j))],
        out_specs=pl.BlockSpec((tt, D_MODEL), lambda i, j: (i, 0)),
        scratch_shapes=[pltpu.VMEM((D_MODEL, tt), F32),
                        pltpu.VMEM((eb, tt), F32),
                        pltpu.VMEM((eb, tt), BF16)],
        compiler_params=pltpu.CompilerParams(dimension_semantics=("parallel", "arbitrary"),
                                             vmem_limit_bytes=PEER_DENSE_VMEM),
        name="peer_dense",
    )(x2d, xnt, r2, e2, n1, w1, u_bf, vt_bf)


def _ple_body(x_ref, p_ref, gp_ref, wg_ref, wp_ref, gf_ref, o_ref):
    x = x_ref[...]
    xn = _rmsnorm_rows(x, gp_ref[...]).astype(BF16)
    pg = jax.nn.sigmoid(jnp.dot(xn, wg_ref[...], preferred_element_type=F32))
    x3 = x + pg * jnp.dot(p_ref[...].astype(BF16), wp_ref[...], preferred_element_type=F32)
    o_ref[...] = _rmsnorm_rows(x3, gf_ref[...])


def _ple_final(x2d, p2d, norm_ple, w_gate, w_proj, norm_final, tm=512):
    t = x2d.shape[0]
    tm = min(tm, t)
    full = lambda shape: pl.BlockSpec(shape, lambda i: (0, 0))
    return pl.pallas_call(
        _ple_body,
        out_shape=jax.ShapeDtypeStruct((t, D_MODEL), F32),
        grid=(t // tm,),
        in_specs=[pl.BlockSpec((tm, D_MODEL), lambda i: (i, 0)),
                  pl.BlockSpec((tm, PLE_DIM), lambda i: (i, 0)),
                  full((1, D_MODEL)), full((D_MODEL, D_MODEL)), full((PLE_DIM, D_MODEL)),
                  full((1, D_MODEL))],
        out_specs=pl.BlockSpec((tm, D_MODEL), lambda i: (i, 0)),
        compiler_params=_params(("parallel",)),
        name="ple_final",
    )(x2d, p2d, norm_ple, w_gate, w_proj, norm_final)


def _prep_weights(norm_mix, w_in, conv_w, conv_b, dt_bias, a_log, d_skip, ssd_norm, w_att_out,
                  w_ssd_out, w_out, norm_ffn, w_peer_q, peer_keys1, peer_keys2, peer_u, peer_v,
                  norm_ple, w_ple_gate, w_ple_proj, norm_final):
    w = w_in[0]
    c = [0, Q_COLS, 2 * Q_COLS, 3 * Q_COLS, 3 * Q_COLS + D_INNER, 3 * Q_COLS + D_INNER + CONV_CH,
         3 * Q_COLS + D_INNER + CONV_CH + SSD_HEADS]
    wq, wk, wv, wz, wxbc, wdt, wgt = (w[:, c[0]:c[1]], w[:, c[1]:c[2]], w[:, c[2]:c[3]], w[:, c[3]:c[4]],
                                      w[:, c[4]:c[5]], w[:, c[5]:c[6]], w[:, c[6]:])
    row = lambda v: v.reshape(1, -1).astype(F32)
    p = {}
    p["norm_mix"] = row(norm_mix[0])
    p["w_qkv"] = jnp.concatenate([_split_halves(wq * (HEAD_DIM ** -0.5)), _split_halves(wk), wv],
                                 axis=1).astype(BF16)
    p["w_rest"] = jnp.concatenate([wxbc, wz, wgt], axis=1).astype(BF16)
    p["w_dt"] = jnp.pad(wdt, ((0, 0), (0, LANES - SSD_HEADS))).astype(BF16)
    p["dt_bias"] = jnp.pad(row(dt_bias[0]), ((0, 0), (0, LANES - SSD_HEADS)))
    p["conv_w"] = conv_w[0].astype(F32)
    p["conv_b"] = row(conv_b[0])
    p["a_row"], p["a_x"], p["d_x"], p["emat"], p["ltri"] = _ssd_consts(a_log[0], d_skip[0])
    p["ssd_norm"] = row(ssd_norm[0])
    p["w_att_out"] = w_att_out[0].astype(BF16)
    p["w_ssd_out"] = w_ssd_out[0].astype(BF16)
    p["w_out"] = w_out[0].astype(BF16)
    p["norm_ffn"] = row(norm_ffn[0])
    p["wq_t"] = w_peer_q[0].T.astype(BF16)
    p["keys1"] = peer_keys1[0].astype(BF16)
    p["keys2"] = peer_keys2[0].astype(BF16)
    p["u"] = peer_u[0].astype(BF16)
    p["vt"] = peer_v[0].T.astype(BF16)
    p["norm_ple"] = row(norm_ple[0])
    p["w_ple_gate"] = w_ple_gate[0].astype(BF16)
    p["w_ple_proj"] = w_ple_proj[0].astype(BF16)
    p["norm_final"] = row(norm_final)
    return p


def _split_halves(w):
    k, n = w.shape
    return w.reshape(k, n // LANES, 2, 2, HEAD_DIM // 2).transpose(0, 1, 3, 2, 4).reshape(k, n)


def _tail(x2d, att, y, rest, p2d, p):
    x1 = _merge(x2d, att, y, rest, p["ssd_norm"], p["w_att_out"], p["w_ssd_out"], p["w_out"])
    x2 = _peer(x1, p["norm_ffn"], p["wq_t"], p["keys1"], p["keys2"], p["u"], p["vt"])
    return _ple_final(x2, p2d, p["norm_ple"], p["w_ple_gate"], p["w_ple_proj"], p["norm_final"])


def _heads_from_slabs(qkv, which, g, b, s, rows):
    s0 = which * (Q_COLS // LANES) + g * ATT_SLABS
    x = qkv.reshape(QKV_COLS // LANES, b, s, LANES)[s0:s0 + ATT_SLABS, :, s - rows:]
    x = x.transpose(1, 2, 0, 3)
    if which < 2:
        x = x.reshape(b, rows, ATT_SLABS, 2, 2, HEAD_DIM // 2).transpose(0, 1, 2, 4, 3, 5)
    return x.reshape(b, rows, N_HEADS, HEAD_DIM)


def _kv_rows(qkv, g, b, s, rows):
    return jnp.stack([_heads_from_slabs(qkv, 1, g, b, s, rows),
                      _heads_from_slabs(qkv, 2, g, b, s, rows)], axis=2)[None]


def _prompt(x, p_in, p):
    b, s, _ = x.shape
    x2d = x.reshape(b * s, D_MODEL)
    qkv, rest, dt = _projections(x2d, jnp.arange(s, dtype=jnp.int32), p["norm_mix"], p["w_qkv"],
                                 p["w_rest"], p["w_dt"], p["dt_bias"], tm=PROJ_TM)
    att = _attn_prompt(qkv, b, s)
    y, ssm = _ssd_prompt(rest, dt, p["conv_w"], p["conv_b"], p["a_row"], p["d_x"], p["emat"], p["ltri"], b, s)
    out = _tail(x2d, att, y, rest, p_in[0].reshape(b * s, PLE_DIM), p)
    kvs = [_kv_rows(qkv, g, b, s, min(win, s)) for g, (win, _) in enumerate(ATT_GROUPS)]
    conv = rest.reshape(b, s, REST_COLS)[:, s - (CONV_W - 1):, :CONV_CH][None]
    ssm = ssm.reshape(1, b, SSD_HEADS, CHUNK, D_STATE)
    return out.reshape(b, s, D_MODEL), kvs, conv, ssm


def _sample(x, p_in, caches, state_conv, state_ssm, p):
    b, s, _ = x.shape
    assert s == 1
    x2d = x.reshape(b, D_MODEL)
    pos = jnp.full((b,), PAST_LEN, dtype=jnp.int32)
    qkv, rest, dt = _projections(x2d, pos, p["norm_mix"], p["w_qkv"], p["w_rest"], p["w_dt"],
                                 p["dt_bias"], tm=b)
    rows = lambda which: jnp.stack([_heads_from_slabs(qkv, which, g, b, 1, 1).reshape(b, ATT_W)
                                    for g in range(N_GROUPS)], axis=1)
    q3, k3, v3 = rows(0), rows(1), rows(2)
    att = _attn_step(q3, k3, v3, [c[0] for c in caches])
    att = att.reshape(b, ATT_SLABS, LANES).transpose(1, 0, 2)
    xbc = rest[:, :CONV_CH]
    y, ssm = _ssd_step(xbc, dt, state_conv[0], state_ssm[0],
                       p["conv_w"], p["conv_b"], p["a_x"], p["d_x"], p["emat"])
    out = _tail(x2d, att, y, rest, p_in[0].reshape(b, PLE_DIM), p)
    hd = lambda a, g: a[:, g].reshape(b, N_HEADS, HEAD_DIM)
    kvs = [jnp.stack([hd(k3, g), hd(v3, g)], axis=1)[None, :, None] for g in range(N_GROUPS)]
    conv = jnp.concatenate([state_conv[0][:, 1:], xbc[:, None, :]], axis=1)[None]
    ssm = ssm.reshape(1, b, SSD_HEADS, CHUNK, D_STATE)
    return out.reshape(b, 1, D_MODEL), kvs, conv, ssm


def kernel(x_prompt, x_sample, cache_kv_w128, cache_kv_w512, cache_kv_w2048, state_conv, state_ssm, p_prompt, p_sample, norm_mix, w_in, conv_w, conv_b, dt_bias, a_log, d_skip, ssd_norm, w_att_out, w_ssd_out, w_out, norm_ffn, w_peer_q, peer_keys1, peer_keys2, peer_u, peer_v, norm_ple, w_ple_gate, w_ple_proj, norm_final):
    p = _prep_weights(norm_mix, w_in, conv_w, conv_b, dt_bias, a_log, d_skip, ssd_norm, w_att_out,
                      w_ssd_out, w_out, norm_ffn, w_peer_q, peer_keys1, peer_keys2, peer_u, peer_v,
                      norm_ple, w_ple_gate, w_ple_proj, norm_final)
    y_p, kv_p, conv_p, ssm_p = _prompt(x_prompt, p_prompt, p)
    y_s, kv_s, conv_s, ssm_s = _sample(x_sample, p_sample, (cache_kv_w128, cache_kv_w512, cache_kv_w2048),
                                       state_conv, state_ssm, p)
    return (y_p, y_s, kv_p[0], kv_p[1], kv_p[2], conv_p, ssm_p, kv_s[0], kv_s[1], kv_s[2], conv_s, ssm_s)
```

```python
import functools
import math

import jax
import jax.numpy as jnp
from jax import lax
from jax.experimental import pallas as pl
from jax.experimental.pallas import tpu as pltpu

F32 = jnp.float32
BF16 = jnp.bfloat16

LANES = 128
SUBLANES = 8
VMEM_BYTES_V7X = 64 * 1024 * 1024
VMEM_LIMIT = VMEM_BYTES_V7X * 3 // 4

D_MODEL = 1024
ATT_GROUPS = ((128, 1), (512, 4), (2048, 16))
N_GROUPS = len(ATT_GROUPS)
N_HEADS = 8
HEAD_DIM = 64
ATT_W = N_HEADS * HEAD_DIM
ATT_SLABS = ATT_W // LANES
Q_COLS = N_GROUPS * ATT_W
QKV_COLS = 3 * Q_COLS
ROPE_THETA = 10000.0
D_INNER = 2048
SSD_HEADS = 32
SSD_GROUPS = 8
SSD_GW = D_INNER // SSD_GROUPS
D_STATE = 128
CONV_W = 4
CONV_CH = D_INNER + 2 * SSD_GROUPS * D_STATE
CHUNK = 64
N_KEYS = 128
N_EXPERTS = N_KEYS * N_KEYS
PEER_HEADS = 8
PEER_TOPK = 16
PLE_DIM = 256
EPS = 1e-6
PAST_LEN = 8192
NEG = -1e30

REST_Z = CONV_CH
REST_GATES = REST_Z + D_INNER
REST_COLS = REST_GATES + 2 * D_MODEL
QKV_TN = 1536
REST_TN = 2048
PROJ_TM = 1024
ATT_ROWS = 4096
ATT_VMEM = VMEM_BYTES_V7X * 7 // 8
ATT_ILP = 8
SSD_STEP_CHUNKS = 4


def _params(sem):
    return pltpu.CompilerParams(dimension_semantics=sem, vmem_limit_bytes=VMEM_LIMIT)


def _rmsnorm_rows(x, g):
    return x * lax.rsqrt(jnp.mean(x * x, axis=-1, keepdims=True) + EPS) * g


def _nt(a, b):
    return lax.dot_general(a, b, (((1,), (1,)), ((), ())), preferred_element_type=F32)


def _proj_qkv_body(x_ref, g_ref, w_ref, cos_ref, sin_ref, o_ref, xn_ref, *, rope_tiles):
    j = pl.program_id(1)

    @pl.when(j == 0)
    def _():
        xn_ref[...] = _rmsnorm_rows(x_ref[...], g_ref[...]).astype(BF16)

    acc = jnp.dot(xn_ref[...], w_ref[...], preferred_element_type=F32)
    n_slab = acc.shape[1] // LANES

    @pl.when(j < rope_tiles)
    def _():
        cos = cos_ref[...]
        sin = sin_ref[...]
        for c in range(n_slab):
            a = acc[:, c * LANES:(c + 1) * LANES]
            o_ref[c] = a * cos + pltpu.roll(a, LANES // 2, 1) * sin

    @pl.when(j >= rope_tiles)
    def _():
        for c in range(n_slab):
            o_ref[c] = acc[:, c * LANES:(c + 1) * LANES]


def _proj_rest_body(x_ref, g_ref, w_ref, wdt_ref, bias_ref, o_ref, dt_ref, xn_ref):
    @pl.when(pl.program_id(1) == 0)
    def _():
        xn = _rmsnorm_rows(x_ref[...], g_ref[...]).astype(BF16)
        xn_ref[...] = xn
        v = jnp.dot(xn, wdt_ref[...], preferred_element_type=F32) + bias_ref[...]
        dt_ref[...] = jnp.maximum(v, 0.0) + jnp.log1p(jnp.exp(-jnp.abs(v)))

    o_ref[...] = jnp.dot(xn_ref[...], w_ref[...], preferred_element_type=F32)


def _rope_tables(pos):
    half = HEAD_DIM // 2
    inv = jnp.exp(jnp.arange(half, dtype=F32) * (-2.0 * math.log(ROPE_THETA) / HEAD_DIM))
    ang = pos.astype(F32)[:, None] * inv[None, :]
    cos, sin = jnp.cos(ang), jnp.sin(ang)
    cos_t = jnp.concatenate([cos, cos, cos, cos], axis=1)
    sin_t = jnp.concatenate([-sin, -sin, sin, sin], axis=1)
    return cos_t, sin_t


def _projections(x2d, pos, norm_mix, w_qkv, w_rest, w_dt, dt_bias_row, tm):
    t, k = x2d.shape
    tm = min(tm, t)
    cos_t, sin_t = _rope_tables(pos)
    n_pos_blocks = pos.shape[0] // tm
    x_spec = pl.BlockSpec((tm, k), lambda i, j: (i, 0))
    g_spec = pl.BlockSpec((1, k), lambda i, j: (0, 0))
    tab_spec = pl.BlockSpec((tm, LANES), lambda i, j: (i % n_pos_blocks, 0))
    slabs = QKV_TN // LANES
    qkv = pl.pallas_call(
        functools.partial(_proj_qkv_body, rope_tiles=2 * Q_COLS // QKV_TN),
        out_shape=jax.ShapeDtypeStruct((QKV_COLS // LANES, t, LANES), F32),
        grid=(t // tm, QKV_COLS // QKV_TN),
        in_specs=[x_spec, g_spec, pl.BlockSpec((k, QKV_TN), lambda i, j: (0, j)), tab_spec, tab_spec],
        out_specs=pl.BlockSpec((slabs, tm, LANES), lambda i, j: (j, i, 0)),
        scratch_shapes=[pltpu.VMEM((tm, k), BF16)],
        compiler_params=_params(("parallel", "arbitrary")),
        name="proj_qkv",
    )(x2d, norm_mix, w_qkv, cos_t, sin_t)
    rest, dt = pl.pallas_call(
        _proj_rest_body,
        out_shape=(jax.ShapeDtypeStruct((t, REST_COLS), F32), jax.ShapeDtypeStruct((t, LANES), F32)),
        grid=(t // tm, REST_COLS // REST_TN),
        in_specs=[x_spec, g_spec, pl.BlockSpec((k, REST_TN), lambda i, j: (0, j)),
                  pl.BlockSpec((k, LANES), lambda i, j: (0, 0)), pl.BlockSpec((1, LANES), lambda i, j: (0, 0))],
        out_specs=(pl.BlockSpec((tm, REST_TN), lambda i, j: (i, j)),
                   pl.BlockSpec((tm, LANES), lambda i, j: (i, 0))),
        scratch_shapes=[pltpu.VMEM((tm, k), BF16)],
        compiler_params=_params(("parallel", "arbitrary")),
        name="proj_rest",
    )(x2d, norm_mix, w_rest, w_dt, dt_bias_row)
    return qkv, rest, dt


def _attn_prompt_body(*refs):
    n_in = 3 * N_GROUPS
    in_refs = refs[:n_in]
    o_ref = refs[n_in]
    kbuf, vbuf, lbuf = refs[n_in + 1:n_in + 4]
    rows = o_ref.shape[0]

    krow = lax.broadcasted_iota(jnp.int32, (2 * LANES, 2 * LANES), 0)
    qidx = lax.broadcasted_iota(jnp.int32, (2 * LANES, 2 * LANES), 1) & (LANES - 1)
    own_ok = jnp.logical_and(krow >= LANES, krow - LANES <= qidx)
    prev_ok = jnp.logical_and(krow < LANES, krow >= qidx)
    low_half = (lax.broadcasted_iota(jnp.int32, (LANES, LANES), 1) & (HEAD_DIM // 2)) == 0

    for g, (_, dil) in enumerate(ATT_GROUPS):
        q_ref, k_ref, v_ref = in_refs[3 * g:3 * g + 3]
        span = LANES * dil
        kbuf[0:span, :] = jnp.zeros((span, LANES), F32)
        kbuf[span:span + rows, :] = k_ref[...]
        vbuf[0:span, :] = jnp.zeros((span, LANES), F32)
        vbuf[span:span + rows, :] = v_ref[...]

        def rd(ref, start, dil=dil):
            if dil == 1:
                return ref[pl.ds(start, LANES), :]
            return ref[pl.ds(start, LANES, stride=dil), :]

        def wr(ref, start, val, dil=dil):
            if dil == 1:
                ref[pl.ds(start, LANES), :] = val
            else:
                ref[pl.ds(start, LANES, stride=dil), :] = val

        def sub_blocks(p, carry, g=g, dil=dil, span=span, q_ref=q_ref, rd=rd, wr=wr):
            us = [p * ATT_ILP + k for k in range(ATT_ILP)]
            sps = [u // dil for u in us]
            starts = [sp * span + (u % dil) for u, sp in zip(us, sps)]
            scs, vsts = [], []
            for sp, start in zip(sps, starts):
                q2 = rd(q_ref, start)
                qs = jnp.concatenate([jnp.where(low_half, q2, 0.0), jnp.where(low_half, 0.0, q2)],
                                     axis=0).astype(BF16)
                ks = jnp.concatenate([rd(kbuf, start), rd(kbuf, span + start)], axis=0).astype(BF16)
                mask = jnp.logical_or(own_ok, jnp.logical_and(prev_ok, sp > 0))
                scs.append(jnp.where(mask, _nt(ks, qs), NEG))
            for start in starts:
                vsts.append(jnp.concatenate([rd(vbuf, start), rd(vbuf, span + start)], axis=0).T.astype(BF16))
            es, dens, lses = [], [], []
            for sc in scs:
                m = jnp.max(sc, axis=0, keepdims=True)
                e = jnp.exp(sc - m)
                den = jnp.sum(e, axis=0, keepdims=True)
                es.append(e.astype(BF16))
                dens.append(den)
                lses.append(m + jnp.log(den))
            ots = [jnp.dot(vst, e, preferred_element_type=F32) for vst, e in zip(vsts, es)]
            for start, ot, den, lse in zip(starts, ots, dens, lses):
                ot = jnp.concatenate([ot[:HEAD_DIM, :LANES] / den[:, :LANES],
                                      ot[HEAD_DIM:, LANES:] / den[:, LANES:]], axis=0)
                lt = jnp.concatenate([jnp.broadcast_to(lse[:, :LANES], (HEAD_DIM, LANES)),
                                      jnp.broadcast_to(lse[:, LANES:], (HEAD_DIM, LANES))], axis=0)
                o2 = ot.T
                l2 = lt.T
                if g > 0:
                    o_prev = rd(o_ref, start)
                    l_prev = rd(lbuf, start)
                    mm = jnp.maximum(l_prev, l2)
                    wp = jnp.exp(l_prev - mm)
                    wc = jnp.exp(l2 - mm)
                    tot = wp + wc
                    o2 = (wp * o_prev + wc * o2) / tot
                    l2 = mm + jnp.log(tot)
                wr(o_ref, start, o2)
                if g < N_GROUPS - 1:
                    wr(lbuf, start, l2)
            return carry

        lax.fori_loop(0, rows // (LANES * ATT_ILP), sub_blocks, 0)


def _attn_prompt(qkv, b, s):
    max_span = LANES * max(dil for _, dil in ATT_GROUPS)
    assert all(win // dil == LANES for win, dil in ATT_GROUPS)
    assert s <= ATT_ROWS and s % max_span == 0 and s % (LANES * ATT_ILP) == 0
    qkv4 = qkv.reshape(QKV_COLS // LANES, b, s, LANES)
    blk = (None, None, s, LANES)
    in_specs = []
    for g in range(N_GROUPS):
        for which in range(3):
            s0 = which * (Q_COLS // LANES) + g * ATT_SLABS
            in_specs.append(pl.BlockSpec(blk, lambda bb, c, s0=s0: (s0 + c, bb, 0, 0)))
    out = pl.pallas_call(
        _attn_prompt_body,
        out_shape=jax.ShapeDtypeStruct((ATT_SLABS, b, s, LANES), F32),
        grid=(b, ATT_SLABS),
        in_specs=in_specs,
        out_specs=pl.BlockSpec(blk, lambda bb, c: (c, bb, 0, 0)),
        scratch_shapes=[pltpu.VMEM((max_span + s, LANES), F32),
                        pltpu.VMEM((max_span + s, LANES), F32),
                        pltpu.VMEM((s, LANES), F32)],
        compiler_params=pltpu.CompilerParams(dimension_semantics=("parallel", "parallel"),
                                             vmem_limit_bytes=ATT_VMEM),
        name="attn_prompt",
    )(*([qkv4] * (3 * N_GROUPS)))
    return out.reshape(ATT_SLABS, b * s, LANES)


def _split3(a):
    hi = a.astype(BF16)
    r1 = a - hi.astype(F32)
    mid = r1.astype(BF16)
    lo = (r1 - mid.astype(F32)).astype(BF16)
    return hi, mid, lo


def _expand_heads(a, emat):
    return sum(jnp.dot(part, emat, preferred_element_type=F32) for part in _split3(a))


def _ssd_prompt_body(xbc_ref, dt_ref, cw_ref, cb_ref, a_ref, dx_ref, e_ref, ltri_ref,
                     y_ref, st_ref, xext, h_ref, *, n_chunks):
    i = pl.program_id(1)
    lc = n_chunks * CHUNK
    halo = SUBLANES

    @pl.when(i == 0)
    def _():
        xext[0:halo, :] = jnp.zeros((halo, CONV_CH), F32)
        h_ref[...] = jnp.zeros_like(h_ref)

    xext[halo:halo + lc, :] = xbc_ref[...]

    lane_x = lax.broadcasted_iota(jnp.int32, (CHUNK, D_INNER), 1) & (CHUNK - 1)
    row_x = lax.broadcasted_iota(jnp.int32, (CHUNK, D_INNER), 0)
    diag = lane_x == row_x
    tril = row_x >= lane_x
    lane_p = lax.broadcasted_iota(jnp.int32, (CHUNK, LANES), 1)
    low_half = lane_p < CHUNK
    a_row = a_ref[...]
    dx = dx_ref[...]
    emat = e_ref[...]
    ltri = ltri_ref[...]

    for c in range(n_chunks):
        o = c * CHUNK
        conv = cb_ref[...] + xext[halo - 3 + o:halo - 3 + o + CHUNK, :] * cw_ref[0:1, :]
        conv = conv + xext[halo - 2 + o:halo - 2 + o + CHUNK, :] * cw_ref[1:2, :]
        conv = conv + xext[halo - 1 + o:halo - 1 + o + CHUNK, :] * cw_ref[2:3, :]
        conv = conv + xext[halo + o:halo + o + CHUNK, :] * cw_ref[3:4, :]
        xc = conv * jax.nn.sigmoid(conv)
        xs = xc[:, :D_INNER]
        bm = xc[:, D_INNER:D_INNER + SSD_GROUPS * D_STATE]
        cm = xc[:, D_INNER + SSD_GROUPS * D_STATE:]

        dt = dt_ref[o:o + CHUNK, :]
        dtx = _expand_heads(dt, emat)
        acs = sum(jnp.dot(ltri, part, preferred_element_type=F32) for part in _split3(dt * a_row))
        acsx = _expand_heads(acs, emat)
        last = acsx[CHUNK - 1:CHUNK, :]
        xdt = xs * dtx
        xw = (xdt * jnp.exp(last - acsx)).astype(BF16)
        eacs = jnp.exp(acsx)
        cdec = jnp.exp(last)
        rrow = jnp.sum(jnp.where(diag, acsx, 0.0), axis=0, keepdims=True)
        lm = jnp.exp(jnp.where(tril, acsx - rrow, NEG))
        xdtb = xdt.astype(BF16)
        cmb = cm.astype(BF16)

        for g in range(SSD_GROUPS):
            gs = slice(g * SSD_GW, (g + 1) * SSD_GW)
            bg = bm[:, g * D_STATE:(g + 1) * D_STATE]
            bgb = bg.astype(BF16)
            cg = cmb[:, g * D_STATE:(g + 1) * D_STATE]
            cbx = _nt(cg, jnp.concatenate([bgb] * (SSD_GW // CHUNK), axis=0))
            wg = (cbx * lm[:, gs]).astype(BF16)
            ydiag = []
            for pr in range(SSD_GW // LANES):
                xp = xdtb[:, g * SSD_GW + pr * LANES:g * SSD_GW + (pr + 1) * LANES]
                zero = jnp.zeros_like(xp)
                bd = jnp.concatenate([jnp.where(low_half, xp, zero), jnp.where(low_half, zero, xp)], axis=0)
                ydiag.append(jnp.dot(wg[:, pr * LANES:(pr + 1) * LANES], bd, preferred_element_type=F32))
            hp = h_ref[g]
            yoff = jnp.dot(cg, hp.astype(BF16), preferred_element_type=F32) * eacs[:, gs]
            st = jnp.dot(bg.T.astype(BF16), xw[:, gs], preferred_element_type=F32)
            h_ref[g] = hp * cdec[:, gs] + st
            y_ref[o:o + CHUNK, gs] = jnp.concatenate(ydiag, axis=1) + yoff + dx[:, gs] * xs[:, gs]

    xext[0:halo, :] = xext[lc:lc + halo, :]

    @pl.when(i == pl.num_programs(1) - 1)
    def _():
        for g in range(SSD_GROUPS):
            st_ref[g * SSD_GW:(g + 1) * SSD_GW, :] = h_ref[g].T


def _ssd_consts(a_log, d_skip):
    a = -jnp.exp(a_log.astype(F32))
    a_row = jnp.pad(a, (0, LANES - SSD_HEADS))[None, :]
    a_x = jnp.repeat(a, CHUNK)[None, :]
    d_x = jnp.repeat(d_skip.astype(F32), CHUNK)[None, :]
    emat = (jnp.arange(D_INNER)[None, :] // CHUNK == jnp.arange(LANES)[:, None]).astype(BF16)
    ltri = (jnp.arange(CHUNK)[:, None] >= jnp.arange(CHUNK)[None, :]).astype(BF16)
    return a_row, a_x, d_x, emat, ltri


def _ssd_prompt(rest, dt, conv_w, conv_b, a_row, d_x, emat, ltri, b, s, n_chunks=SSD_STEP_CHUNKS):
    lc = n_chunks * CHUNK
    rest_v = rest.reshape(b, s, REST_COLS)
    dt_v = dt.reshape(b, s, LANES)
    const = lambda shape: pl.BlockSpec(shape, lambda bb, i: (0,) * len(shape))
    body = functools.partial(_ssd_prompt_body, n_chunks=n_chunks)
    y, st = pl.pallas_call(
        body,
        out_shape=(jax.ShapeDtypeStruct((b, s, D_INNER), F32),
                   jax.ShapeDtypeStruct((b, D_INNER, D_STATE), F32)),
        grid=(b, s // lc),
        in_specs=[pl.BlockSpec((None, lc, CONV_CH), lambda bb, i: (bb, i, 0)),
                  pl.BlockSpec((None, lc, LANES), lambda bb, i: (bb, i, 0)),
                  const((CONV_W, CONV_CH)), const((1, CONV_CH)), const((1, LANES)),
                  const((1, D_INNER)), const((LANES, D_INNER)), const((CHUNK, CHUNK))],
        out_specs=(pl.BlockSpec((None, lc, D_INNER), lambda bb, i: (bb, i, 0)),
                   pl.BlockSpec((None, D_INNER, D_STATE), lambda bb, i: (bb, 0, 0))),
        scratch_shapes=[pltpu.VMEM((lc + SUBLANES, CONV_CH), F32),
                        pltpu.VMEM((SSD_GROUPS, D_STATE, SSD_GW), F32)],
        compiler_params=_params(("parallel", "arbitrary")),
        name="ssd_prompt",
    )(rest_v, dt_v, conv_w, conv_b, a_row, d_x, emat, ltri)
    return y.reshape(b * s, D_INNER), st


def _attn_step_body(q_ref, kn_ref, vn_ref, c0_ref, c1_ref, c2_ref, o_ref):
    caches = (c0_ref, c1_ref, c2_ref)
    hrow = lax.broadcasted_iota(jnp.int32, (N_HEADS, ATT_W), 0)
    hlane = lax.broadcasted_iota(jnp.int32, (N_HEADS, ATT_W), 1) // HEAD_DIM
    hmask = hrow == hlane
    s_list, sn_list = [], []
    for g, (win, dil) in enumerate(ATT_GROUPS):
        qbd = jnp.where(hmask, jnp.broadcast_to(q_ref[g:g + 1, :], (N_HEADS, ATT_W)), 0.0).astype(BF16)
        s = jnp.dot(qbd, caches[g][0].astype(BF16), preferred_element_type=F32)
        if dil > 1:
            wpos = lax.broadcasted_iota(jnp.int32, s.shape, 1)
            s = jnp.where((wpos & (dil - 1)) == 0, s, NEG)
        s_list.append(s)
        sn_list.append(jnp.sum(qbd.astype(F32) * kn_ref[g:g + 1, :].astype(BF16).astype(F32),
                               axis=1, keepdims=True))
    m = sn_list[0]
    for g in range(N_GROUPS):
        m = jnp.maximum(m, jnp.maximum(jnp.max(s_list[g], axis=1, keepdims=True), sn_list[g]))
    den = jnp.zeros((N_HEADS, 1), F32)
    acc = jnp.zeros((N_HEADS, ATT_W), F32)
    for g in range(N_GROUPS):
        e = jnp.exp(s_list[g] - m)
        en = jnp.exp(sn_list[g] - m)
        den = den + jnp.sum(e, axis=1, keepdims=True) + en
        acc = acc + _nt(e.astype(BF16), caches[g][1].astype(BF16))
        acc = acc + en.astype(BF16).astype(F32) * vn_ref[g:g + 1, :].astype(BF16).astype(F32)
    o_ref[...] = jnp.sum(jnp.where(hmask, acc / den, 0.0), axis=0, keepdims=True)


def _attn_step(q, kn, vn, caches):
    nb = q.shape[0]
    views, specs = [], []
    for (win, dil), c in zip(ATT_GROUPS, caches):
        w = c.shape[1]
        assert w == win, "cache must hold exactly one window of past rows"
        views.append(c.transpose(0, 2, 3, 4, 1).reshape(nb, 2, ATT_W, w))
        specs.append(pl.BlockSpec((None, 2, ATT_W, w), lambda i: (i, 0, 0, 0)))
    qspec = pl.BlockSpec((None, N_GROUPS, ATT_W), lambda i: (i, 0, 0))
    out = pl.pallas_call(
        _attn_step_body,
        out_shape=jax.ShapeDtypeStruct((nb, 1, ATT_W), F32),
        grid=(nb,),
        in_specs=[qspec, qspec, qspec, *specs],
        out_specs=pl.BlockSpec((None, 1, ATT_W), lambda i: (i, 0, 0)),
        compiler_params=_params(("parallel",)),
        name="attn_step",
    )(q, kn, vn, *views)
    return out.reshape(nb, ATT_W)


def _conv_step_body(rest_ref, sc_ref, dt_ref, cw_ref, cb_ref, ax_ref, e_ref,
                    xs_ref, bm_ref, cm_ref, xdt_ref, dec_ref):
    conv = cb_ref[...] + sc_ref[:, 0:CONV_CH] * cw_ref[0:1, :]
    conv = conv + sc_ref[:, CONV_CH:2 * CONV_CH] * cw_ref[1:2, :]
    conv = conv + sc_ref[:, 2 * CONV_CH:3 * CONV_CH] * cw_ref[2:3, :]
    conv = conv + rest_ref[...] * cw_ref[3:4, :]
    xc = conv * jax.nn.sigmoid(conv)
    xs = xc[:, :D_INNER]
    dtx = _expand_heads(dt_ref[...], e_ref[...])
    xs_ref[...] = xs
    bm_ref[...] = xc[:, D_INNER:D_INNER + SSD_GROUPS * D_STATE]
    cm_ref[...] = xc[:, D_INNER + SSD_GROUPS * D_STATE:]
    xdt_ref[...] = xs * dtx
    dec_ref[...] = jnp.exp(dtx * ax_ref[...])


def _state_step_body(h_ref, xdt_ref, dec_ref, b_ref, c_ref, xs_ref, dx_ref, ho_ref, y_ref, *, bb):
    grow = lax.broadcasted_iota(jnp.int32, (SSD_GROUPS, D_INNER), 0)
    glane = lax.broadcasted_iota(jnp.int32, (SSD_GROUPS, D_INNER), 1) // SSD_GW
    gmask = grow == glane
    for bi in range(bb):
        h = h_ref[bi]
        bmat = b_ref[bi]
        bx = jnp.concatenate([jnp.broadcast_to(bmat[g:g + 1, :], (SSD_GW, D_STATE))
                              for g in range(SSD_GROUPS)], axis=0)
        hn = h * dec_ref[:, bi:bi + 1] + xdt_ref[:, bi:bi + 1] * bx
        ho_ref[bi] = hn
        y8 = _nt(c_ref[bi].astype(BF16), hn.astype(BF16))
        y = jnp.sum(jnp.where(gmask, y8, 0.0), axis=0, keepdims=True)
        y_ref[bi:bi + 1, :] = y + dx_ref[...] * xs_ref[bi:bi + 1, :]


def _ssd_step(rest, dt, state_conv, state_ssm, conv_w, conv_b, a_x, d_x, emat, bb=4):
    nb = rest.shape[0]
    nblk = nb // bb
    full = lambda shape: pl.BlockSpec(shape, lambda i: (0,) * len(shape))
    xs, bm, cm, xdt, dec = pl.pallas_call(
        _conv_step_body,
        out_shape=(jax.ShapeDtypeStruct((nb, D_INNER), F32),
                   jax.ShapeDtypeStruct((nb, SSD_GROUPS * D_STATE), F32),
                   jax.ShapeDtypeStruct((nb, SSD_GROUPS * D_STATE), F32),
                   jax.ShapeDtypeStruct((nb, D_INNER), F32),
                   jax.ShapeDtypeStruct((nb, D_INNER), F32)),
        grid=(1,),
        in_specs=[pl.BlockSpec((nb, CONV_CH), lambda i: (0, 0)),
                  full((nb, (CONV_W - 1) * CONV_CH)), full((nb, LANES)),
                  full((CONV_W, CONV_CH)), full((1, CONV_CH)), full((1, D_INNER)),
                  full((LANES, D_INNER))],
        out_specs=(full((nb, D_INNER)), full((nb, SSD_GROUPS * D_STATE)),
                   full((nb, SSD_GROUPS * D_STATE)), full((nb, D_INNER)), full((nb, D_INNER))),
        compiler_params=_params(("arbitrary",)),
        name="conv_step",
    )(rest, state_conv.reshape(nb, (CONV_W - 1) * CONV_CH), dt, conv_w, conv_b, a_x, emat)

    to_cols = lambda a: a.reshape(nblk, bb, D_INNER).transpose(0, 2, 1)
    blk3 = lambda d1, d2: pl.BlockSpec((None, d1, d2), lambda i: (i, 0, 0))
    hspec = pl.BlockSpec((bb, D_INNER, D_STATE), lambda i: (i, 0, 0))
    gspec = pl.BlockSpec((bb, SSD_GROUPS, D_STATE), lambda i: (i, 0, 0))
    h_new, y = pl.pallas_call(
        functools.partial(_state_step_body, bb=bb),
        out_shape=(jax.ShapeDtypeStruct((nb, D_INNER, D_STATE), F32),
                   jax.ShapeDtypeStruct((nblk, bb, D_INNER), F32)),
        grid=(nblk,),
        in_specs=[hspec, blk3(D_INNER, bb), blk3(D_INNER, bb), gspec, gspec, blk3(bb, D_INNER),
                  pl.BlockSpec((1, D_INNER), lambda i: (0, 0))],
        out_specs=(hspec, blk3(bb, D_INNER)),
        compiler_params=_params(("parallel",)),
        name="state_step",
    )(state_ssm.reshape(nb, D_INNER, D_STATE), to_cols(xdt), to_cols(dec),
      bm.reshape(nb, SSD_GROUPS, D_STATE), cm.reshape(nb, SSD_GROUPS, D_STATE),
      xs.reshape(nblk, bb, D_INNER), d_x)
    return y.reshape(nb, D_INNER), h_new


def _merge_body(x_ref, att_ref, y_ref, z_ref, ga_ref, gb_ref, gn_ref, wa_ref, wb_ref, wo_ref, o_ref):
    att = jnp.concatenate([att_ref[c] for c in range(ATT_SLABS)], axis=1)
    out_a = jnp.dot(att.astype(BF16), wa_ref[...], preferred_element_type=F32)
    z = z_ref[...]
    y = y_ref[...] * (z * jax.nn.sigmoid(z))
    yn = _rmsnorm_rows(y, gn_ref[...]).astype(BF16)
    out_b = jnp.dot(yn, wb_ref[...], preferred_element_type=F32)
    merged = jax.nn.sigmoid(ga_ref[...]) * out_a + jax.nn.sigmoid(gb_ref[...]) * out_b
    o_ref[...] = x_ref[...] + jnp.dot(merged.astype(BF16), wo_ref[...], preferred_element_type=F32)


def _merge(x2d, att, y, rest, ssd_norm, w_att_out, w_ssd_out, w_out, tm=256):
    t = x2d.shape[0]
    tm = min(tm, t)
    row = lambda w, cb: pl.BlockSpec((tm, w), lambda i: (i, cb))
    full = lambda shape: pl.BlockSpec(shape, lambda i: (0, 0))
    return pl.pallas_call(
        _merge_body,
        out_shape=jax.ShapeDtypeStruct((t, D_MODEL), F32),
        grid=(t // tm,),
        in_specs=[row(D_MODEL, 0),
                  pl.BlockSpec((ATT_SLABS, tm, LANES), lambda i: (0, i, 0)),
                  row(D_INNER, 0),
                  row(D_INNER, REST_Z // D_INNER),
                  row(D_MODEL, REST_GATES // D_MODEL), row(D_MODEL, REST_GATES // D_MODEL + 1),
                  full((1, D_INNER)), full((ATT_W, D_MODEL)), full((D_INNER, D_MODEL)),
                  full((D_MODEL, D_MODEL))],
        out_specs=row(D_MODEL, 0),
        compiler_params=_params(("parallel",)),
        name="merge",
    )(x2d, att, y, rest, rest, rest, ssd_norm, w_att_out, w_ssd_out, w_out)


I1_BLOCK = SUBLANES
BF16_ROWS = 2 * SUBLANES
PEER_GATE_TT = 512
PEER_DENSE_TT = 1024
PEER_DENSE_VMEM = VMEM_BYTES_V7X * 7 // 8


def _gelu_tanh(x):
    return 0.5 * x * (1.0 + jnp.tanh(math.sqrt(2.0 / math.pi) * (x + 0.044715 * (x * x * x))))


def _peer_gate_body(x_ref, g_ref, wq_ref, k1_ref, k2_ref,
                    xnt_ref, r2_ref, e2_ref, n1_ref, w1_ref, s_all, v_all, rk1, *, tg):
    nch = tg // LANES
    nkb = N_KEYS // I1_BLOCK
    xn = _rmsnorm_rows(x_ref[...], g_ref[...])
    xnt = xn.T.astype(BF16)
    xnt_ref[...] = xnt
    qt = jnp.dot(wq_ref[...], xnt, preferred_element_type=F32)
    for h in range(PEER_HEADS):
        for sd, kref in enumerate((k1_ref, k2_ref)):
            r0 = (2 * h + sd) * N_KEYS
            s_all[2 * h + sd] = jnp.dot(kref[h], qt[r0:r0 + N_KEYS, :].astype(BF16),
                                        preferred_element_type=F32)

    rowid = lax.broadcasted_iota(jnp.int32, (N_KEYS, LANES), 0).astype(F32)
    row16 = lax.broadcasted_iota(jnp.int32, (PEER_TOPK, LANES), 0)
    row16f = row16.astype(F32)

    def make_extract(exact):
        def extract(idx, ties):
            h = idx // nch
            off = pl.multiple_of((idx % nch) * LANES, LANES)
            s = [s_all[2 * h + sd, :, pl.ds(off, LANES)] for sd in range(2)]
            rank = [jnp.full((N_KEYS, LANES), float(PEER_TOPK), F32) for _ in range(2)]
            vals = [jnp.zeros((PEER_TOPK, LANES), F32) for _ in range(2)]
            for k in range(PEER_TOPK):
                for sd in range(2):
                    m = jnp.max(s[sd], axis=0, keepdims=True)
                    sel = s[sd] == m
                    if exact:
                        first = jnp.min(jnp.where(sel, rowid, float(N_KEYS)), axis=0, keepdims=True)
                        sel = rowid == first
                    rank[sd] = jnp.where(sel, float(k), rank[sd])
                    s[sd] = jnp.where(sel, -jnp.inf, s[sd])
                    vals[sd] = jnp.where(row16 == k, m, vals[sd])
            rk1[h, :, pl.ds(off, LANES)] = rank[0]
            r2_ref[h, :, :, pl.ds(off, LANES)] = rank[1].reshape(N_KEYS // BF16_ROWS, BF16_ROWS, LANES).astype(BF16)
            v_all[2 * h, :, pl.ds(off, LANES)] = vals[0]
            v_all[2 * h + 1, :, pl.ds(off, LANES)] = vals[1]
            if not exact:
                for sd in range(2):
                    taken = jnp.sum(jnp.where(rank[sd] < float(PEER_TOPK), 1.0, 0.0), axis=0, keepdims=True)
                    ties = jnp.maximum(ties, jnp.where(taken != float(PEER_TOPK), 1.0, 0.0))
            return ties
        return extract

    no_ties = jnp.zeros((1, LANES), F32)
    ties = lax.fori_loop(0, PEER_HEADS * nch, make_extract(False), no_ties)

    @pl.when(jnp.max(ties) > 0.0)
    def _():
        lax.fori_loop(0, PEER_HEADS * nch, make_extract(True), no_ties)

    def finish(h, carry):
        for c in range(nch):
            lanes = slice(c * LANES, (c + 1) * LANES)
            v1 = v_all[2 * h, :, lanes]
            v2 = v_all[2 * h + 1, :, lanes]
            shifted = [jnp.broadcast_to(v2[b:b + 1, :], (PEER_TOPK, LANES)) for b in range(PEER_TOPK)]
            cnt = jnp.zeros((PEER_TOPK, LANES), F32)
            zsum = jnp.zeros((1, LANES), F32)
            top = v1[0:1, :] + v2[0:1, :]
            for step in range(PEER_TOPK):
                front = v1 + shifted[0]
                m = jnp.max(front, axis=0, keepdims=True)
                first = jnp.min(jnp.where(front == m, row16f, float(PEER_TOPK)), axis=0, keepdims=True)
                sel = row16f == first
                cnt = cnt + jnp.where(sel, 1.0, 0.0)
                zsum = zsum + jnp.exp(m - top)
                live = PEER_TOPK - 1 - step
                for b in range(live):
                    shifted[b] = jnp.where(sel, shifted[b + 1], shifted[b])
            rz = 1.0 / zsum
            r1 = rk1[h, :, lanes]
            s1 = s_all[2 * h, :, lanes]
            s2 = s_all[2 * h + 1, :, lanes]
            n_of = jnp.zeros((N_KEYS, LANES), F32)
            for a in range(PEER_TOPK):
                n_of = jnp.where(r1 == float(a), cnt[a:a + 1, :], n_of)
            wgt = jnp.exp(s1 - v1[0:1, :]) * rz
            e2v = jnp.exp(s2 - v2[0:1, :])
            e2_ref[h, :, :, lanes] = e2v.reshape(N_KEYS // BF16_ROWS, BF16_ROWS, LANES).astype(BF16)
            for kb in range(nkb):
                rs = slice(kb * I1_BLOCK, (kb + 1) * I1_BLOCK)
                n1_ref[kb, h, :, lanes] = n_of[rs, :]
                w1_ref[kb, h, :, lanes] = wgt[rs, :]
        return carry

    lax.fori_loop(0, PEER_HEADS, finish, 0)


def _peer_dense_body(x_ref, xnt_ref, r2_ref, e2_ref, n1_ref, w1_ref, u_ref, vt_ref, o_ref,
                     acct, act, hbuf, *, tt):
    j = pl.program_id(1)
    nch = tt // LANES
    half = I1_BLOCK // 2 * N_KEYS
    ktiles = N_KEYS // BF16_ROWS

    @pl.when(j == 0)
    def _():
        acct[...] = jnp.zeros_like(acct)

    xnt = xnt_ref[...]
    for hf in range(2):
        act[hf * half:(hf + 1) * half, :] = jnp.dot(u_ref[hf * half:(hf + 1) * half, :], xnt,
                                                    preferred_element_type=F32)
    for hf in range(2):
        for ii in range(hf * I1_BLOCK // 2, (hf + 1) * I1_BLOCK // 2):
            for c in range(nch):
                lanes = slice(c * LANES, (c + 1) * LANES)
                gate = None
                for h in range(PEER_HEADS):
                    n16 = jnp.broadcast_to(n1_ref[h, ii:ii + 1, lanes], (BF16_ROWS, LANES)).astype(BF16)
                    w16 = jnp.broadcast_to(w1_ref[h, ii:ii + 1, lanes], (BF16_ROWS, LANES)).astype(BF16)
                    prod = e2_ref[h, :, :, lanes] * w16[None]
                    term = jnp.where(r2_ref[h, :, :, lanes] < n16[None], prod, jnp.zeros_like(prod))
                    gate = term if gate is None else gate + term
                a = act[ii * N_KEYS:(ii + 1) * N_KEYS, lanes]
                hv = gate.astype(F32).reshape(N_KEYS, LANES) * _gelu_tanh(a)
                hbuf[ii * N_KEYS:(ii + 1) * N_KEYS, lanes] = hv.astype(BF16)
        acct[...] += jnp.dot(vt_ref[:, hf * half:(hf + 1) * half], hbuf[hf * half:(hf + 1) * half, :],
                             preferred_element_type=F32)

    @pl.when(j == pl.num_programs(1) - 1)
    def _():
        o_ref[...] = x_ref[...] + acct[...].T


def _peer(x2d, norm_ffn, wq_t, keys1, keys2, u_bf, vt_bf):
    t = x2d.shape[0]
    tg = min(PEER_GATE_TT, t)
    tt = min(PEER_DENSE_TT, t)
    nkb = N_KEYS // I1_BLOCK
    eb = I1_BLOCK * N_KEYS
    ktiles = N_KEYS // BF16_ROWS
    full1 = lambda shape: pl.BlockSpec(shape, lambda i: (0,) * len(shape))
    tab_shape = jax.ShapeDtypeStruct((PEER_HEADS, ktiles, BF16_ROWS, t), BF16)
    row_shape = jax.ShapeDtypeStruct((nkb, PEER_HEADS, I1_BLOCK, t), F32)
    tab_spec1 = pl.BlockSpec((PEER_HEADS, ktiles, BF16_ROWS, tg), lambda i: (0, 0, 0, i))
    row_spec1 = pl.BlockSpec((nkb, PEER_HEADS, I1_BLOCK, tg), lambda i: (0, 0, 0, i))
    xnt, r2, e2, n1, w1 = pl.pallas_call(
        functools.partial(_peer_gate_body, tg=tg),
        out_shape=(jax.ShapeDtypeStruct((D_MODEL, t), BF16), tab_shape, tab_shape, row_shape, row_shape),
        grid=(t // tg,),
        in_specs=[pl.BlockSpec((tg, D_MODEL), lambda i: (i, 0)),
                  full1((1, D_MODEL)), full1((2 * PEER_HEADS * N_KEYS, D_MODEL)),
                  full1((PEER_HEADS, N_KEYS, N_KEYS)), full1((PEER_HEADS, N_KEYS, N_KEYS))],
        out_specs=(pl.BlockSpec((D_MODEL, tg), lambda i: (0, i)), tab_spec1, tab_spec1, row_spec1, row_spec1),
        scratch_shapes=[pltpu.VMEM((2 * PEER_HEADS, N_KEYS, tg), F32),
                        pltpu.VMEM((2 * PEER_HEADS, PEER_TOPK, tg), F32),
                        pltpu.VMEM((PEER_HEADS, N_KEYS, tg), F32)],
        compiler_params=_params(("parallel",)),
        name="peer_gate",
    )(x2d, norm_ffn, wq_t, keys1, keys2)

    tab_spec = pl.BlockSpec((PEER_HEADS, ktiles, BF16_ROWS, tt), lambda i, j: (0, 0, 0, i))
    row_spec = pl.BlockSpec((None, PEER_HEADS, I1_BLOCK, tt), lambda i, j: (j, 0, 0, i))
    return pl.pallas_call(
        functools.partial(_peer_dense_body, tt=tt),
        out_shape=jax.ShapeDtypeStruct((t, D_MODEL), F32),
        grid=(t // tt, nkb),
        in_specs=[pl.BlockSpec((tt, D_MODEL), lambda i, j: (i, 0)),
                  pl.BlockSpec((D_MODEL, tt), lambda i, j: (0, i)),
                  tab_spec, tab_spec, row_spec, row_spec,
                  pl.BlockSpec((eb, D_MODEL), lambda i, j: (j, 0)),
                  pl.BlockSpec((D_MODEL, eb), lambda i, j: (0, j))],
        out_specs=pl.BlockSpec((tt, D_MODEL), lambda i, j: (i, 0)),
        scratch_shapes=[pltpu.VMEM((D_MODEL, tt), F32),
                        pltpu.VMEM((eb, tt), F32),
                        pltpu.VMEM((eb, tt), BF16)],
        compiler_params=pltpu.CompilerParams(dimension_semantics=("parallel", "arbitrary"),
                                             vmem_limit_bytes=PEER_DENSE_VMEM),
        name="peer_dense",
    )(x2d, xnt, r2, e2, n1, w1, u_bf, vt_bf)


def _ple_body(x_ref, p_ref, gp_ref, wg_ref, wp_ref, gf_ref, o_ref):
    x = x_ref[...]
    xn = _rmsnorm_rows(x, gp_ref[...]).astype(BF16)
    pg = jax.nn.sigmoid(jnp.dot(xn, wg_ref[...], preferred_element_type=F32))
    x3 = x + pg * jnp.dot(p_ref[...].astype(BF16), wp_ref[...], preferred_element_type=F32)
    o_ref[...] = _rmsnorm_rows(x3, gf_ref[...])


def _ple_final(x2d, p2d, norm_ple, w_gate, w_proj, norm_final, tm=512):
    t = x2d.shape[0]
    tm = min(tm, t)
    full = lambda shape: pl.BlockSpec(shape, lambda i: (0, 0))
    return pl.pallas_call(
        _ple_body,
        out_shape=jax.ShapeDtypeStruct((t, D_MODEL), F32),
        grid=(t // tm,),
        in_specs=[pl.BlockSpec((tm, D_MODEL), lambda i: (i, 0)),
                  pl.BlockSpec((tm, PLE_DIM), lambda i: (i, 0)),
                  full((1, D_MODEL)), full((D_MODEL, D_MODEL)), full((PLE_DIM, D_MODEL)),
                  full((1, D_MODEL))],
        out_specs=pl.BlockSpec((tm, D_MODEL), lambda i: (i, 0)),
        compiler_params=_params(("parallel",)),
        name="ple_final",
    )(x2d, p2d, norm_ple, w_gate, w_proj, norm_final)


def _prep_weights(norm_mix, w_in, conv_w, conv_b, dt_bias, a_log, d_skip, ssd_norm, w_att_out,
                  w_ssd_out, w_out, norm_ffn, w_peer_q, peer_keys1, peer_keys2, peer_u, peer_v,
                  norm_ple, w_ple_gate, w_ple_proj, norm_final):
    w = w_in[0]
    c = [0, Q_COLS, 2 * Q_COLS, 3 * Q_COLS, 3 * Q_COLS + D_INNER, 3 * Q_COLS + D_INNER + CONV_CH,
         3 * Q_COLS + D_INNER + CONV_CH + SSD_HEADS]
    wq, wk, wv, wz, wxbc, wdt, wgt = (w[:, c[0]:c[1]], w[:, c[1]:c[2]], w[:, c[2]:c[3]], w[:, c[3]:c[4]],
                                      w[:, c[4]:c[5]], w[:, c[5]:c[6]], w[:, c[6]:])
    row = lambda v: v.reshape(1, -1).astype(F32)
    p = {}
    p["norm_mix"] = row(norm_mix[0])
    p["w_qkv"] = jnp.concatenate([_split_halves(wq * (HEAD_DIM ** -0.5)), _split_halves(wk), wv],
                                 axis=1).astype(BF16)
    p["w_rest"] = jnp.concatenate([wxbc, wz, wgt], axis=1).astype(BF16)
    p["w_dt"] = jnp.pad(wdt, ((0, 0), (0, LANES - SSD_HEADS))).astype(BF16)
    p["dt_bias"] = jnp.pad(row(dt_bias[0]), ((0, 0), (0, LANES - SSD_HEADS)))
    p["conv_w"] = conv_w[0].astype(F32)
    p["conv_b"] = row(conv_b[0])
    p["a_row"], p["a_x"], p["d_x"], p["emat"], p["ltri"] = _ssd_consts(a_log[0], d_skip[0])
    p["ssd_norm"] = row(ssd_norm[0])
    p["w_att_out"] = w_att_out[0].astype(BF16)
    p["w_ssd_out"] = w_ssd_out[0].astype(BF16)
    p["w_out"] = w_out[0].astype(BF16)
    p["norm_ffn"] = row(norm_ffn[0])
    p["wq_t"] = w_peer_q[0].T.astype(BF16)
    p["keys1"] = peer_keys1[0].astype(BF16)
    p["keys2"] = peer_keys2[0].astype(BF16)
    p["u"] = peer_u[0].astype(BF16)
    p["vt"] = peer_v[0].T.astype(BF16)
    p["norm_ple"] = row(norm_ple[0])
    p["w_ple_gate"] = w_ple_gate[0].astype(BF16)
    p["w_ple_proj"] = w_ple_proj[0].astype(BF16)
    p["norm_final"] = row(norm_final)
    return p


def _split_halves(w):
    k, n = w.shape
    return w.reshape(k, n // LANES, 2, 2, HEAD_DIM // 2).transpose(0, 1, 3, 2, 4).reshape(k, n)


def _tail(x2d, att, y, rest, p2d, p):
    x1 = _merge(x2d, att, y, rest, p["ssd_norm"], p["w_att_out"], p["w_ssd_out"], p["w_out"])
    x2 = _peer(x1, p["norm_ffn"], p["wq_t"], p["keys1"], p["keys2"], p["u"], p["vt"])
    return _ple_final(x2, p2d, p["norm_ple"], p["w_ple_gate"], p["w_ple_proj"], p["norm_final"])


def _heads_from_slabs(qkv, which, g, b, s, rows):
    s0 = which * (Q_COLS // LANES) + g * ATT_SLABS
    x = qkv.reshape(QKV_COLS // LANES, b, s, LANES)[s0:s0 + ATT_SLABS, :, s - rows:]
    x = x.transpose(1, 2, 0, 3)
    if which < 2:
        x = x.reshape(b, rows, ATT_SLABS, 2, 2, HEAD_DIM // 2).transpose(0, 1, 2, 4, 3, 5)
    return x.reshape(b, rows, N_HEADS, HEAD_DIM)


def _kv_rows(qkv, g, b, s, rows):
    return jnp.stack([_heads_from_slabs(qkv, 1, g, b, s, rows),
                      _heads_from_slabs(qkv, 2, g, b, s, rows)], axis=2)[None]


def _prompt(x, p_in, p):
    b, s, _ = x.shape
    x2d = x.reshape(b * s, D_MODEL)
    qkv, rest, dt = _projections(x2d, jnp.arange(s, dtype=jnp.int32), p["norm_mix"], p["w_qkv"],
                                 p["w_rest"], p["w_dt"], p["dt_bias"], tm=PROJ_TM)
    att = _attn_prompt(qkv, b, s)
    y, ssm = _ssd_prompt(rest, dt, p["conv_w"], p["conv_b"], p["a_row"], p["d_x"], p["emat"], p["ltri"], b, s)
    out = _tail(x2d, att, y, rest, p_in[0].reshape(b * s, PLE_DIM), p)
    kvs = [_kv_rows(qkv, g, b, s, min(win, s)) for g, (win, _) in enumerate(ATT_GROUPS)]
    conv = rest.reshape(b, s, REST_COLS)[:, s - (CONV_W - 1):, :CONV_CH][None]
    ssm = ssm.reshape(1, b, SSD_HEADS, CHUNK, D_STATE)
    return out.reshape(b, s, D_MODEL), kvs, conv, ssm


def _sample(x, p_in, caches, state_conv, state_ssm, p):
    b, s, _ = x.shape
    assert s == 1
    x2d = x.reshape(b, D_MODEL)
    pos = jnp.full((b,), PAST_LEN, dtype=jnp.int32)
    qkv, rest, dt = _projections(x2d, pos, p["norm_mix"], p["w_qkv"], p["w_rest"], p["w_dt"],
                                 p["dt_bias"], tm=b)
    rows = lambda which: jnp.stack([_heads_from_slabs(qkv, which, g, b, 1, 1).reshape(b, ATT_W)
                                    for g in range(N_GROUPS)], axis=1)
    q3, k3, v3 = rows(0), rows(1), rows(2)
    att = _attn_step(q3, k3, v3, [c[0] for c in caches])
    att = att.reshape(b, ATT_SLABS, LANES).transpose(1, 0, 2)
    xbc = rest[:, :CONV_CH]
    y, ssm = _ssd_step(xbc, dt, state_conv[0], state_ssm[0],
                       p["conv_w"], p["conv_b"], p["a_x"], p["d_x"], p["emat"])
    out = _tail(x2d, att, y, rest, p_in[0].reshape(b, PLE_DIM), p)
    hd = lambda a, g: a[:, g].reshape(b, N_HEADS, HEAD_DIM)
    kvs = [jnp.stack([hd(k3, g), hd(v3, g)], axis=1)[None, :, None] for g in range(N_GROUPS)]
    conv = jnp.concatenate([state_conv[0][:, 1:], xbc[:, None, :]], axis=1)[None]
    ssm = ssm.reshape(1, b, SSD_HEADS, CHUNK, D_STATE)
    return out.reshape(b, 1, D_MODEL), kvs, conv, ssm


def kernel(x_prompt, x_sample, cache_kv_w128, cache_kv_w512, cache_kv_w2048, state_conv, state_ssm, p_prompt, p_sample, norm_mix, w_in, conv_w, conv_b, dt_bias, a_log, d_skip, ssd_norm, w_att_out, w_ssd_out, w_out, norm_ffn, w_peer_q, peer_keys1, peer_keys2, peer_u, peer_v, norm_ple, w_ple_gate, w_ple_proj, norm_final):
    p = _prep_weights(norm_mix, w_in, conv_w, conv_b, dt_bias, a_log, d_skip, ssd_norm, w_att_out,
                      w_ssd_out, w_out, norm_ffn, w_peer_q, peer_keys1, peer_keys2, peer_u, peer_v,
                      norm_ple, w_ple_gate, w_ple_proj, norm_final)
    y_p, kv_p, conv_p, ssm_p = _prompt(x_prompt, p_prompt, p)
    y_s, kv_s, conv_s, ssm_s = _sample(x_sample, p_sample, (cache_kv_w128, cache_kv_w512, cache_kv_w2048),
                                       state_conv, state_ssm, p)
    return (y_p, y_s, kv_p[0], kv_p[1], kv_p[2], conv_p, ssm_p, kv_s[0], kv_s[1], kv_s[2], conv_s, ssm_s)
```

```python
import functools
import math

import jax
import jax.numpy as jnp
from jax import lax
from jax.experimental import pallas as pl
from jax.experimental.pallas import tpu as pltpu

F32 = jnp.float32
BF16 = jnp.bfloat16

LANES = 128
SUBLANES = 8
VMEM_BYTES_V7X = 64 * 1024 * 1024
VMEM_LIMIT = VMEM_BYTES_V7X * 3 // 4

D_MODEL = 1024
ATT_GROUPS = ((128, 1), (512, 4), (2048, 16))
N_GROUPS = len(ATT_GROUPS)
N_HEADS = 8
HEAD_DIM = 64
ATT_W = N_HEADS * HEAD_DIM
ATT_SLABS = ATT_W // LANES
Q_COLS = N_GROUPS * ATT_W
QKV_COLS = 3 * Q_COLS
ROPE_THETA = 10000.0
D_INNER = 2048
SSD_HEADS = 32
SSD_GROUPS = 8
SSD_GW = D_INNER // SSD_GROUPS
D_STATE = 128
CONV_W = 4
CONV_CH = D_INNER + 2 * SSD_GROUPS * D_STATE
CHUNK = 64
N_KEYS = 128
N_EXPERTS = N_KEYS * N_KEYS
PEER_HEADS = 8
PEER_TOPK = 16
PLE_DIM = 256
EPS = 1e-6
PAST_LEN = 8192
NEG = -1e30

REST_Z = CONV_CH
REST_GATES = REST_Z + D_INNER
REST_COLS = REST_GATES + 2 * D_MODEL
QKV_TN = 1536
REST_TN = 2048
PROJ_TM = 1024
ATT_ROWS = 4096
ATT_VMEM = VMEM_BYTES_V7X * 7 // 8
ATT_ILP = 8
SSD_STEP_CHUNKS = 4


def _params(sem):
    return pltpu.CompilerParams(dimension_semantics=sem, vmem_limit_bytes=VMEM_LIMIT)


def _rmsnorm_rows(x, g):
    return x * lax.rsqrt(jnp.mean(x * x, axis=-1, keepdims=True) + EPS) * g


def _nt(a, b):
    return lax.dot_general(a, b, (((1,), (1,)), ((), ())), preferred_element_type=F32)


def _proj_qkv_body(x_ref, g_ref, w_ref, cos_ref, sin_ref, o_ref, xn_ref, *, rope_tiles):
    j = pl.program_id(1)

    @pl.when(j == 0)
    def _():
        xn_ref[...] = _rmsnorm_rows(x_ref[...], g_ref[...]).astype(BF16)

    acc = jnp.dot(xn_ref[...], w_ref[...], preferred_element_type=F32)
    n_slab = acc.shape[1] // LANES

    @pl.when(j < rope_tiles)
    def _():
        cos = cos_ref[...]
        sin = sin_ref[...]
        for c in range(n_slab):
            a = acc[:, c * LANES:(c + 1) * LANES]
            o_ref[c] = a * cos + pltpu.roll(a, LANES // 2, 1) * sin

    @pl.when(j >= rope_tiles)
    def _():
        for c in range(n_slab):
            o_ref[c] = acc[:, c * LANES:(c + 1) * LANES]


def _proj_rest_body(x_ref, g_ref, w_ref, wdt_ref, bias_ref, o_ref, dt_ref, xn_ref):
    @pl.when(pl.program_id(1) == 0)
    def _():
        xn = _rmsnorm_rows(x_ref[...], g_ref[...]).astype(BF16)
        xn_ref[...] = xn
        v = jnp.dot(xn, wdt_ref[...], preferred_element_type=F32) + bias_ref[...]
        dt_ref[...] = jnp.maximum(v, 0.0) + jnp.log1p(jnp.exp(-jnp.abs(v)))

    o_ref[...] = jnp.dot(xn_ref[...], w_ref[...], preferred_element_type=F32)


def _rope_tables(pos):
    half = HEAD_DIM // 2
    inv = jnp.exp(jnp.arange(half, dtype=F32) * (-2.0 * math.log(ROPE_THETA) / HEAD_DIM))
    ang = pos.astype(F32)[:, None] * inv[None, :]
    cos, sin = jnp.cos(ang), jnp.sin(ang)
    cos_t = jnp.concatenate([cos, cos, cos, cos], axis=1)
    sin_t = jnp.concatenate([-sin, -sin, sin, sin], axis=1)
    return cos_t, sin_t


def _projections(x2d, pos, norm_mix, w_qkv, w_rest, w_dt, dt_bias_row, tm):
    t, k = x2d.shape
    tm = min(tm, t)
    cos_t, sin_t = _rope_tables(pos)
    n_pos_blocks = pos.shape[0] // tm
    x_spec = pl.BlockSpec((tm, k), lambda i, j: (i, 0))
    g_spec = pl.BlockSpec((1, k), lambda i, j: (0, 0))
    tab_spec = pl.BlockSpec((tm, LANES), lambda i, j: (i % n_pos_blocks, 0))
    slabs = QKV_TN // LANES
    qkv = pl.pallas_call(
        functools.partial(_proj_qkv_body, rope_tiles=2 * Q_COLS // QKV_TN),
        out_shape=jax.ShapeDtypeStruct((QKV_COLS // LANES, t, LANES), F32),
        grid=(t // tm, QKV_COLS // QKV_TN),
        in_specs=[x_spec, g_spec, pl.BlockSpec((k, QKV_TN), lambda i, j: (0, j)), tab_spec, tab_spec],
        out_specs=pl.BlockSpec((slabs, tm, LANES), lambda i, j: (j, i, 0)),
        scratch_shapes=[pltpu.VMEM((tm, k), BF16)],
        compiler_params=_params(("parallel", "arbitrary")),
        name="proj_qkv",
    )(x2d, norm_mix, w_qkv, cos_t, sin_t)
    rest, dt = pl.pallas_call(
        _proj_rest_body,
        out_shape=(jax.ShapeDtypeStruct((t, REST_COLS), F32), jax.ShapeDtypeStruct((t, LANES), F32)),
        grid=(t // tm, REST_COLS // REST_TN),
        in_specs=[x_spec, g_spec, pl.BlockSpec((k, REST_TN), lambda i, j: (0, j)),
                  pl.BlockSpec((k, LANES), lambda i, j: (0, 0)), pl.BlockSpec((1, LANES), lambda i, j: (0, 0))],
        out_specs=(pl.BlockSpec((tm, REST_TN), lambda i, j: (i, j)),
                   pl.BlockSpec((tm, LANES), lambda i, j: (i, 0))),
        scratch_shapes=[pltpu.VMEM((tm, k), BF16)],
        compiler_params=_params(("parallel", "arbitrary")),
        name="proj_rest",
    )(x2d, norm_mix, w_rest, w_dt, dt_bias_row)
    return qkv, rest, dt


def _attn_prompt_body(*refs):
    n_in = 3 * N_GROUPS
    in_refs = refs[:n_in]
    o_ref = refs[n_in]
    kbuf, vbuf, lbuf = refs[n_in + 1:n_in + 4]
    rows = o_ref.shape[0]

    krow = lax.broadcasted_iota(jnp.int32, (2 * LANES, 2 * LANES), 0)
    qidx = lax.broadcasted_iota(jnp.int32, (2 * LANES, 2 * LANES), 1) & (LANES - 1)
    own_ok = jnp.logical_and(krow >= LANES, krow - LANES <= qidx)
    prev_ok = jnp.logical_and(krow < LANES, krow >= qidx)
    low_half = (lax.broadcasted_iota(jnp.int32, (LANES, LANES), 1) & (HEAD_DIM // 2)) == 0

    for g, (_, dil) in enumerate(ATT_GROUPS):
        q_ref, k_ref, v_ref = in_refs[3 * g:3 * g + 3]
        span = LANES * dil
        kbuf[0:span, :] = jnp.zeros((span, LANES), F32)
        kbuf[span:span + rows, :] = k_ref[...]
        vbuf[0:span, :] = jnp.zeros((span, LANES), F32)
        vbuf[span:span + rows, :] = v_ref[...]

        def rd(ref, start, dil=dil):
            if dil == 1:
                return ref[pl.ds(start, LANES), :]
            return ref[pl.ds(start, LANES, stride=dil), :]

        def wr(ref, start, val, dil=dil):
            if dil == 1:
                ref[pl.ds(start, LANES), :] = val
            else:
                ref[pl.ds(start, LANES, stride=dil), :] = val

        def sub_blocks(p, carry, g=g, dil=dil, span=span, q_ref=q_ref, rd=rd, wr=wr):
            us = [p * ATT_ILP + k for k in range(ATT_ILP)]
            sps = [u // dil for u in us]
            starts = [sp * span + (u % dil) for u, sp in zip(us, sps)]
            scs, vsts = [], []
            for sp, start in zip(sps, starts):
                q2 = rd(q_ref, start)
                qs = jnp.concatenate([jnp.where(low_half, q2, 0.0), jnp.where(low_half, 0.0, q2)],
                                     axis=0).astype(BF16)
                ks = jnp.concatenate([rd(kbuf, start), rd(kbuf, span + start)], axis=0).astype(BF16)
                mask = jnp.logical_or(own_ok, jnp.logical_and(prev_ok, sp > 0))
                scs.append(jnp.where(mask, _nt(ks, qs), NEG))
            for start in starts:
                vsts.append(jnp.concatenate([rd(vbuf, start), rd(vbuf, span + start)], axis=0).T.astype(BF16))
            es, dens, lses = [], [], []
            for sc in scs:
                m = jnp.max(sc, axis=0, keepdims=True)
                e = jnp.exp(sc - m)
                den = jnp.sum(e, axis=0, keepdims=True)
                es.append(e.astype(BF16))
                dens.append(den)
                lses.append(m + jnp.log(den))
            ots = [jnp.dot(vst, e, preferred_element_type=F32) for vst, e in zip(vsts, es)]
            for start, ot, den, lse in zip(starts, ots, dens, lses):
                ot = jnp.concatenate([ot[:HEAD_DIM, :LANES] / den[:, :LANES],
                                      ot[HEAD_DIM:, LANES:] / den[:, LANES:]], axis=0)
                lt = jnp.concatenate([jnp.broadcast_to(lse[:, :LANES], (HEAD_DIM, LANES)),
                                      jnp.broadcast_to(lse[:, LANES:], (HEAD_DIM, LANES))], axis=0)
                o2 = ot.T
                l2 = lt.T
                if g > 0:
                    o_prev = rd(o_ref, start)
                    l_prev = rd(lbuf, start)
                    mm = jnp.maximum(l_prev, l2)
                    wp = jnp.exp(l_prev - mm)
                    wc = jnp.exp(l2 - mm)
                    tot = wp + wc
                    o2 = (wp * o_prev + wc * o2) / tot
                    l2 = mm + jnp.log(tot)
                wr(o_ref, start, o2)
                if g < N_GROUPS - 1:
                    wr(lbuf, start, l2)
            return carry

        lax.fori_loop(0, rows // (LANES * ATT_ILP), sub_blocks, 0)


def _attn_prompt(qkv, b, s):
    max_span = LANES * max(dil for _, dil in ATT_GROUPS)
    assert all(win // dil == LANES for win, dil in ATT_GROUPS)
    assert s <= ATT_ROWS and s % max_span == 0 and s % (LANES * ATT_ILP) == 0
    qkv4 = qkv.reshape(QKV_COLS // LANES, b, s, LANES)
    blk = (None, None, s, LANES)
    in_specs = []
    for g in range(N_GROUPS):
        for which in range(3):
            s0 = which * (Q_COLS // LANES) + g * ATT_SLABS
            in_specs.append(pl.BlockSpec(blk, lambda bb, c, s0=s0: (s0 + c, bb, 0, 0)))
    out = pl.pallas_call(
        _attn_prompt_body,
        out_shape=jax.ShapeDtypeStruct((ATT_SLABS, b, s, LANES), F32),
        grid=(b, ATT_SLABS),
        in_specs=in_specs,
        out_specs=pl.BlockSpec(blk, lambda bb, c: (c, bb, 0, 0)),
        scratch_shapes=[pltpu.VMEM((max_span + s, LANES), F32),
                        pltpu.VMEM((max_span + s, LANES), F32),
                        pltpu.VMEM((s, LANES), F32)],
        compiler_params=pltpu.CompilerParams(dimension_semantics=("parallel", "parallel"),
                                             vmem_limit_bytes=ATT_VMEM),
        name="attn_prompt",
    )(*([qkv4] * (3 * N_GROUPS)))
    return out.reshape(ATT_SLABS, b * s, LANES)


def _split3(a):
    hi = a.astype(BF16)
    r1 = a - hi.astype(F32)
    mid = r1.astype(BF16)
    lo = (r1 - mid.astype(F32)).astype(BF16)
    return hi, mid, lo


def _expand_heads(a, emat):
    return sum(jnp.dot(part, emat, preferred_element_type=F32) for part in _split3(a))


def _ssd_prompt_body(xbc_ref, dt_ref, cw_ref, cb_ref, a_ref, dx_ref, e_ref, ltri_ref,
                     y_ref, st_ref, xext, h_ref, *, n_chunks):
    i = pl.program_id(1)
    lc = n_chunks * CHUNK
    halo = SUBLANES

    @pl.when(i == 0)
    def _():
        xext[0:halo, :] = jnp.zeros((halo, CONV_CH), F32)
        h_ref[...] = jnp.zeros_like(h_ref)

    xext[halo:halo + lc, :] = xbc_ref[...]

    lane_x = lax.broadcasted_iota(jnp.int32, (CHUNK, D_INNER), 1) & (CHUNK - 1)
    row_x = lax.broadcasted_iota(jnp.int32, (CHUNK, D_INNER), 0)
    diag = lane_x == row_x
    tril = row_x >= lane_x
    lane_p = lax.broadcasted_iota(jnp.int32, (CHUNK, LANES), 1)
    low_half = lane_p < CHUNK
    a_row = a_ref[...]
    dx = dx_ref[...]
    emat = e_ref[...]
    ltri = ltri_ref[...]

    for c in range(n_chunks):
        o = c * CHUNK
        conv = cb_ref[...] + xext[halo - 3 + o:halo - 3 + o + CHUNK, :] * cw_ref[0:1, :]
        conv = conv + xext[halo - 2 + o:halo - 2 + o + CHUNK, :] * cw_ref[1:2, :]
        conv = conv + xext[halo - 1 + o:halo - 1 + o + CHUNK, :] * cw_ref[2:3, :]
        conv = conv + xext[halo + o:halo + o + CHUNK, :] * cw_ref[3:4, :]
        xc = conv * jax.nn.sigmoid(conv)
        xs = xc[:, :D_INNER]
        bm = xc[:, D_INNER:D_INNER + SSD_GROUPS * D_STATE]
        cm = xc[:, D_INNER + SSD_GROUPS * D_STATE:]

        dt = dt_ref[o:o + CHUNK, :]
        dtx = _expand_heads(dt, emat)
        acs = sum(jnp.dot(ltri, part, preferred_element_type=F32) for part in _split3(dt * a_row))
        acsx = _expand_heads(acs, emat)
        last = acsx[CHUNK - 1:CHUNK, :]
        xdt = xs * dtx
        xw = (xdt * jnp.exp(last - acsx)).astype(BF16)
        eacs = jnp.exp(acsx)
        cdec = jnp.exp(last)
        rrow = jnp.sum(jnp.where(diag, acsx, 0.0), axis=0, keepdims=True)
        lm = jnp.exp(jnp.where(tril, acsx - rrow, NEG))
        xdtb = xdt.astype(BF16)
        cmb = cm.astype(BF16)

        for g in range(SSD_GROUPS):
            gs = slice(g * SSD_GW, (g + 1) * SSD_GW)
            bg = bm[:, g * D_STATE:(g + 1) * D_STATE]
            bgb = bg.astype(BF16)
            cg = cmb[:, g * D_STATE:(g + 1) * D_STATE]
            cbx = _nt(cg, jnp.concatenate([bgb] * (SSD_GW // CHUNK), axis=0))
            wg = (cbx * lm[:, gs]).astype(BF16)
            ydiag = []
            for pr in range(SSD_GW // LANES):
                xp = xdtb[:, g * SSD_GW + pr * LANES:g * SSD_GW + (pr + 1) * LANES]
                zero = jnp.zeros_like(xp)
                bd = jnp.concatenate([jnp.where(low_half, xp, zero), jnp.where(low_half, zero, xp)], axis=0)
                ydiag.append(jnp.dot(wg[:, pr * LANES:(pr + 1) * LANES], bd, preferred_element_type=F32))
            hp = h_ref[g]
            yoff = jnp.dot(cg, hp.astype(BF16), preferred_element_type=F32) * eacs[:, gs]
            st = jnp.dot(bg.T.astype(BF16), xw[:, gs], preferred_element_type=F32)
            h_ref[g] = hp * cdec[:, gs] + st
            y_ref[o:o + CHUNK, gs] = jnp.concatenate(ydiag, axis=1) + yoff + dx[:, gs] * xs[:, gs]

    xext[0:halo, :] = xext[lc:lc + halo, :]

    @pl.when(i == pl.num_programs(1) - 1)
    def _():
        for g in range(SSD_GROUPS):
            st_ref[g * SSD_GW:(g + 1) * SSD_GW, :] = h_ref[g].T


def _ssd_consts(a_log, d_skip):
    a = -jnp.exp(a_log.astype(F32))
    a_row = jnp.pad(a, (0, LANES - SSD_HEADS))[None, :]
    a_x = jnp.repeat(a, CHUNK)[None, :]
    d_x = jnp.repeat(d_skip.astype(F32), CHUNK)[None, :]
    emat = (jnp.arange(D_INNER)[None, :] // CHUNK == jnp.arange(LANES)[:, None]).astype(BF16)
    ltri = (jnp.arange(CHUNK)[:, None] >= jnp.arange(CHUNK)[None, :]).astype(BF16)
    return a_row, a_x, d_x, emat, ltri


def _ssd_prompt(rest, dt, conv_w, conv_b, a_row, d_x, emat, ltri, b, s, n_chunks=SSD_STEP_CHUNKS):
    lc = n_chunks * CHUNK
    rest_v = rest.reshape(b, s, REST_COLS)
    dt_v = dt.reshape(b, s, LANES)
    const = lambda shape: pl.BlockSpec(shape, lambda bb, i: (0,) * len(shape))
    body = functools.partial(_ssd_prompt_body, n_chunks=n_chunks)
    y, st = pl.pallas_call(
        body,
        out_shape=(jax.ShapeDtypeStruct((b, s, D_INNER), F32),
                   jax.ShapeDtypeStruct((b, D_INNER, D_STATE), F32)),
        grid=(b, s // lc),
        in_specs=[pl.BlockSpec((None, lc, CONV_CH), lambda bb, i: (bb, i, 0)),
                  pl.BlockSpec((None, lc, LANES), lambda bb, i: (bb, i, 0)),
                  const((CONV_W, CONV_CH)), const((1, CONV_CH)), const((1, LANES)),
                  const((1, D_INNER)), const((LANES, D_INNER)), const((CHUNK, CHUNK))],
        out_specs=(pl.BlockSpec((None, lc, D_INNER), lambda bb, i: (bb, i, 0)),
                   pl.BlockSpec((None, D_INNER, D_STATE), lambda bb, i: (bb, 0, 0))),
        scratch_shapes=[pltpu.VMEM((lc + SUBLANES, CONV_CH), F32),
                        pltpu.VMEM((SSD_GROUPS, D_STATE, SSD_GW), F32)],
        compiler_params=_params(("parallel", "arbitrary")),
        name="ssd_prompt",
    )(rest_v, dt_v, conv_w, conv_b, a_row, d_x, emat, ltri)
    return y.reshape(b * s, D_INNER), st


def _attn_step_body(q_ref, kn_ref, vn_ref, c0_ref, c1_ref, c2_ref, o_ref):
    caches = (c0_ref, c1_ref, c2_ref)
    hrow = lax.broadcasted_iota(jnp.int32, (N_HEADS, ATT_W), 0)
    hlane = lax.broadcasted_iota(jnp.int32, (N_HEADS, ATT_W), 1) // HEAD_DIM
    hmask = hrow == hlane
    s_list, sn_list = [], []
    for g, (win, dil) in enumerate(ATT_GROUPS):
        qbd = jnp.where(hmask, jnp.broadcast_to(q_ref[g:g + 1, :], (N_HEADS, ATT_W)), 0.0).astype(BF16)
        s = jnp.dot(qbd, caches[g][0].astype(BF16), preferred_element_type=F32)
        if dil > 1:
            wpos = lax.broadcasted_iota(jnp.int32, s.shape, 1)
            s = jnp.where((wpos & (dil - 1)) == 0, s, NEG)
        s_list.append(s)
        sn_list.append(jnp.sum(qbd.astype(F32) * kn_ref[g:g + 1, :].astype(BF16).astype(F32),
                               axis=1, keepdims=True))
    m = sn_list[0]
    for g in range(N_GROUPS):
        m = jnp.maximum(m, jnp.maximum(jnp.max(s_list[g], axis=1, keepdims=True), sn_list[g]))
    den = jnp.zeros((N_HEADS, 1), F32)
    acc = jnp.zeros((N_HEADS, ATT_W), F32)
    for g in range(N_GROUPS):
        e = jnp.exp(s_list[g] - m)
        en = jnp.exp(sn_list[g] - m)
        den = den + jnp.sum(e, axis=1, keepdims=True) + en
        acc = acc + _nt(e.astype(BF16), caches[g][1].astype(BF16))
        acc = acc + en.astype(BF16).astype(F32) * vn_ref[g:g + 1, :].astype(BF16).astype(F32)
    o_ref[...] = jnp.sum(jnp.where(hmask, acc / den, 0.0), axis=0, keepdims=True)


def _attn_step(q, kn, vn, caches):
    nb = q.shape[0]
    views, specs = [], []
    for (win, dil), c in zip(ATT_GROUPS, caches):
        w = c.shape[1]
        assert w == win, "cache must hold exactly one window of past rows"
        views.append(c.transpose(0, 2, 3, 4, 1).reshape(nb, 2, ATT_W, w))
        specs.append(pl.BlockSpec((None, 2, ATT_W, w), lambda i: (i, 0, 0, 0)))
    qspec = pl.BlockSpec((None, N_GROUPS, ATT_W), lambda i: (i, 0, 0))
    out = pl.pallas_call(
        _attn_step_body,
        out_shape=jax.ShapeDtypeStruct((nb, 1, ATT_W), F32),
        grid=(nb,),
        in_specs=[qspec, qspec, qspec, *specs],
        out_specs=pl.BlockSpec((None, 1, ATT_W), lambda i: (i, 0, 0)),
        compiler_params=_params(("parallel",)),
        name="attn_step",
    )(q, kn, vn, *views)
    return out.reshape(nb, ATT_W)


def _conv_step_body(rest_ref, sc_ref, dt_ref, cw_ref, cb_ref, ax_ref, e_ref,
                    xs_ref, bm_ref, cm_ref, xdt_ref, dec_ref):
    conv = cb_ref[...] + sc_ref[:, 0:CONV_CH] * cw_ref[0:1, :]
    conv = conv + sc_ref[:, CONV_CH:2 * CONV_CH] * cw_ref[1:2, :]
    conv = conv + sc_ref[:, 2 * CONV_CH:3 * CONV_CH] * cw_ref[2:3, :]
    conv = conv + rest_ref[...] * cw_ref[3:4, :]
    xc = conv * jax.nn.sigmoid(conv)
    xs = xc[:, :D_INNER]
    dtx = _expand_heads(dt_ref[...], e_ref[...])
    xs_ref[...] = xs
    bm_ref[...] = xc[:, D_INNER:D_INNER + SSD_GROUPS * D_STATE]
    cm_ref[...] = xc[:, D_INNER + SSD_GROUPS * D_STATE:]
    xdt_ref[...] = xs * dtx
    dec_ref[...] = jnp.exp(dtx * ax_ref[...])


def _state_step_body(h_ref, xdt_ref, dec_ref, b_ref, c_ref, xs_ref, dx_ref, ho_ref, y_ref, *, bb):
    grow = lax.broadcasted_iota(jnp.int32, (SSD_GROUPS, D_INNER), 0)
    glane = lax.broadcasted_iota(jnp.int32, (SSD_GROUPS, D_INNER), 1) // SSD_GW
    gmask = grow == glane
    for bi in range(bb):
        h = h_ref[bi]
        bmat = b_ref[bi]
        bx = jnp.concatenate([jnp.broadcast_to(bmat[g:g + 1, :], (SSD_GW, D_STATE))
                              for g in range(SSD_GROUPS)], axis=0)
        hn = h * dec_ref[:, bi:bi + 1] + xdt_ref[:, bi:bi + 1] * bx
        ho_ref[bi] = hn
        y8 = _nt(c_ref[bi].astype(BF16), hn.astype(BF16))
        y = jnp.sum(jnp.where(gmask, y8, 0.0), axis=0, keepdims=True)
        y_ref[bi:bi + 1, :] = y + dx_ref[...] * xs_ref[bi:bi + 1, :]


def _ssd_step(rest, dt, state_conv, state_ssm, conv_w, conv_b, a_x, d_x, emat, bb=4):
    nb = rest.shape[0]
    nblk = nb // bb
    full = lambda shape: pl.BlockSpec(shape, lambda i: (0,) * len(shape))
    xs, bm, cm, xdt, dec = pl.pallas_call(
        _conv_step_body,
        out_shape=(jax.ShapeDtypeStruct((nb, D_INNER), F32),
                   jax.ShapeDtypeStruct((nb, SSD_GROUPS * D_STATE), F32),
                   jax.ShapeDtypeStruct((nb, SSD_GROUPS * D_STATE), F32),
                   jax.ShapeDtypeStruct((nb, D_INNER), F32),
                   jax.ShapeDtypeStruct((nb, D_INNER), F32)),
        grid=(1,),
        in_specs=[pl.BlockSpec((nb, CONV_CH), lambda i: (0, 0)),
                  full((nb, (CONV_W - 1) * CONV_CH)), full((nb, LANES)),
                  full((CONV_W, CONV_CH)), full((1, CONV_CH)), full((1, D_INNER)),
                  full((LANES, D_INNER))],
        out_specs=(full((nb, D_INNER)), full((nb, SSD_GROUPS * D_STATE)),
                   full((nb, SSD_GROUPS * D_STATE)), full((nb, D_INNER)), full((nb, D_INNER))),
        compiler_params=_params(("arbitrary",)),
        name="conv_step",
    )(rest, state_conv.reshape(nb, (CONV_W - 1) * CONV_CH), dt, conv_w, conv_b, a_x, emat)

    to_cols = lambda a: a.reshape(nblk, bb, D_INNER).transpose(0, 2, 1)
    blk3 = lambda d1, d2: pl.BlockSpec((None, d1, d2), lambda i: (i, 0, 0))
    hspec = pl.BlockSpec((bb, D_INNER, D_STATE), lambda i: (i, 0, 0))
    gspec = pl.BlockSpec((bb, SSD_GROUPS, D_STATE), lambda i: (i, 0, 0))
    h_new, y = pl.pallas_call(
        functools.partial(_state_step_body, bb=bb),
        out_shape=(jax.ShapeDtypeStruct((nb, D_INNER, D_STATE), F32),
                   jax.ShapeDtypeStruct((nblk, bb, D_INNER), F32)),
        grid=(nblk,),
        in_specs=[hspec, blk3(D_INNER, bb), blk3(D_INNER, bb), gspec, gspec, blk3(bb, D_INNER),
                  pl.BlockSpec((1, D_INNER), lambda i: (0, 0))],
        out_specs=(hspec, blk3(bb, D_INNER)),
        compiler_params=_params(("parallel",)),
        name="state_step",
    )(state_ssm.reshape(nb, D_INNER, D_STATE), to_cols(xdt), to_cols(dec),
      bm.reshape(nb, SSD_GROUPS, D_STATE), cm.reshape(nb, SSD_GROUPS, D_STATE),
      xs.reshape(nblk, bb, D_INNER), d_x)
    return y.reshape(nb, D_INNER), h_new


def _merge_body(x_ref, att_ref, y_ref, z_ref, ga_ref, gb_ref, gn_ref, wa_ref, wb_ref, wo_ref, o_ref):
    att = jnp.concatenate([att_ref[c] for c in range(ATT_SLABS)], axis=1)
    out_a = jnp.dot(att.astype(BF16), wa_ref[...], preferred_element_type=F32)
    z = z_ref[...]
    y = y_ref[...] * (z * jax.nn.sigmoid(z))
    yn = _rmsnorm_rows(y, gn_ref[...]).astype(BF16)
    out_b = jnp.dot(yn, wb_ref[...], preferred_element_type=F32)
    merged = jax.nn.sigmoid(ga_ref[...]) * out_a + jax.nn.sigmoid(gb_ref[...]) * out_b
    o_ref[...] = x_ref[...] + jnp.dot(merged.astype(BF16), wo_ref[...], preferred_element_type=F32)


def _merge(x2d, att, y, rest, ssd_norm, w_att_out, w_ssd_out, w_out, tm=512):
    t = x2d.shape[0]
    tm = min(tm, t)
    row = lambda w, cb: pl.BlockSpec((tm, w), lambda i: (i, cb))
    once = lambda shape: pl.BlockSpec(shape, lambda i: (0, 0), pipeline_mode=pl.Buffered(1))
    return pl.pallas_call(
        _merge_body,
        out_shape=jax.ShapeDtypeStruct((t, D_MODEL), F32),
        grid=(t // tm,),
        in_specs=[row(D_MODEL, 0),
                  pl.BlockSpec((ATT_SLABS, tm, LANES), lambda i: (0, i, 0)),
                  row(D_INNER, 0),
                  row(D_INNER, REST_Z // D_INNER),
                  row(D_MODEL, REST_GATES // D_MODEL), row(D_MODEL, REST_GATES // D_MODEL + 1),
                  once((1, D_INNER)), once((ATT_W, D_MODEL)), once((D_INNER, D_MODEL)),
                  once((D_MODEL, D_MODEL))],
        out_specs=row(D_MODEL, 0),
        compiler_params=_params(("parallel",)),
        name="merge",
    )(x2d, att, y, rest, rest, rest, ssd_norm, w_att_out, w_ssd_out, w_out)


I1_BLOCK = SUBLANES
BF16_ROWS = 2 * SUBLANES
PEER_GATE_TT = 512
PEER_DENSE_TT = 1024
PEER_DENSE_VMEM = VMEM_BYTES_V7X * 7 // 8


def _gelu_tanh(x):
    return 0.5 * x * (1.0 + jnp.tanh(math.sqrt(2.0 / math.pi) * (x + 0.044715 * (x * x * x))))


def _peer_gate_body(x_ref, g_ref, wq_ref, k1_ref, k2_ref,
                    xnt_ref, r2_ref, e2_ref, n1_ref, w1_ref, s_all, v_all, rk1, *, tg):
    nch = tg // LANES
    nkb = N_KEYS // I1_BLOCK
    xn = _rmsnorm_rows(x_ref[...], g_ref[...])
    xnt = xn.T.astype(BF16)
    xnt_ref[...] = xnt
    qt = jnp.dot(wq_ref[...], xnt, preferred_element_type=F32)
    for h in range(PEER_HEADS):
        for sd, kref in enumerate((k1_ref, k2_ref)):
            r0 = (2 * h + sd) * N_KEYS
            s_all[2 * h + sd] = jnp.dot(kref[h], qt[r0:r0 + N_KEYS, :].astype(BF16),
                                        preferred_element_type=F32)

    rowid = lax.broadcasted_iota(jnp.int32, (N_KEYS, LANES), 0).astype(F32)
    row16 = lax.broadcasted_iota(jnp.int32, (PEER_TOPK, LANES), 0)
    row16f = row16.astype(F32)

    def make_extract(exact):
        def extract(idx, ties):
            h = idx // nch
            off = pl.multiple_of((idx % nch) * LANES, LANES)
            s = [s_all[2 * h + sd, :, pl.ds(off, LANES)] for sd in range(2)]
            rank = [jnp.full((N_KEYS, LANES), float(PEER_TOPK), F32) for _ in range(2)]
            vals = [jnp.zeros((PEER_TOPK, LANES), F32) for _ in range(2)]
            for k in range(PEER_TOPK):
                for sd in range(2):
                    m = jnp.max(s[sd], axis=0, keepdims=True)
                    sel = s[sd] == m
                    if exact:
                        first = jnp.min(jnp.where(sel, rowid, float(N_KEYS)), axis=0, keepdims=True)
                        sel = rowid == first
                    rank[sd] = jnp.where(sel, float(k), rank[sd])
                    s[sd] = jnp.where(sel, -jnp.inf, s[sd])
                    vals[sd] = jnp.where(row16 == k, m, vals[sd])
            rk1[h, :, pl.ds(off, LANES)] = rank[0]
            r2_ref[h, :, :, pl.ds(off, LANES)] = rank[1].reshape(N_KEYS // BF16_ROWS, BF16_ROWS, LANES).astype(BF16)
            v_all[2 * h, :, pl.ds(off, LANES)] = vals[0]
            v_all[2 * h + 1, :, pl.ds(off, LANES)] = vals[1]
            if not exact:
                for sd in range(2):
                    taken = jnp.sum(jnp.where(rank[sd] < float(PEER_TOPK), 1.0, 0.0), axis=0, keepdims=True)
                    ties = jnp.maximum(ties, jnp.where(taken != float(PEER_TOPK), 1.0, 0.0))
            return ties
        return extract

    no_ties = jnp.zeros((1, LANES), F32)
    ties = lax.fori_loop(0, PEER_HEADS * nch, make_extract(False), no_ties)

    @pl.when(jnp.max(ties) > 0.0)
    def _():
        lax.fori_loop(0, PEER_HEADS * nch, make_extract(True), no_ties)

    def finish(h, carry):
        for c in range(nch):
            lanes = slice(c * LANES, (c + 1) * LANES)
            v1 = v_all[2 * h, :, lanes]
            v2 = v_all[2 * h + 1, :, lanes]
            shifted = [jnp.broadcast_to(v2[b:b + 1, :], (PEER_TOPK, LANES)) for b in range(PEER_TOPK)]
            cnt = jnp.zeros((PEER_TOPK, LANES), F32)
            zsum = jnp.zeros((1, LANES), F32)
            top = v1[0:1, :] + v2[0:1, :]
            for step in range(PEER_TOPK):
                front = v1 + shifted[0]
                m = jnp.max(front, axis=0, keepdims=True)
                first = jnp.min(jnp.where(front == m, row16f, float(PEER_TOPK)), axis=0, keepdims=True)
                sel = row16f == first
                cnt = cnt + jnp.where(sel, 1.0, 0.0)
                zsum = zsum + jnp.exp(m - top)
                live = PEER_TOPK - 1 - step
                for b in range(live):
                    shifted[b] = jnp.where(sel, shifted[b + 1], shifted[b])
            rz = 1.0 / zsum
            r1 = rk1[h, :, lanes]
            s1 = s_all[2 * h, :, lanes]
            s2 = s_all[2 * h + 1, :, lanes]
            n_of = jnp.zeros((N_KEYS, LANES), F32)
            for a in range(PEER_TOPK):
                n_of = jnp.where(r1 == float(a), cnt[a:a + 1, :], n_of)
            wgt = jnp.exp(s1 - v1[0:1, :]) * rz
            e2v = jnp.exp(s2 - v2[0:1, :])
            e2_ref[h, :, :, lanes] = e2v.reshape(N_KEYS // BF16_ROWS, BF16_ROWS, LANES).astype(BF16)
            for kb in range(nkb):
                rs = slice(kb * I1_BLOCK, (kb + 1) * I1_BLOCK)
                n1_ref[kb, h, :, lanes] = n_of[rs, :]
                w1_ref[kb, h, :, lanes] = wgt[rs, :]
        return carry

    lax.fori_loop(0, PEER_HEADS, finish, 0)


def _peer_dense_body(x_ref, xnt_ref, r2_ref, e2_ref, n1_ref, w1_ref, u_ref, vt_ref, o_ref,
                     acct, act, hbuf, *, tt):
    j = pl.program_id(1)
    nch = tt // LANES
    half = I1_BLOCK // 2 * N_KEYS
    ktiles = N_KEYS // BF16_ROWS

    @pl.when(j == 0)
    def _():
        acct[...] = jnp.zeros_like(acct)

    xnt = xnt_ref[...]
    for hf in range(2):
        act[hf * half:(hf + 1) * half, :] = jnp.dot(u_ref[hf * half:(hf + 1) * half, :], xnt,
                                                    preferred_element_type=F32)
    for hf in range(2):
        for ii in range(hf * I1_BLOCK // 2, (hf + 1) * I1_BLOCK // 2):
            for c in range(nch):
                lanes = slice(c * LANES, (c + 1) * LANES)
                gate = None
                for h in range(PEER_HEADS):
                    n16 = jnp.broadcast_to(n1_ref[h, ii:ii + 1, lanes], (BF16_ROWS, LANES)).astype(BF16)
                    w16 = jnp.broadcast_to(w1_ref[h, ii:ii + 1, lanes], (BF16_ROWS, LANES)).astype(BF16)
                    prod = e2_ref[h, :, :, lanes] * w16[None]
                    term = jnp.where(r2_ref[h, :, :, lanes] < n16[None], prod, jnp.zeros_like(prod))
                    gate = term if gate is None else gate + term
                a = act[ii * N_KEYS:(ii + 1) * N_KEYS, lanes]
                hv = gate.astype(F32).reshape(N_KEYS, LANES) * _gelu_tanh(a)
                hbuf[ii * N_KEYS:(ii + 1) * N_KEYS, lanes] = hv.astype(BF16)
        acct[...] += jnp.dot(vt_ref[:, hf * half:(hf + 1) * half], hbuf[hf * half:(hf + 1) * half, :],
                             preferred_element_type=F32)

    @pl.when(j == pl.num_programs(1) - 1)
    def _():
        o_ref[...] = x_ref[...] + acct[...].T


def _peer(x2d, norm_ffn, wq_t, keys1, keys2, u_bf, vt_bf):
    t = x2d.shape[0]
    tg = min(PEER_GATE_TT, t)
    tt = min(PEER_DENSE_TT, t)
    nkb = N_KEYS // I1_BLOCK
    eb = I1_BLOCK * N_KEYS
    ktiles = N_KEYS // BF16_ROWS
    full1 = lambda shape: pl.BlockSpec(shape, lambda i: (0,) * len(shape))
    tab_shape = jax.ShapeDtypeStruct((PEER_HEADS, ktiles, BF16_ROWS, t), BF16)
    row_shape = jax.ShapeDtypeStruct((nkb, PEER_HEADS, I1_BLOCK, t), F32)
    tab_spec1 = pl.BlockSpec((PEER_HEADS, ktiles, BF16_ROWS, tg), lambda i: (0, 0, 0, i))
    row_spec1 = pl.BlockSpec((nkb, PEER_HEADS, I1_BLOCK, tg), lambda i: (0, 0, 0, i))
    xnt, r2, e2, n1, w1 = pl.pallas_call(
        functools.partial(_peer_gate_body, tg=tg),
        out_shape=(jax.ShapeDtypeStruct((D_MODEL, t), BF16), tab_shape, tab_shape, row_shape, row_shape),
        grid=(t // tg,),
        in_specs=[pl.BlockSpec((tg, D_MODEL), lambda i: (i, 0)),
                  full1((1, D_MODEL)), full1((2 * PEER_HEADS * N_KEYS, D_MODEL)),
                  full1((PEER_HEADS, N_KEYS, N_KEYS)), full1((PEER_HEADS, N_KEYS, N_KEYS))],
        out_specs=(pl.BlockSpec((D_MODEL, tg), lambda i: (0, i)), tab_spec1, tab_spec1, row_spec1, row_spec1),
        scratch_shapes=[pltpu.VMEM((2 * PEER_HEADS, N_KEYS, tg), F32),
                        pltpu.VMEM((2 * PEER_HEADS, PEER_TOPK, tg), F32),
                        pltpu.VMEM((PEER_HEADS, N_KEYS, tg), F32)],
        compiler_params=_params(("parallel",)),
        name="peer_gate",
    )(x2d, norm_ffn, wq_t, keys1, keys2)

    tab_spec = pl.BlockSpec((PEER_HEADS, ktiles, BF16_ROWS, tt), lambda i, j: (0, 0, 0, i))
    row_spec = pl.BlockSpec((None, PEER_HEADS, I1_BLOCK, tt), lambda i, j: (j, 0, 0, i))
    return pl.pallas_call(
        functools.partial(_peer_dense_body, tt=tt),
        out_shape=jax.ShapeDtypeStruct((t, D_MODEL), F32),
        grid=(t // tt, nkb),
        in_specs=[pl.BlockSpec((tt, D_MODEL), lambda i, j: (i, 0)),
                  pl.BlockSpec((D_MODEL, tt), lambda i, j: (0, i)),
                  tab_spec, tab_spec, row_spec, row_spec,
                  pl.BlockSpec((eb, D_MODEL), lambda i, j: (j, 0)),
                  pl.BlockSpec((D_MODEL, eb), lambda i, j: (0, j))],
        out_specs=pl.BlockSpec((tt, D_MODEL), lambda i, j: (i, 0)),
        scratch_shapes=[pltpu.VMEM((D_MODEL, tt), F32),
                        pltpu.VMEM((eb, tt), F32),
                        pltpu.VMEM((eb, tt), BF16)],
        compiler_params=pltpu.CompilerParams(dimension_semantics=("parallel", "arbitrary"),
                                             vmem_limit_bytes=PEER_DENSE_VMEM),
        name="peer_dense",
    )(x2d, xnt, r2, e2, n1, w1, u_bf, vt_bf)


def _ple_body(x_ref, p_ref, gp_ref, wg_ref, wp_ref, gf_ref, o_ref):
    x = x_ref[...]
    xn = _rmsnorm_rows(x, gp_ref[...]).astype(BF16)
    pg = jax.nn.sigmoid(jnp.dot(xn, wg_ref[...], preferred_element_type=F32))
    x3 = x + pg * jnp.dot(p_ref[...].astype(BF16), wp_ref[...], preferred_element_type=F32)
    o_ref[...] = _rmsnorm_rows(x3, gf_ref[...])


def _ple_final(x2d, p2d, norm_ple, w_gate, w_proj, norm_final, tm=512):
    t = x2d.shape[0]
    tm = min(tm, t)
    full = lambda shape: pl.BlockSpec(shape, lambda i: (0, 0))
    return pl.pallas_call(
        _ple_body,
        out_shape=jax.ShapeDtypeStruct((t, D_MODEL), F32),
        grid=(t // tm,),
        in_specs=[pl.BlockSpec((tm, D_MODEL), lambda i: (i, 0)),
                  pl.BlockSpec((tm, PLE_DIM), lambda i: (i, 0)),
                  full((1, D_MODEL)), full((D_MODEL, D_MODEL)), full((PLE_DIM, D_MODEL)),
                  full((1, D_MODEL))],
        out_specs=pl.BlockSpec((tm, D_MODEL), lambda i: (i, 0)),
        compiler_params=_params(("parallel",)),
        name="ple_final",
    )(x2d, p2d, norm_ple, w_gate, w_proj, norm_final)


def _prep_weights(norm_mix, w_in, conv_w, conv_b, dt_bias, a_log, d_skip, ssd_norm, w_att_out,
                  w_ssd_out, w_out, norm_ffn, w_peer_q, peer_keys1, peer_keys2, peer_u, peer_v,
                  norm_ple, w_ple_gate, w_ple_proj, norm_final):
    w = w_in[0]
    c = [0, Q_COLS, 2 * Q_COLS, 3 * Q_COLS, 3 * Q_COLS + D_INNER, 3 * Q_COLS + D_INNER + CONV_CH,
         3 * Q_COLS + D_INNER + CONV_CH + SSD_HEADS]
    wq, wk, wv, wz, wxbc, wdt, wgt = (w[:, c[0]:c[1]], w[:, c[1]:c[2]], w[:, c[2]:c[3]], w[:, c[3]:c[4]],
                                      w[:, c[4]:c[5]], w[:, c[5]:c[6]], w[:, c[6]:])
    row = lambda v: v.reshape(1, -1).astype(F32)
    p = {}
    p["norm_mix"] = row(norm_mix[0])
    p["w_qkv"] = jnp.concatenate([_split_halves(wq * (HEAD_DIM ** -0.5)), _split_halves(wk), wv],
                                 axis=1).astype(BF16)
    p["w_rest"] = jnp.concatenate([wxbc, wz, wgt], axis=1).astype(BF16)
    p["w_dt"] = jnp.pad(wdt, ((0, 0), (0, LANES - SSD_HEADS))).astype(BF16)
    p["dt_bias"] = jnp.pad(row(dt_bias[0]), ((0, 0), (0, LANES - SSD_HEADS)))
    p["conv_w"] = conv_w[0].astype(F32)
    p["conv_b"] = row(conv_b[0])
    p["a_row"], p["a_x"], p["d_x"], p["emat"], p["ltri"] = _ssd_consts(a_log[0], d_skip[0])
    p["ssd_norm"] = row(ssd_norm[0])
    p["w_att_out"] = w_att_out[0].astype(BF16)
    p["w_ssd_out"] = w_ssd_out[0].astype(BF16)
    p["w_out"] = w_out[0].astype(BF16)
    p["norm_ffn"] = row(norm_ffn[0])
    p["wq_t"] = w_peer_q[0].T.astype(BF16)
    p["keys1"] = peer_keys1[0].astype(BF16)
    p["keys2"] = peer_keys2[0].astype(BF16)
    p["u"] = peer_u[0].astype(BF16)
    p["vt"] = peer_v[0].T.astype(BF16)
    p["norm_ple"] = row(norm_ple[0])
    p["w_ple_gate"] = w_ple_gate[0].astype(BF16)
    p["w_ple_proj"] = w_ple_proj[0].astype(BF16)
    p["norm_final"] = row(norm_final)
    return p


def _split_halves(w):
    k, n = w.shape
    return w.reshape(k, n // LANES, 2, 2, HEAD_DIM // 2).transpose(0, 1, 3, 2, 4).reshape(k, n)


def _tail(x2d, att, y, rest, p2d, p):
    x1 = _merge(x2d, att, y, rest, p["ssd_norm"], p["w_att_out"], p["w_ssd_out"], p["w_out"])
    x2 = _peer(x1, p["norm_ffn"], p["wq_t"], p["keys1"], p["keys2"], p["u"], p["vt"])
    return _ple_final(x2, p2d, p["norm_ple"], p["w_ple_gate"], p["w_ple_proj"], p["norm_final"])


def _heads_from_slabs(qkv, which, g, b, s, rows):
    s0 = which * (Q_COLS // LANES) + g * ATT_SLABS
    x = qkv.reshape(QKV_COLS // LANES, b, s, LANES)[s0:s0 + ATT_SLABS, :, s - rows:]
    x = x.transpose(1, 2, 0, 3)
    if which < 2:
        x = x.reshape(b, rows, ATT_SLABS, 2, 2, HEAD_DIM // 2).transpose(0, 1, 2, 4, 3, 5)
    return x.reshape(b, rows, N_HEADS, HEAD_DIM)


def _kv_rows(qkv, g, b, s, rows):
    return jnp.stack([_heads_from_slabs(qkv, 1, g, b, s, rows),
                      _heads_from_slabs(qkv, 2, g, b, s, rows)], axis=2)[None]


def _prompt(x, p_in, p):
    b, s, _ = x.shape
    x2d = x.reshape(b * s, D_MODEL)
    qkv, rest, dt = _projections(x2d, jnp.arange(s, dtype=jnp.int32), p["norm_mix"], p["w_qkv"],
                                 p["w_rest"], p["w_dt"], p["dt_bias"], tm=PROJ_TM)
    att = _attn_prompt(qkv, b, s)
    y, ssm = _ssd_prompt(rest, dt, p["conv_w"], p["conv_b"], p["a_row"], p["d_x"], p["emat"], p["ltri"], b, s)
    out = _tail(x2d, att, y, rest, p_in[0].reshape(b * s, PLE_DIM), p)
    kvs = [_kv_rows(qkv, g, b, s, min(win, s)) for g, (win, _) in enumerate(ATT_GROUPS)]
    conv = rest.reshape(b, s, REST_COLS)[:, s - (CONV_W - 1):, :CONV_CH][None]
    ssm = ssm.reshape(1, b, SSD_HEADS, CHUNK, D_STATE)
    return out.reshape(b, s, D_MODEL), kvs, conv, ssm


def _sample(x, p_in, caches, state_conv, state_ssm, p):
    b, s, _ = x.shape
    assert s == 1
    x2d = x.reshape(b, D_MODEL)
    pos = jnp.full((b,), PAST_LEN, dtype=jnp.int32)
    qkv, rest, dt = _projections(x2d, pos, p["norm_mix"], p["w_qkv"], p["w_rest"], p["w_dt"],
                                 p["dt_bias"], tm=b)
    rows = lambda which: jnp.stack([_heads_from_slabs(qkv, which, g, b, 1, 1).reshape(b, ATT_W)
                                    for g in range(N_GROUPS)], axis=1)
    q3, k3, v3 = rows(0), rows(1), rows(2)
    att = _attn_step(q3, k3, v3, [c[0] for c in caches])
    att = att.reshape(b, ATT_SLABS, LANES).transpose(1, 0, 2)
    xbc = rest[:, :CONV_CH]
    y, ssm = _ssd_step(xbc, dt, state_conv[0], state_ssm[0],
                       p["conv_w"], p["conv_b"], p["a_x"], p["d_x"], p["emat"])
    out = _tail(x2d, att, y, rest, p_in[0].reshape(b, PLE_DIM), p)
    hd = lambda a, g: a[:, g].reshape(b, N_HEADS, HEAD_DIM)
    kvs = [jnp.stack([hd(k3, g), hd(v3, g)], axis=1)[None, :, None] for g in range(N_GROUPS)]
    conv = jnp.concatenate([state_conv[0][:, 1:], xbc[:, None, :]], axis=1)[None]
    ssm = ssm.reshape(1, b, SSD_HEADS, CHUNK, D_STATE)
    return out.reshape(b, 1, D_MODEL), kvs, conv, ssm


def kernel(x_prompt, x_sample, cache_kv_w128, cache_kv_w512, cache_kv_w2048, state_conv, state_ssm, p_prompt, p_sample, norm_mix, w_in, conv_w, conv_b, dt_bias, a_log, d_skip, ssd_norm, w_att_out, w_ssd_out, w_out, norm_ffn, w_peer_q, peer_keys1, peer_keys2, peer_u, peer_v, norm_ple, w_ple_gate, w_ple_proj, norm_final):
    p = _prep_weights(norm_mix, w_in, conv_w, conv_b, dt_bias, a_log, d_skip, ssd_norm, w_att_out,
                      w_ssd_out, w_out, norm_ffn, w_peer_q, peer_keys1, peer_keys2, peer_u, peer_v,
                      norm_ple, w_ple_gate, w_ple_proj, norm_final)
    y_p, kv_p, conv_p, ssm_p = _prompt(x_prompt, p_prompt, p)
    y_s, kv_s, conv_s, ssm_s = _sample(x_sample, p_sample, (cache_kv_w128, cache_kv_w512, cache_kv_w2048),
                                       state_conv, state_ssm, p)
    return (y_p, y_s, kv_p[0], kv_p[1], kv_p[2], conv_p, ssm_p, kv_s[0], kv_s[1], kv_s[2], conv_s, ssm_s)
```
